```python
import jax, jax.numpy as jnp
from jax import lax
import numpy as np

D_MODEL = 2048
BATCH = 4
SEQ = 8192
DEPTH = 1

D_MIX = D_MODEL
CONV_CH = D_MIX // 4
N_HEADS = 12
HEAD_DIM = 128
N_KV = 3
GROUP = N_HEADS // N_KV
CONV_K = 31
CMP_BLOCK = 32
CMP_STRIDE = 16
CMP_HIDDEN = 256
SLC_BLOCK = 64
N_SELECT = 16
WINDOW = 512
Q_BLOCK = 64
D_FF = 5504
ROPE_THETA = 10000.0
EPS = 1e-6
N_IN = 2 * CONV_CH + N_HEADS * HEAD_DIM + 6 * N_KV * HEAD_DIM + 3 * N_HEADS

kernel_name = "hymba_conformer_conv_nsa_macaron"


def _split_points():
    sizes = [CONV_CH, CONV_CH, N_HEADS * HEAD_DIM] + [N_KV * HEAD_DIM] * 6 + [3 * N_HEADS]
    return [int(v) for v in np.cumsum(sizes)[:-1]]


def _rmsnorm(x, g):
    xf = x.astype(jnp.float32)
    y = xf * lax.rsqrt(jnp.mean(xf * xf, axis=-1, keepdims=True) + EPS)
    return (y * g.astype(jnp.float32)).astype(x.dtype)


def _layernorm(x, g, b):
    xf = x.astype(jnp.float32)
    mu = jnp.mean(xf, axis=-1, keepdims=True)
    xc = xf - mu
    var = jnp.mean(xc * xc, axis=-1, keepdims=True)
    return (xc * lax.rsqrt(var + EPS) * g.astype(jnp.float32) + b.astype(jnp.float32)).astype(x.dtype)


def _swiglu(h, w_gate, w_up, w_down):
    return (jax.nn.silu(h @ w_gate) * (h @ w_up)) @ w_down


def _rope_tables(seq):
    inv = jnp.power(ROPE_THETA, -jnp.arange(0, HEAD_DIM, 2, dtype=jnp.float32) / HEAD_DIM)
    ang = jnp.arange(seq, dtype=jnp.float32)[:, None] * inv[None, :]
    return jnp.cos(ang), jnp.sin(ang)


def _rope(x, cos, sin):
    xf = x.astype(jnp.float32)
    x1, x2 = jnp.split(xf, 2, axis=-1)
    c = cos[None, :, None, :]
    s = sin[None, :, None, :]
    return jnp.concatenate([x1 * c - x2 * s, x2 * c + x1 * s], axis=-1).astype(x.dtype)


def _masked_softmax(s, mask):
    s = jnp.where(mask, s, -jnp.inf)
    m = jnp.max(s, axis=-1, keepdims=True)
    m = jnp.where(jnp.isfinite(m), m, 0.0)
    e = jnp.where(mask, jnp.exp(s - m), 0.0)
    return e / jnp.maximum(jnp.sum(e, axis=-1, keepdims=True), 1e-30)


def _cmp_to_slc_matrix(n_cmp, n_slc):
    units_per_cmp = CMP_BLOCK // CMP_STRIDE
    units_per_slc = SLC_BLOCK // CMP_STRIDE
    first_unit = np.arange(n_cmp)[:, None]
    slc = np.arange(n_slc)[None, :]
    m = np.zeros((n_cmp, n_slc), np.float32)
    for u in range(units_per_cmp):
        m += ((first_unit + u) // units_per_slc == slc)
    return jnp.asarray(m)


def _conformer_conv(val, gate, dw_w, dw_b, ln_g, ln_b, pw_w):
    u = val * jax.nn.sigmoid(gate)
    u = lax.conv_general_dilated(
        u, dw_w[:, None, :].astype(u.dtype), window_strides=(1,),
        padding=[(CONV_K - 1, 0)], dimension_numbers=("NWC", "WIO", "NWC"),
        feature_group_count=CONV_CH) + dw_b
    u = _layernorm(u, ln_g, ln_b)
    return jax.nn.silu(u) @ pw_w


def _compress(kv, pos_emb, w1, w2):
    b, s = kv.shape[0], kv.shape[1]
    n_cmp = (s - CMP_BLOCK) // CMP_STRIDE + 1
    idx = np.arange(n_cmp)[:, None] * CMP_STRIDE + np.arange(CMP_BLOCK)[None, :]
    blk = kv[:, idx] + pos_emb[None, None, :, None, :]
    blk = blk.transpose(0, 3, 1, 2, 4).reshape(b, N_KV, n_cmp, CMP_BLOCK * HEAD_DIM)
    return jax.nn.silu(blk @ w1) @ w2


def _nsa(q, k_c, v_c, k_s, v_s, k_w, v_w, gate_logits, pos_k, pos_v, kw1, kw2, vw1, vw2):
    b, s = q.shape[0], q.shape[1]
    cos, sin = _rope_tables(s)
    k_cmp = _compress(k_c, pos_k, kw1, kw2)
    v_cmp = _compress(v_c, pos_v, vw1, vw2)
    n_cmp = k_cmp.shape[2]
    n_slc = s // SLC_BLOCK
    n_sel = min(N_SELECT, n_slc)
    q_raw = q.reshape(b, s, N_KV, GROUP, HEAD_DIM).transpose(0, 2, 3, 1, 4)
    q_rot = _rope(q, cos, sin).reshape(b, s, N_KV, GROUP, HEAD_DIM).transpose(0, 2, 3, 1, 4)
    k_sel = _rope(k_s, cos, sin).transpose(0, 2, 1, 3).reshape(b, N_KV, n_slc, SLC_BLOCK, HEAD_DIM)
    v_sel = v_s.transpose(0, 2, 1, 3).reshape(b, N_KV, n_slc, SLC_BLOCK, HEAD_DIM)
    pad = ((0, 0), (0, 0), (WINDOW, 0), (0, 0))
    k_win = jnp.pad(_rope(k_w, cos, sin).transpose(0, 2, 1, 3), pad)
    v_win = jnp.pad(v_w.transpose(0, 2, 1, 3), pad)
    gates = jax.nn.sigmoid(gate_logits.astype(jnp.float32)).reshape(
        b, s, N_KV, GROUP, 3).transpose(0, 2, 3, 1, 4)
    cmp_end = jnp.arange(n_cmp) * CMP_STRIDE + (CMP_BLOCK - 1)
    cmp_to_slc = _cmp_to_slc_matrix(n_cmp, n_slc)
    blk_ids = jnp.arange(n_slc)
    scale = HEAD_DIM ** -0.5
    gather_blocks = jax.vmap(jax.vmap(lambda blocks, ids: blocks[ids]))

    def block(i):
        q0 = i * Q_BLOCK
        t = q0 + jnp.arange(Q_BLOCK)
        qr = lax.dynamic_slice_in_dim(q_raw, q0, Q_BLOCK, axis=3)
        qp = lax.dynamic_slice_in_dim(q_rot, q0, Q_BLOCK, axis=3)
        g = lax.dynamic_slice_in_dim(gates, q0, Q_BLOCK, axis=3)
        sc = jnp.einsum("bgrqd,bgcd->bgrqc", qr, k_cmp, preferred_element_type=jnp.float32) * scale
        p_cmp = _masked_softmax(sc, cmp_end[None, :] <= t[:, None])
        o_cmp = jnp.einsum("bgrqc,bgcd->bgrqd", p_cmp.astype(v_cmp.dtype), v_cmp)
        imp = jnp.einsum("bgrqc,cs->bgqs", p_cmp, cmp_to_slc)
        jt = (t // SLC_BLOCK)[:, None]
        forced = (blk_ids == 0) | (blk_ids == jt) | (blk_ids == jt - 1)
        imp = jnp.where(forced, jnp.inf, jnp.where(blk_ids <= jt, imp, -jnp.inf))
        top_val, top_idx = lax.top_k(imp, n_sel)
        k_g = gather_blocks(k_sel, top_idx)
        v_g = gather_blocks(v_sel, top_idx)
        ss = jnp.einsum("bgrqd,bgqnkd->bgrqnk", qp, k_g, preferred_element_type=jnp.float32) * scale
        kpos = top_idx[..., None] * SLC_BLOCK + jnp.arange(SLC_BLOCK)
        ms = (kpos <= t[:, None, None]) & (top_val > -jnp.inf)[..., None]
        n_keys = n_sel * SLC_BLOCK
        p_slc = _masked_softmax(ss.reshape(b, N_KV, GROUP, Q_BLOCK, n_keys),
                                ms.reshape(b, N_KV, 1, Q_BLOCK, n_keys))
        o_slc = jnp.einsum("bgrqm,bgqmd->bgrqd", p_slc.astype(v_g.dtype),
                           v_g.reshape(b, N_KV, Q_BLOCK, n_keys, HEAD_DIM))
        kw = lax.dynamic_slice_in_dim(k_win, q0, WINDOW + Q_BLOCK, axis=2)
        vw = lax.dynamic_slice_in_dim(v_win, q0, WINDOW + Q_BLOCK, axis=2)
        kp = q0 - WINDOW + jnp.arange(WINDOW + Q_BLOCK)
        mw = (kp[None, :] <= t[:, None]) & (kp[None, :] > t[:, None] - WINDOW) & (kp[None, :] >= 0)
        sw = jnp.einsum("bgrqd,bgkd->bgrqk", qp, kw, preferred_element_type=jnp.float32) * scale
        p_win = _masked_softmax(sw, mw)
        o_win = jnp.einsum("bgrqk,bgkd->bgrqd", p_win.astype(vw.dtype), vw)
        o = g[..., 0:1] * o_cmp + g[..., 1:2] * o_slc + g[..., 2:3] * o_win
        return o.astype(q.dtype)

    out = lax.map(block, jnp.arange(s // Q_BLOCK))
    return out.transpose(1, 0, 4, 2, 3, 5).reshape(b, s, N_HEADS * HEAD_DIM)


def setup_inputs(seed: int = 0) -> dict:
    key = jax.random.key(seed)
    ks = iter(jax.random.split(key, 32))
    f32 = jnp.float32
    L = DEPTH

    def w(shape, fan_in):
        return jax.random.normal(next(ks), shape, f32) * fan_in ** -0.5

    def gain(shape):
        return 1.0 + 0.02 * jax.random.normal(next(ks), shape, f32)

    def small(shape, sc=0.02):
        return sc * jax.random.normal(next(ks), shape, f32)

    return {
        "x": jax.random.normal(next(ks), (BATCH, SEQ, D_MODEL), f32),
        "ffn1_norm": gain((L, D_MODEL)),
        "ffn1_w_gate": w((L, D_MODEL, D_FF), D_MODEL),
        "ffn1_w_up": w((L, D_MODEL, D_FF), D_MODEL),
        "ffn1_w_down": w((L, D_FF, D_MODEL), D_FF),
        "mix_norm": gain((L, D_MODEL)),
        "w_in": w((L, D_MODEL, N_IN), D_MODEL),
        "cmp_pos_k": small((L, CMP_BLOCK, HEAD_DIM), 0.1),
        "cmp_pos_v": small((L, CMP_BLOCK, HEAD_DIM), 0.1),
        "cmp_k_w1": w((L, CMP_BLOCK * HEAD_DIM, CMP_HIDDEN), CMP_BLOCK * HEAD_DIM),
        "cmp_k_w2": w((L, CMP_HIDDEN, HEAD_DIM), CMP_HIDDEN),
        "cmp_v_w1": w((L, CMP_BLOCK * HEAD_DIM, CMP_HIDDEN), CMP_BLOCK * HEAD_DIM),
        "cmp_v_w2": w((L, CMP_HIDDEN, HEAD_DIM), CMP_HIDDEN),
        "conv_dw_w": w((L, CONV_K, CONV_CH), CONV_K),
        "conv_dw_b": small((L, CONV_CH)),
        "conv_ln_g": gain((L, CONV_CH)),
        "conv_ln_b": small((L, CONV_CH)),
        "conv_pw_w": w((L, CONV_CH, CONV_CH), CONV_CH),
        "out_norm_conv": gain((L, CONV_CH)),
        "out_norm_nsa": gain((L, N_HEADS * HEAD_DIM)),
        "w_out": w((L, D_MIX, D_MODEL), D_MIX),
        "ffn2_norm": gain((L, D_MODEL)),
        "ffn2_w_gate": w((L, D_MODEL, D_FF), D_MODEL),
        "ffn2_w_up": w((L, D_MODEL, D_FF), D_MODEL),
        "ffn2_w_down": w((L, D_FF, D_MODEL), D_FF),
        "final_norm": gain((D_MODEL,)),
    }


def reference(x, ffn1_norm, ffn1_w_gate, ffn1_w_up, ffn1_w_down, mix_norm, w_in,
              cmp_pos_k, cmp_pos_v, cmp_k_w1, cmp_k_w2, cmp_v_w1, cmp_v_w2,
              conv_dw_w, conv_dw_b, conv_ln_g, conv_ln_b, conv_pw_w,
              out_norm_conv, out_norm_nsa, w_out,
              ffn2_norm, ffn2_w_gate, ffn2_w_up, ffn2_w_down, final_norm):
    b, s = x.shape[0], x.shape[1]
    splits = _split_points()
    for l in range(DEPTH):
        h = _rmsnorm(x, ffn1_norm[l])
        x = x + 0.5 * _swiglu(h, ffn1_w_gate[l], ffn1_w_up[l], ffn1_w_down[l])
        h = _rmsnorm(x, mix_norm[l])
        proj = h @ w_in[l]
        c_val, c_gate, q, k_c, v_c, k_s, v_s, k_w, v_w, g_logit = jnp.split(proj, splits, axis=-1)
        conv_out = _conformer_conv(c_val, c_gate, conv_dw_w[l], conv_dw_b[l],
                                   conv_ln_g[l], conv_ln_b[l], conv_pw_w[l])
        kvs = lambda a: a.reshape(b, s, N_KV, HEAD_DIM)
        attn_out = _nsa(q.reshape(b, s, N_HEADS, HEAD_DIM), kvs(k_c), kvs(v_c), kvs(k_s), kvs(v_s),
                        kvs(k_w), kvs(v_w), g_logit, cmp_pos_k[l], cmp_pos_v[l],
                        cmp_k_w1[l], cmp_k_w2[l], cmp_v_w1[l], cmp_v_w2[l])
        mixed = jnp.concatenate([_rmsnorm(conv_out, out_norm_conv[l]),
                                 _rmsnorm(attn_out, out_norm_nsa[l])], axis=-1)
        x = x + mixed @ w_out[l]
        h = _rmsnorm(x, ffn2_norm[l])
        x = x + 0.5 * _swiglu(h, ffn2_w_gate[l], ffn2_w_up[l], ffn2_w_down[l])
    return _rmsnorm(x, final_norm)
```

```python
import functools

import numpy as np
import jax
import jax.numpy as jnp
from jax import lax
from jax.experimental import pallas as pl
from jax.experimental.pallas import tpu as pltpu

F32 = jnp.float32
BF16 = jnp.bfloat16

V7X_LANES = 128
V7X_VMEM_BYTES = 64 * 2 ** 20

CONV_CH = 512
N_HEADS = 12
HEAD_DIM = 128
N_KV = 3
GROUP = N_HEADS // N_KV
CONV_K = 31
CMP_BLOCK = 32
CMP_STRIDE = 16
CMP_HIDDEN = 256
SLC_BLOCK = 64
N_SELECT = 16
N_FORCED = 3
WINDOW = 512
ROPE_THETA = 10000.0
EPS = 1e-6
MASKED = -1e30

KV_DIM = N_KV * HEAD_DIM
Q_DIM = N_HEADS * HEAD_DIM
CONV_HALO = 32

_NT = (((1,), (1,)), ((), ()))


def _rms(x, g):
    return x * lax.rsqrt(jnp.mean(x * x, axis=-1, keepdims=True) + EPS) * g


def _params(semantics, vmem_mib):
    return pltpu.CompilerParams(dimension_semantics=semantics, vmem_limit_bytes=vmem_mib * 2 ** 20)


def _resident(shape, index_map):
    return pl.BlockSpec(shape, index_map, pipeline_mode=pl.Buffered(1))


def _ffn_body(*refs, n_f, final_norm):
    if final_norm:
        x_ref, g_ref, wg_ref, wu_ref, wd_ref, fg_ref, o_ref, h_scr = refs
    else:
        x_ref, g_ref, wg_ref, wu_ref, wd_ref, o_ref, h_scr = refs
    j = pl.program_id(1)

    @pl.when(j == 0)
    def _():
        x = x_ref[...]
        h_scr[...] = _rms(x, g_ref[...]).astype(BF16)
        o_ref[...] = x

    h = h_scr[...]
    a = jnp.dot(h, wg_ref[...], preferred_element_type=F32)
    b = jnp.dot(h, wu_ref[...], preferred_element_type=F32)
    z = (a * jax.nn.sigmoid(a) * b).astype(BF16)
    o_ref[...] += 0.5 * jnp.dot(z, wd_ref[...], preferred_element_type=F32)

    if final_norm:
        @pl.when(j == n_f - 1)
        def _():
            o_ref[...] = _rms(o_ref[...], fg_ref[...])


def _ffn(x, g, w_gate, w_up, w_down, final_g=None):
    t, d = x.shape
    f = w_gate.shape[1]
    tm = min(512, t)
    tf = 512
    fp = -(-f // tf) * tf
    wg = jnp.pad(w_gate.astype(BF16), ((0, 0), (0, fp - f)))
    wu = jnp.pad(w_up.astype(BF16), ((0, 0), (0, fp - f)))
    wd = jnp.pad(w_down.astype(BF16), ((0, fp - f), (0, 0)))
    n_f = fp // tf
    final_norm = final_g is not None
    row = pl.BlockSpec((tm, d), lambda i, j: (i, 0))
    vec = pl.BlockSpec((1, d), lambda i, j: (0, 0))
    in_specs = [row, vec,
                pl.BlockSpec((d, tf), lambda i, j: (0, j)),
                pl.BlockSpec((d, tf), lambda i, j: (0, j)),
                pl.BlockSpec((tf, d), lambda i, j: (j, 0))]
    args = [x, g.reshape(1, d), wg, wu, wd]
    if final_norm:
        in_specs.append(vec)
        args.append(final_g.reshape(1, d))
    return pl.pallas_call(
        functools.partial(_ffn_body, n_f=n_f, final_norm=final_norm),
        grid=(t // tm, n_f),
        in_specs=in_specs,
        out_specs=row,
        out_shape=jax.ShapeDtypeStruct((t, d), F32),
        scratch_shapes=[pltpu.VMEM((tm, d), BF16)],
        compiler_params=_params(("parallel", "arbitrary"), 48),
        name="ffn_final" if final_norm else "ffn",
    )(*args)


def _rope(x, cos2, sin2):
    return x * cos2 + pltpu.roll(x, HEAD_DIM // 2, 1) * sin2


def _in_proj_body(x_ref, g_ref, w_ref, cos_ref, sin_ref,
                  u_ref, qraw_ref, qrot_ref, kc_ref, vc_ref, ks_ref, vs_ref, kw_ref, vw_ref, gate_ref):
    h = _rms(x_ref[...], g_ref[...]).astype(BF16)
    cos2 = cos_ref[...]
    sin2 = sin_ref[...]
    scale = HEAD_DIM ** -0.5

    def proj(c0, width):
        return jnp.dot(h, w_ref[:, c0:c0 + width], preferred_element_type=F32)

    glu = proj(0, 2 * CONV_CH)
    u_ref[...] = glu[:, :CONV_CH] * jax.nn.sigmoid(glu[:, CONV_CH:])

    c0 = 2 * CONV_CH
    for hd in range(N_HEADS):
        qh = proj(c0 + hd * HEAD_DIM, HEAD_DIM) * scale
        qraw_ref[0, hd] = qh.astype(BF16)
        qrot_ref[0, hd] = _rope(qh, cos2, sin2).astype(BF16)
    c0 += Q_DIM
    for ref, roped in ((kc_ref, False), (vc_ref, False), (ks_ref, True), (vs_ref, False),
                       (kw_ref, True), (vw_ref, False)):
        for gk in range(N_KV):
            kv = proj(c0 + gk * HEAD_DIM, HEAD_DIM)
            if roped:
                kv = _rope(kv, cos2, sin2)
            ref[0, gk] = kv.astype(ref.dtype)
        c0 += KV_DIM
    for gk in range(N_KV):
        gate_ref[0, gk] = jax.nn.sigmoid(proj(c0 + gk * V7X_LANES, V7X_LANES))


def _in_proj(x, g, w_in, batch, seq):
    t, d = x.shape
    tm = min(256, seq)
    n_s = seq // tm
    main = 2 * CONV_CH + Q_DIM + 6 * KV_DIM
    gate_w = w_in[:, main:].reshape(d, N_KV, GROUP * 3)
    gate_w = jnp.pad(gate_w, ((0, 0), (0, 0), (0, V7X_LANES - GROUP * 3))).reshape(d, N_KV * V7X_LANES)
    w = jnp.concatenate([w_in[:, :main], gate_w], axis=1).astype(BF16)
    n_w = w.shape[1]

    inv = jnp.power(ROPE_THETA, -jnp.arange(0, HEAD_DIM, 2, dtype=F32) / HEAD_DIM)
    ang = jnp.arange(seq, dtype=F32)[:, None] * inv[None, :]
    cos2 = jnp.concatenate([jnp.cos(ang), jnp.cos(ang)], axis=1)
    sin2 = jnp.concatenate([-jnp.sin(ang), jnp.sin(ang)], axis=1)

    def heads(n, dtype):
        return (jax.ShapeDtypeStruct((batch, n, seq, HEAD_DIM), dtype),
                pl.BlockSpec((1, n, tm, HEAD_DIM), lambda i: (i // n_s, 0, i % n_s, 0)))

    outs = [(jax.ShapeDtypeStruct((t, CONV_CH), F32), pl.BlockSpec((tm, CONV_CH), lambda i: (i, 0))),
            heads(N_HEADS, BF16), heads(N_HEADS, BF16),
            heads(N_KV, F32), heads(N_KV, F32),
            heads(N_KV, BF16), heads(N_KV, BF16), heads(N_KV, BF16), heads(N_KV, BF16),
            heads(N_KV, F32)]
    table = pl.BlockSpec((tm, HEAD_DIM), lambda i: (i % n_s, 0))
    return pl.pallas_call(
        _in_proj_body,
        grid=(t // tm,),
        in_specs=[pl.BlockSpec((tm, d), lambda i: (i, 0)),
                  pl.BlockSpec((1, d), lambda i: (0, 0)),
                  _resident((d, n_w), lambda i: (0, 0)),
                  table, table],
        out_specs=[o[1] for o in outs],
        out_shape=[o[0] for o in outs],
        compiler_params=_params(("parallel",), 48),
        name="in_proj",
    )(x, g.reshape(1, d), w, cos2, sin2)


CONV_ROWS = 32


def _conv_body(u_ref, halo_ref, dw_ref, db_ref, lg_ref, lb_ref, pw_ref, og_ref, o_ref, ext_scr, y_scr, *, ts):
    i = pl.program_id(1)
    ext_scr[0:CONV_HALO, :] = jnp.where(i == 0, 0.0, halo_ref[0])
    ext_scr[CONV_HALO:CONV_HALO + ts, :] = u_ref[0]
    first = CONV_HALO - (CONV_K - 1)
    for c in range(ts // CONV_ROWS):
        r0 = c * CONV_ROWS
        acc = jnp.broadcast_to(db_ref[...], (CONV_ROWS, CONV_CH))
        for k in range(CONV_K):
            acc = acc + dw_ref[k:k + 1, :] * ext_scr[r0 + first + k:r0 + first + k + CONV_ROWS, :]
        mu = jnp.mean(acc, axis=-1, keepdims=True)
        xc = acc - mu
        var = jnp.mean(xc * xc, axis=-1, keepdims=True)
        y = xc * lax.rsqrt(var + EPS) * lg_ref[...] + lb_ref[...]
        y_scr[r0:r0 + CONV_ROWS, :] = (y * jax.nn.sigmoid(y)).astype(BF16)
    z = jnp.dot(y_scr[...], pw_ref[...], preferred_element_type=F32)
    o_ref[0] = _rms(z, og_ref[...]).astype(BF16)


def _conv(u, dw_w, dw_b, ln_g, ln_b, pw_w, out_g):
    b, s, c = u.shape
    ts = min(256, s)
    per = ts // CONV_HALO
    vec = pl.BlockSpec((1, c), lambda bi, i: (0, 0))
    return pl.pallas_call(
        functools.partial(_conv_body, ts=ts),
        grid=(b, s // ts),
        in_specs=[pl.BlockSpec((1, ts, c), lambda bi, i: (bi, i, 0)),
                  pl.BlockSpec((1, CONV_HALO, c), lambda bi, i: (bi, jnp.maximum(i * per - 1, 0), 0)),
                  pl.BlockSpec((CONV_K, c), lambda bi, i: (0, 0)),
                  vec, vec, vec,
                  pl.BlockSpec((c, c), lambda bi, i: (0, 0)),
                  vec],
        out_specs=pl.BlockSpec((1, ts, c), lambda bi, i: (bi, i, 0)),
        out_shape=jax.ShapeDtypeStruct((b, s, c), BF16),
        scratch_shapes=[pltpu.VMEM((CONV_HALO + ts, c), F32), pltpu.VMEM((ts, c), BF16)],
        compiler_params=_params(("parallel", "parallel"), 32),
        name="conv",
    )(u, u, dw_w, dw_b.reshape(1, c), ln_g.reshape(1, c), ln_b.reshape(1, c), pw_w.astype(BF16),
      out_g.reshape(1, c))


def _compress_one(u_ref, pos_ref, w1_ref, w2_ref, o_ref):
    u = u_ref[0, 0]
    half = u.shape[1]
    nu = u.shape[0]
    top = jnp.dot((u + pos_ref[0:1, :]).astype(BF16), w1_ref[0:half, :], preferred_element_type=F32)
    bot = jnp.dot((u + pos_ref[1:2, :]).astype(BF16), w1_ref[half:2 * half, :], preferred_element_type=F32)
    hid = top + pltpu.roll(bot, nu - 1, 0)
    hid = hid * jax.nn.sigmoid(hid)
    o_ref[0, 0] = jnp.dot(hid.astype(BF16), w2_ref[...], preferred_element_type=F32).astype(BF16)


def _compress_body(uk_ref, uv_ref, pk_ref, pv_ref, kw1_ref, kw2_ref, vw1_ref, vw2_ref, ok_ref, ov_ref):
    _compress_one(uk_ref, pk_ref, kw1_ref, kw2_ref, ok_ref)
    _compress_one(uv_ref, pv_ref, vw1_ref, vw2_ref, ov_ref)


def _compress(kc, vc, pos_k, pos_v, kw1, kw2, vw1, vw2):
    b, g, s, dh = kc.shape
    nu = s // CMP_STRIDE
    unit = CMP_STRIDE * dh
    units = lambda a: a.reshape(b, g, nu, unit)
    pos = lambda p: p.reshape(CMP_BLOCK // CMP_STRIDE, unit)
    u_spec = pl.BlockSpec((1, 1, nu, unit), lambda bi, gi: (bi, gi, 0, 0))
    full = lambda shape: pl.BlockSpec(shape, lambda bi, gi: (0,) * len(shape))
    o_spec = pl.BlockSpec((1, 1, nu, dh), lambda bi, gi: (bi, gi, 0, 0))
    o_shape = jax.ShapeDtypeStruct((b, g, nu, dh), BF16)
    return pl.pallas_call(
        _compress_body,
        grid=(b, g),
        in_specs=[u_spec, u_spec, full((2, unit)), full((2, unit)),
                  full((CMP_BLOCK * dh, CMP_HIDDEN)), full((CMP_HIDDEN, dh)),
                  full((CMP_BLOCK * dh, CMP_HIDDEN)), full((CMP_HIDDEN, dh))],
        out_specs=[o_spec, o_spec],
        out_shape=[o_shape, o_shape],
        compiler_params=_params(("parallel", "parallel"), 40),
        name="compress",
    )(units(kc), units(vc), pos(pos_k), pos(pos_v),
      kw1.astype(BF16), kw2.astype(BF16), vw1.astype(BF16), vw2.astype(BF16))


def _cmp_select_body(q_ref, kc_ref, vc_ref, ct_ref, ocmp_ref, sel_ref, *, tq):
    q0 = pl.program_id(2) * tq
    nu = kc_ref.shape[2]
    n_slc = ct_ref.shape[0]
    q = q_ref[0].reshape(GROUP * tq, HEAD_DIM)
    s = lax.dot_general(q, kc_ref[0, 0], _NT, preferred_element_type=F32).reshape(GROUP, tq, nu)
    t = q0 + lax.broadcasted_iota(jnp.int32, (tq, nu), 0)
    cmp_end = lax.broadcasted_iota(jnp.int32, (tq, nu), 1) * CMP_STRIDE + (CMP_BLOCK - 1)
    s = jnp.where((cmp_end <= t)[None], s, MASKED)
    m = jnp.max(s, axis=-1, keepdims=True)
    m = jnp.where(m > 0.5 * MASKED, m, 0.0)
    e = jnp.exp(s - m)
    p = e * (1.0 / jnp.maximum(jnp.sum(e, axis=-1, keepdims=True), 1e-30))
    o = jnp.dot(p.reshape(GROUP * tq, nu).astype(BF16), vc_ref[0, 0], preferred_element_type=F32)
    ocmp_ref[0] = o.reshape(GROUP, tq, HEAD_DIM)

    psum = p[0] + p[1] + p[2] + p[3]
    hi = psum.astype(BF16)
    r1 = psum - hi.astype(F32)
    mid = r1.astype(BF16)
    lo = (r1 - mid.astype(F32)).astype(BF16)
    ct = ct_ref[...]
    imp = (lax.dot_general(ct, hi, _NT, preferred_element_type=F32)
           + lax.dot_general(ct, mid, _NT, preferred_element_type=F32)
           + lax.dot_general(ct, lo, _NT, preferred_element_type=F32))

    blk = lax.broadcasted_iota(jnp.int32, (n_slc, tq), 0)
    jt = (q0 + lax.broadcasted_iota(jnp.int32, (n_slc, tq), 1)) // SLC_BLOCK
    forced = (blk == 0) | (blk == jt) | (blk == jt - 1)
    val = jnp.where((blk >= 1) & (blk < jt - 1), imp, -1.0)
    sel = jnp.where(forced, 1.0, 0.0)
    for _ in range(N_SELECT - N_FORCED):
        best = jnp.max(val, axis=0, keepdims=True)
        first = jnp.min(jnp.where(val == best, blk, n_slc), axis=0, keepdims=True)
        hit = (blk == first) & (best >= 0.0)
        sel = jnp.where(hit, 1.0, sel)
        val = jnp.where(hit, -1.0, val)
    sel_ref[0, 0] = sel.T.astype(BF16)


def _cmp_to_slc_t(nu, n_slc):
    per_slc = SLC_BLOCK // CMP_STRIDE
    c = np.arange(nu)[None, :]
    j = np.arange(n_slc)[:, None]
    m = np.zeros((n_slc, nu), np.float32)
    for unit in range(CMP_BLOCK // CMP_STRIDE):
        m += ((c + unit) // per_slc == j)
    m[:, nu - 1] = 0.0
    return jnp.asarray(m, BF16)


def _cmp_select(q_raw, k_cmp, v_cmp):
    b, _, s, dh = q_raw.shape
    nu = k_cmp.shape[2]
    n_slc = s // SLC_BLOCK
    tq = 128
    kv_spec = pl.BlockSpec((1, 1, nu, dh), lambda bi, gi, qi: (bi, gi, 0, 0))
    return pl.pallas_call(
        functools.partial(_cmp_select_body, tq=tq),
        grid=(b, N_KV, s // tq),
        in_specs=[pl.BlockSpec((1, GROUP, tq, dh), lambda bi, gi, qi: (bi, gi, qi, 0)),
                  kv_spec, kv_spec,
                  pl.BlockSpec((n_slc, nu), lambda bi, gi, qi: (0, 0))],
        out_specs=[pl.BlockSpec((1, GROUP, tq, dh), lambda bi, gi, qi: (bi, gi, qi, 0)),
                   pl.BlockSpec((1, 1, tq, n_slc), lambda bi, gi, qi: (bi, gi, qi, 0))],
        out_shape=[jax.ShapeDtypeStruct((b, N_HEADS, s, dh), F32),
                   jax.ShapeDtypeStruct((b, N_KV, s, n_slc), BF16)],
        compiler_params=_params(("parallel", "parallel", "parallel"), 32),
        name="cmp_select",
    )(q_raw, k_cmp, v_cmp, _cmp_to_slc_t(nu, n_slc))


def _attend_body(q_ref, ks_ref, vs_ref, kw_ref, vw_ref, sel_ref, exp_ref, ocmp_ref, g_ref, o_ref,
                 m_scr, l_scr, acc_scr, *, tq, tk):
    q0 = pl.program_id(2) * tq
    q = q_ref[0].reshape(GROUP * tq, HEAD_DIM)
    sel = sel_ref[0, 0]
    t = q0 + lax.broadcasted_iota(jnp.int32, (tq, 1), 0)

    m_scr[...] = jnp.full(m_scr.shape, MASKED, F32)
    l_scr[...] = jnp.zeros(l_scr.shape, F32)
    acc_scr[...] = jnp.zeros(acc_scr.shape, F32)

    def key_tile(kt, carry):
        k0 = pl.multiple_of(kt * tk, tk)
        s = lax.dot_general(q, ks_ref[0, 0, pl.ds(k0, tk), :], _NT, preferred_element_type=F32)
        chosen = jnp.dot(sel, exp_ref[kt], preferred_element_type=F32)
        kpos = k0 + lax.broadcasted_iota(jnp.int32, (tq, tk), 1)
        bias = jnp.where((chosen > 0.5) & (kpos <= t), 0.0, MASKED)
        s = s.reshape(GROUP, tq, tk) + bias[None]
        m_old = m_scr[...]
        m_new = jnp.maximum(m_old, jnp.max(s, axis=-1, keepdims=True))
        alpha = jnp.exp(m_old - m_new)
        p = jnp.exp(s - m_new)
        l_scr[...] = alpha * l_scr[...] + jnp.sum(p, axis=-1, keepdims=True)
        pv = jnp.dot(p.reshape(GROUP * tq, tk).astype(BF16), vs_ref[0, 0, pl.ds(k0, tk), :],
                     preferred_element_type=F32)
        acc_scr[...] = alpha * acc_scr[...] + pv.reshape(GROUP, tq, HEAD_DIM)
        m_scr[...] = m_new
        return carry

    lax.fori_loop(0, (q0 + tq + tk - 1) // tk, key_tile, 0)
    o_slc = acc_scr[...] * (1.0 / l_scr[...])

    span = WINDOW + tq
    w0 = pl.multiple_of(jnp.maximum(q0 - WINDOW, 0), tq)
    sw = lax.dot_general(q, kw_ref[0, 0, pl.ds(w0, span), :], _NT, preferred_element_type=F32)
    kpos = w0 + lax.broadcasted_iota(jnp.int32, (tq, span), 1)
    bias = jnp.where((kpos <= t) & (kpos > t - WINDOW), 0.0, MASKED)
    sw = sw.reshape(GROUP, tq, span) + bias[None]
    pw = jnp.exp(sw - jnp.max(sw, axis=-1, keepdims=True))
    lw = jnp.sum(pw, axis=-1, keepdims=True)
    o_win = jnp.dot(pw.reshape(GROUP * tq, span).astype(BF16), vw_ref[0, 0, pl.ds(w0, span), :],
                    preferred_element_type=F32).reshape(GROUP, tq, HEAD_DIM) * (1.0 / lw)

    gate = g_ref[0, 0]
    for r in range(GROUP):
        o_ref[0, :, r * HEAD_DIM:(r + 1) * HEAD_DIM] = (
            gate[:, 3 * r:3 * r + 1] * ocmp_ref[0, r]
            + gate[:, 3 * r + 1:3 * r + 2] * o_slc[r]
            + gate[:, 3 * r + 2:3 * r + 3] * o_win[r])


def _block_expander(n_slc, seq, tk):
    key = np.arange(seq).reshape(seq // tk, 1, tk)
    j = np.arange(n_slc).reshape(1, n_slc, 1)
    return jnp.asarray(key // SLC_BLOCK == j, BF16)


def _attend(q_rot, ks, vs, kw, vw, sel, o_cmp, gates):
    b, _, s, dh = q_rot.shape
    n_slc = sel.shape[3]
    tq = 128
    tk = 512
    q_spec = pl.BlockSpec((1, GROUP, tq, dh), lambda bi, gi, qi: (bi, gi, qi, 0))
    kv_spec = pl.BlockSpec((1, 1, s, dh), lambda bi, gi, qi: (bi, gi, 0, 0))
    row_spec = lambda w: pl.BlockSpec((1, 1, tq, w), lambda bi, gi, qi: (bi, gi, qi, 0))
    return pl.pallas_call(
        functools.partial(_attend_body, tq=tq, tk=tk),
        grid=(b, N_KV, s // tq),
        in_specs=[q_spec, kv_spec, kv_spec, kv_spec, kv_spec, row_spec(n_slc),
                  pl.BlockSpec((s // tk, n_slc, tk), lambda bi, gi, qi: (0, 0, 0)),
                  q_spec, row_spec(V7X_LANES)],
        out_specs=pl.BlockSpec((1, tq, GROUP * dh), lambda bi, gi, qi: (bi, qi, gi)),
        out_shape=jax.ShapeDtypeStruct((b, s, Q_DIM), F32),
        scratch_shapes=[pltpu.VMEM((GROUP, tq, 1), F32), pltpu.VMEM((GROUP, tq, 1), F32),
                        pltpu.VMEM((GROUP, tq, dh), F32)],
        compiler_params=_params(("parallel", "parallel", "arbitrary"), 48),
        name="attend",
    )(q_rot, ks, vs, kw, vw, sel, _block_expander(n_slc, s, tk), o_cmp, gates)


def _out_proj_body(cn_ref, a_ref, x_ref, gn_ref, wc_ref, wa_ref, o_ref):
    an = _rms(a_ref[...], gn_ref[...]).astype(BF16)
    y = (jnp.dot(cn_ref[...], wc_ref[...], preferred_element_type=F32)
         + jnp.dot(an, wa_ref[...], preferred_element_type=F32))
    o_ref[...] = x_ref[...] + y


def _out_proj(conv_n, attn, x, nsa_g, w_out):
    t, d = x.shape
    tm = min(256, t)
    wc = w_out[:CONV_CH].astype(BF16)
    wa = w_out[CONV_CH:].astype(BF16)
    return pl.pallas_call(
        _out_proj_body,
        grid=(t // tm,),
        in_specs=[pl.BlockSpec((tm, CONV_CH), lambda i: (i, 0)),
                  pl.BlockSpec((tm, Q_DIM), lambda i: (i, 0)),
                  pl.BlockSpec((tm, d), lambda i: (i, 0)),
                  pl.BlockSpec((1, Q_DIM), lambda i: (0, 0)),
                  _resident((CONV_CH, d), lambda i: (0, 0)),
                  _resident((Q_DIM, d), lambda i: (0, 0))],
        out_specs=pl.BlockSpec((tm, d), lambda i: (i, 0)),
        out_shape=jax.ShapeDtypeStruct((t, d), F32),
        compiler_params=_params(("parallel",), 40),
        name="out_proj",
    )(conv_n, attn, x, nsa_g.reshape(1, Q_DIM), wc, wa)


def kernel(x, ffn1_norm, ffn1_w_gate, ffn1_w_up, ffn1_w_down, mix_norm, w_in, cmp_pos_k, cmp_pos_v, cmp_k_w1, cmp_k_w2, cmp_v_w1, cmp_v_w2, conv_dw_w, conv_dw_b, conv_ln_g, conv_ln_b, conv_pw_w, out_norm_conv, out_norm_nsa, w_out, ffn2_norm, ffn2_w_gate, ffn2_w_up, ffn2_w_down, final_norm):
    b, s, d = x.shape
    assert s % (SLC_BLOCK * V7X_LANES) == 0, "selection blocks must fill whole 128-lane rows"
    depth = ffn1_norm.shape[0]
    y = x.reshape(b * s, d)
    for l in range(depth):
        y = _ffn(y, ffn1_norm[l], ffn1_w_gate[l], ffn1_w_up[l], ffn1_w_down[l])
        u, q_raw, q_rot, kc, vc, ks, vs, kw, vw, gates = _in_proj(y, mix_norm[l], w_in[l], b, s)
        conv_n = _conv(u.reshape(b, s, CONV_CH), conv_dw_w[l], conv_dw_b[l], conv_ln_g[l], conv_ln_b[l],
                       conv_pw_w[l], out_norm_conv[l])
        k_cmp, v_cmp = _compress(kc, vc, cmp_pos_k[l], cmp_pos_v[l],
                                 cmp_k_w1[l], cmp_k_w2[l], cmp_v_w1[l], cmp_v_w2[l])
        o_cmp, sel = _cmp_select(q_raw, k_cmp, v_cmp)
        attn = _attend(q_rot, ks, vs, kw, vw, sel, o_cmp, gates)
        y = _out_proj(conv_n.reshape(b * s, CONV_CH), attn.reshape(b * s, Q_DIM), y, out_norm_nsa[l], w_out[l])
        y = _ffn(y, ffn2_norm[l], ffn2_w_gate[l], ffn2_w_up[l], ffn2_w_down[l],
                 final_g=final_norm if l == depth - 1 else None)
    return y.reshape(b, s, d)
```

```python
import functools

import numpy as np
import jax
import jax.numpy as jnp
from jax import lax
from jax.experimental import pallas as pl
from jax.experimental.pallas import tpu as pltpu

F32 = jnp.float32
BF16 = jnp.bfloat16

V7X_LANES = 128
V7X_VMEM_BYTES = 64 * 2 ** 20

CONV_CH = 512
N_HEADS = 12
HEAD_DIM = 128
N_KV = 3
GROUP = N_HEADS // N_KV
CONV_K = 31
CMP_BLOCK = 32
CMP_STRIDE = 16
CMP_HIDDEN = 256
SLC_BLOCK = 64
N_SELECT = 16
N_FORCED = 3
WINDOW = 512
ROPE_THETA = 10000.0
EPS = 1e-6
MASKED = -1e30
LOG2_E = 1.4426950408889634
SLC, WIN = 0, 1

KV_DIM = N_KV * HEAD_DIM
Q_DIM = N_HEADS * HEAD_DIM
CONV_HALO = 32

_NT = (((1,), (1,)), ((), ()))


def _rms(x, g):
    return x * lax.rsqrt(jnp.mean(x * x, axis=-1, keepdims=True) + EPS) * g


def _params(semantics, vmem_mib):
    return pltpu.CompilerParams(dimension_semantics=semantics, vmem_limit_bytes=vmem_mib * 2 ** 20)


def _resident(shape, index_map):
    return pl.BlockSpec(shape, index_map, pipeline_mode=pl.Buffered(1))


def _ffn_body(*refs, n_f, final_norm):
    if final_norm:
        x_ref, g_ref, wg_ref, wu_ref, wd_ref, fg_ref, o_ref, h_scr = refs
    else:
        x_ref, g_ref, wg_ref, wu_ref, wd_ref, o_ref, h_scr = refs
    j = pl.program_id(1)

    @pl.when(j == 0)
    def _():
        x = x_ref[...]
        h_scr[...] = _rms(x, g_ref[...]).astype(BF16)
        o_ref[...] = x

    h = h_scr[...]
    a = jnp.dot(h, wg_ref[...], preferred_element_type=F32)
    b = jnp.dot(h, wu_ref[...], preferred_element_type=F32)
    z = (a * jax.nn.sigmoid(a) * b).astype(BF16)
    o_ref[...] += 0.5 * jnp.dot(z, wd_ref[...], preferred_element_type=F32)

    if final_norm:
        @pl.when(j == n_f - 1)
        def _():
            o_ref[...] = _rms(o_ref[...], fg_ref[...])


def _ffn(x, g, w_gate, w_up, w_down, final_g=None):
    t, d = x.shape
    f = w_gate.shape[1]
    tm = min(512, t)
    tf = 512
    fp = -(-f // tf) * tf
    wg = jnp.pad(w_gate.astype(BF16), ((0, 0), (0, fp - f)))
    wu = jnp.pad(w_up.astype(BF16), ((0, 0), (0, fp - f)))
    wd = jnp.pad(w_down.astype(BF16), ((0, fp - f), (0, 0)))
    n_f = fp // tf
    final_norm = final_g is not None
    row = pl.BlockSpec((tm, d), lambda i, j: (i, 0))
    vec = pl.BlockSpec((1, d), lambda i, j: (0, 0))
    in_specs = [row, vec,
                pl.BlockSpec((d, tf), lambda i, j: (0, j)),
                pl.BlockSpec((d, tf), lambda i, j: (0, j)),
                pl.BlockSpec((tf, d), lambda i, j: (j, 0))]
    args = [x, g.reshape(1, d), wg, wu, wd]
    if final_norm:
        in_specs.append(vec)
        args.append(final_g.reshape(1, d))
    return pl.pallas_call(
        functools.partial(_ffn_body, n_f=n_f, final_norm=final_norm),
        grid=(t // tm, n_f),
        in_specs=in_specs,
        out_specs=row,
        out_shape=jax.ShapeDtypeStruct((t, d), F32),
        scratch_shapes=[pltpu.VMEM((tm, d), BF16)],
        compiler_params=_params(("parallel", "arbitrary"), 48),
        name="ffn_final" if final_norm else "ffn",
    )(*args)


def _rope(x, cos2, sin2):
    return x * cos2 + pltpu.roll(x, HEAD_DIM // 2, 1) * sin2


def _in_proj_body(x_ref, g_ref, w_ref, cos_ref, sin_ref, blk_ref,
                  u_ref, qraw_ref, qrot_ref, kc_ref, vc_ref, k_ref, v_ref, gate_ref):
    h = _rms(x_ref[...], g_ref[...]).astype(BF16)
    cos2 = cos_ref[...]
    sin2 = sin_ref[...]
    scale = HEAD_DIM ** -0.5

    def proj(c0, width):
        return jnp.dot(h, w_ref[:, c0:c0 + width], preferred_element_type=F32)

    glu = proj(0, 2 * CONV_CH)
    u_ref[...] = glu[:, :CONV_CH] * jax.nn.sigmoid(glu[:, CONV_CH:])

    def head(cols, i):
        return cols[:, i * HEAD_DIM:(i + 1) * HEAD_DIM]

    c0 = 2 * CONV_CH
    for gk in range(N_KV):
        cols = proj(c0, GROUP * HEAD_DIM) * (scale * LOG2_E)
        for r in range(GROUP):
            qh = head(cols, r)
            qraw_ref[0, gk * GROUP + r] = qh.astype(BF16)
            qrot_ref[0, gk * GROUP + r] = _rope(qh, cos2, sin2).astype(BF16)
        c0 += GROUP * HEAD_DIM
    cols = proj(c0, 2 * KV_DIM)
    for gk in range(N_KV):
        kc_ref[0, gk] = head(cols, gk)
        vc_ref[0, gk] = head(cols, N_KV + gk)
    c0 += 2 * KV_DIM
    lo, hi = slice(0, HEAD_DIM), slice(HEAD_DIM, 2 * HEAD_DIM)
    for branch in (SLC, WIN):
        cols = proj(c0, 2 * KV_DIM)
        for gk in range(N_KV):
            k_ref[0, gk, branch, :, lo] = _rope(head(cols, gk), cos2, sin2).astype(BF16)
            v_ref[0, gk, branch, :, lo] = head(cols, N_KV + gk).astype(BF16)
        c0 += 2 * KV_DIM
    cols = jax.nn.sigmoid(proj(c0, N_KV * V7X_LANES))
    for gk in range(N_KV):
        k_ref[0, gk, SLC, :, hi] = blk_ref[...]
        k_ref[0, gk, WIN, :, hi] = jnp.zeros(blk_ref.shape, BF16)
        v_ref[0, gk, SLC, :, hi] = jnp.ones(blk_ref.shape, BF16)
        v_ref[0, gk, WIN, :, hi] = jnp.ones(blk_ref.shape, BF16)
        gate_ref[0, gk] = head(cols, gk)


def _in_proj(x, g, w_in, batch, seq):
    t, d = x.shape
    tm = min(256, seq)
    n_s = seq // tm
    main = 2 * CONV_CH + Q_DIM + 6 * KV_DIM
    gate_w = w_in[:, main:].reshape(d, N_KV, GROUP * 3)
    gate_w = jnp.pad(gate_w, ((0, 0), (0, 0), (0, V7X_LANES - GROUP * 3))).reshape(d, N_KV * V7X_LANES)
    w = jnp.concatenate([w_in[:, :main], gate_w], axis=1).astype(BF16)
    n_w = w.shape[1]

    inv = jnp.power(ROPE_THETA, -jnp.arange(0, HEAD_DIM, 2, dtype=F32) / HEAD_DIM)
    ang = jnp.arange(seq, dtype=F32)[:, None] * inv[None, :]
    cos2 = jnp.concatenate([jnp.cos(ang), jnp.cos(ang)], axis=1)
    sin2 = jnp.concatenate([-jnp.sin(ang), jnp.sin(ang)], axis=1)

    key_blk = np.arange(seq)[:, None] // SLC_BLOCK == np.arange(seq // SLC_BLOCK)[None, :]
    blk_mask = jnp.asarray(np.where(key_blk, -(2.0 ** 100), 0.0), BF16)

    def heads(n, dtype):
        return (jax.ShapeDtypeStruct((batch, n, seq, HEAD_DIM), dtype),
                pl.BlockSpec((1, n, tm, HEAD_DIM), lambda i: (i // n_s, 0, i % n_s, 0)))

    stacked = (jax.ShapeDtypeStruct((batch, N_KV, 2, seq, 2 * HEAD_DIM), BF16),
               pl.BlockSpec((1, N_KV, 2, tm, 2 * HEAD_DIM), lambda i: (i // n_s, 0, 0, i % n_s, 0)))
    outs = [(jax.ShapeDtypeStruct((t, CONV_CH), F32), pl.BlockSpec((tm, CONV_CH), lambda i: (i, 0))),
            heads(N_HEADS, BF16), heads(N_HEADS, BF16),
            heads(N_KV, F32), heads(N_KV, F32),
            stacked, stacked,
            heads(N_KV, F32)]
    table = pl.BlockSpec((tm, HEAD_DIM), lambda i: (i % n_s, 0))
    return pl.pallas_call(
        _in_proj_body,
        grid=(t // tm,),
        in_specs=[pl.BlockSpec((tm, d), lambda i: (i, 0)),
                  pl.BlockSpec((1, d), lambda i: (0, 0)),
                  _resident((d, n_w), lambda i: (0, 0)),
                  table, table, table],
        out_specs=[o[1] for o in outs],
        out_shape=[o[0] for o in outs],
        compiler_params=_params(("parallel",), 48),
        name="in_proj",
    )(x, g.reshape(1, d), w, cos2, sin2, blk_mask)


CONV_ROWS = 32


def _conv_body(u_ref, halo_ref, dw_ref, db_ref, lg_ref, lb_ref, pw_ref, og_ref, o_ref, ext_scr, y_scr, *, ts):
    i = pl.program_id(1)
    ext_scr[0:CONV_HALO, :] = jnp.where(i == 0, 0.0, halo_ref[0])
    ext_scr[CONV_HALO:CONV_HALO + ts, :] = u_ref[0]
    first = CONV_HALO - (CONV_K - 1)
    for c in range(ts // CONV_ROWS):
        r0 = c * CONV_ROWS
        acc = jnp.broadcast_to(db_ref[...], (CONV_ROWS, CONV_CH))
        for k in range(CONV_K):
            acc = acc + dw_ref[k:k + 1, :] * ext_scr[r0 + first + k:r0 + first + k + CONV_ROWS, :]
        mu = jnp.mean(acc, axis=-1, keepdims=True)
        xc = acc - mu
        var = jnp.mean(xc * xc, axis=-1, keepdims=True)
        y = xc * lax.rsqrt(var + EPS) * lg_ref[...] + lb_ref[...]
        y_scr[r0:r0 + CONV_ROWS, :] = (y * jax.nn.sigmoid(y)).astype(BF16)
    z = jnp.dot(y_scr[...], pw_ref[...], preferred_element_type=F32)
    o_ref[0] = _rms(z, og_ref[...]).astype(BF16)


def _conv(u, dw_w, dw_b, ln_g, ln_b, pw_w, out_g):
    b, s, c = u.shape
    ts = min(256, s)
    per = ts // CONV_HALO
    vec = pl.BlockSpec((1, c), lambda bi, i: (0, 0))
    return pl.pallas_call(
        functools.partial(_conv_body, ts=ts),
        grid=(b, s // ts),
        in_specs=[pl.BlockSpec((1, ts, c), lambda bi, i: (bi, i, 0)),
                  pl.BlockSpec((1, CONV_HALO, c), lambda bi, i: (bi, jnp.maximum(i * per - 1, 0), 0)),
                  pl.BlockSpec((CONV_K, c), lambda bi, i: (0, 0)),
                  vec, vec, vec,
                  pl.BlockSpec((c, c), lambda bi, i: (0, 0)),
                  vec],
        out_specs=pl.BlockSpec((1, ts, c), lambda bi, i: (bi, i, 0)),
        out_shape=jax.ShapeDtypeStruct((b, s, c), BF16),
        scratch_shapes=[pltpu.VMEM((CONV_HALO + ts, c), F32), pltpu.VMEM((ts, c), BF16)],
        compiler_params=_params(("parallel", "parallel"), 32),
        name="conv",
    )(u, u, dw_w, dw_b.reshape(1, c), ln_g.reshape(1, c), ln_b.reshape(1, c), pw_w.astype(BF16),
      out_g.reshape(1, c))


def _compress_one(u_ref, pos_ref, w1_ref, w2_ref, o_ref):
    u = u_ref[0, 0]
    half = u.shape[1]
    nu = u.shape[0]
    top = jnp.dot((u + pos_ref[0:1, :]).astype(BF16), w1_ref[0:half, :], preferred_element_type=F32)
    bot = jnp.dot((u + pos_ref[1:2, :]).astype(BF16), w1_ref[half:2 * half, :], preferred_element_type=F32)
    hid = top + pltpu.roll(bot, nu - 1, 0)
    hid = hid * jax.nn.sigmoid(hid)
    o_ref[0, 0] = jnp.dot(hid.astype(BF16), w2_ref[...], preferred_element_type=F32).astype(BF16)


def _compress_body(uk_ref, uv_ref, pk_ref, pv_ref, kw1_ref, kw2_ref, vw1_ref, vw2_ref, ok_ref, ov_ref):
    _compress_one(uk_ref, pk_ref, kw1_ref, kw2_ref, ok_ref)
    _compress_one(uv_ref, pv_ref, vw1_ref, vw2_ref, ov_ref)


def _compress(kc, vc, pos_k, pos_v, kw1, kw2, vw1, vw2):
    b, g, s, dh = kc.shape
    nu = s // CMP_STRIDE
    unit = CMP_STRIDE * dh
    units = lambda a: a.reshape(b, g, nu, unit)
    pos = lambda p: p.reshape(CMP_BLOCK // CMP_STRIDE, unit)
    u_spec = pl.BlockSpec((1, 1, nu, unit), lambda bi, gi: (bi, gi, 0, 0))
    full = lambda shape: pl.BlockSpec(shape, lambda bi, gi: (0,) * len(shape))
    o_spec = pl.BlockSpec((1, 1, nu, dh), lambda bi, gi: (bi, gi, 0, 0))
    o_shape = jax.ShapeDtypeStruct((b, g, nu, dh), BF16)
    return pl.pallas_call(
        _compress_body,
        grid=(b, g),
        in_specs=[u_spec, u_spec, full((2, unit)), full((2, unit)),
                  full((CMP_BLOCK * dh, CMP_HIDDEN)), full((CMP_HIDDEN, dh)),
                  full((CMP_BLOCK * dh, CMP_HIDDEN)), full((CMP_HIDDEN, dh))],
        out_specs=[o_spec, o_spec],
        out_shape=[o_shape, o_shape],
        compiler_params=_params(("parallel", "parallel"), 40),
        name="compress",
    )(units(kc), units(vc), pos(pos_k), pos(pos_v),
      kw1.astype(BF16), kw2.astype(BF16), vw1.astype(BF16), vw2.astype(BF16))


def _cmp_select_body(q_ref, kc_ref, vc_ref, ct_ref, ocmp_ref, sel_ref, *, tq):
    q0 = pl.program_id(2) * tq
    nu = kc_ref.shape[2]
    n_slc = ct_ref.shape[0]
    q = q_ref[0].reshape(GROUP * tq, HEAD_DIM)
    s = lax.dot_general(q, kc_ref[0, 0], _NT, preferred_element_type=F32).reshape(GROUP, tq, nu)
    t = q0 + lax.broadcasted_iota(jnp.int32, (tq, nu), 0)
    cmp_end = lax.broadcasted_iota(jnp.int32, (tq, nu), 1) * CMP_STRIDE + (CMP_BLOCK - 1)
    s = s + jnp.where(cmp_end <= t, 0.0, MASKED)[None]
    m = jnp.max(s, axis=-1, keepdims=True)
    m = jnp.where(m > 0.5 * MASKED, m, 0.0)
    e = jnp.exp2(s - m)
    p = e * (1.0 / jnp.maximum(jnp.sum(e, axis=-1, keepdims=True), 1e-30))
    o = jnp.dot(p.reshape(GROUP * tq, nu).astype(BF16), vc_ref[0, 0], preferred_element_type=F32)
    ocmp_ref[0] = o.reshape(GROUP, tq, HEAD_DIM)

    psum = p[0] + p[1] + p[2] + p[3]
    hi = psum.astype(BF16)
    r1 = psum - hi.astype(F32)
    mid = r1.astype(BF16)
    lo = (r1 - mid.astype(F32)).astype(BF16)
    ct = ct_ref[...]
    imp = (lax.dot_general(ct, hi, _NT, preferred_element_type=F32)
           + lax.dot_general(ct, mid, _NT, preferred_element_type=F32)
           + lax.dot_general(ct, lo, _NT, preferred_element_type=F32))

    blk = lax.broadcasted_iota(jnp.int32, (n_slc, tq), 0)
    jt = (q0 + lax.broadcasted_iota(jnp.int32, (n_slc, tq), 1)) // SLC_BLOCK
    forced = (blk == 0) | (blk == jt) | (blk == jt - 1)
    val = jnp.where((blk >= 1) & (blk < jt - 1), imp, -1.0)
    sel = jnp.where(forced, 1.0, 0.0)
    for _ in range(N_SELECT - N_FORCED):
        best = jnp.max(val, axis=0, keepdims=True)
        first = jnp.min(jnp.where(val == best, blk, n_slc), axis=0, keepdims=True)
        hit = (blk == first) & (best >= 0.0)
        sel = jnp.where(hit, 1.0, sel)
        val = jnp.where(hit, -1.0, val)
    sel_ref[0, 0] = (1.0 - sel).T.astype(BF16)


def _cmp_to_slc_t(nu, n_slc):
    per_slc = SLC_BLOCK // CMP_STRIDE
    c = np.arange(nu)[None, :]
    j = np.arange(n_slc)[:, None]
    m = np.zeros((n_slc, nu), np.float32)
    for unit in range(CMP_BLOCK // CMP_STRIDE):
        m += ((c + unit) // per_slc == j)
    m[:, nu - 1] = 0.0
    return jnp.asarray(m, BF16)


def _cmp_select(q_raw, k_cmp, v_cmp):
    b, _, s, dh = q_raw.shape
    nu = k_cmp.shape[2]
    n_slc = s // SLC_BLOCK
    tq = 128
    kv_spec = pl.BlockSpec((1, 1, nu, dh), lambda bi, gi, qi: (bi, gi, 0, 0))
    return pl.pallas_call(
        functools.partial(_cmp_select_body, tq=tq),
        grid=(b, N_KV, s // tq),
        in_specs=[pl.BlockSpec((1, GROUP, tq, dh), lambda bi, gi, qi: (bi, gi, qi, 0)),
                  kv_spec, kv_spec,
                  pl.BlockSpec((n_slc, nu), lambda bi, gi, qi: (0, 0))],
        out_specs=[pl.BlockSpec((1, GROUP, tq, dh), lambda bi, gi, qi: (bi, gi, qi, 0)),
                   pl.BlockSpec((1, 1, tq, n_slc), lambda bi, gi, qi: (bi, gi, qi, 0))],
        out_shape=[jax.ShapeDtypeStruct((b, N_HEADS, s, dh), F32),
                   jax.ShapeDtypeStruct((b, N_KV, s, n_slc), BF16)],
        compiler_params=_params(("parallel", "parallel", "parallel"), 32),
        name="cmp_select",
    )(q_raw, k_cmp, v_cmp, _cmp_to_slc_t(nu, n_slc))


def _attend_body(q_ref, k_ref, v_ref, unsel_ref, ocmp_ref, g_ref, o_ref,
                 qa_scr, s_scr, m_scr, acc_scr, *, tq, tk):
    q0 = pl.program_id(2) * tq
    rows = GROUP * tq
    seq = k_ref.shape[3]
    t = q0 + lax.broadcasted_iota(jnp.int32, (tq, 1), 0)

    unsel = unsel_ref[0, 0]
    for r in range(GROUP):
        qa_scr[r * tq:(r + 1) * tq, 0:HEAD_DIM] = q_ref[0, r]
        qa_scr[r * tq:(r + 1) * tq, HEAD_DIM:2 * HEAD_DIM] = unsel
    m_scr[...] = jnp.full(m_scr.shape, MASKED, F32)
    acc_scr[...] = jnp.zeros(acc_scr.shape, F32)

    last = q0 // tk
    n_win = jnp.minimum(last + 1, WINDOW // tk + 1)
    n_tiles = last + 1 + n_win

    def tile(i):
        branch = (i > last).astype(jnp.int32)
        kt = i - branch * n_win
        return branch, pl.multiple_of(kt * tk, tk)

    def scores(i):
        branch, k0 = tile(i)
        s_scr[...] = lax.dot_general(qa_scr[...], k_ref[0, 0, branch, pl.ds(k0, tk), :], _NT,
                                     preferred_element_type=F32)

    def softmax_pv(i, boundary):
        branch, k0 = tile(i)
        if boundary:
            kpos = k0 + lax.broadcasted_iota(jnp.int32, (tq, tk), 1)
            reach = jnp.where(branch == WIN, WINDOW, seq)
            bias = jnp.where((kpos <= t) & (kpos > t - reach), 0.0, MASKED)
            s = (s_scr[...].reshape(GROUP, tq, tk) + bias[None]).reshape(rows, tk)
        else:
            s = s_scr[...]
        m_old = m_scr[branch]
        m_new = jnp.maximum(m_old, jnp.max(s, axis=-1, keepdims=True))
        p = jnp.exp2(s - m_new).astype(BF16)
        pv = jnp.dot(p, v_ref[0, 0, branch, pl.ds(k0, tk), :], preferred_element_type=F32)
        acc_scr[branch] = jnp.exp2(m_old - m_new) * acc_scr[branch] + pv
        m_scr[branch] = m_new

    scores(0)

    def interior_step(i, carry):
        softmax_pv(i, False)
        scores(i + 1)
        return carry

    def boundary_step(i, carry):
        softmax_pv(i, True)
        scores(i + 1)
        return carry

    lax.fori_loop(0, last, interior_step, 0)
    lax.fori_loop(last, n_tiles - 1, boundary_step, 0)
    softmax_pv(n_tiles - 1, True)

    def normalized(branch):
        acc = acc_scr[branch]
        return acc[:, 0:HEAD_DIM] * (1.0 / acc[:, HEAD_DIM:2 * HEAD_DIM])

    o_slc = normalized(SLC)
    o_win = normalized(WIN)
    gate = g_ref[0, 0]
    for r in range(GROUP):
        o_ref[0, :, r * HEAD_DIM:(r + 1) * HEAD_DIM] = (
            gate[:, 3 * r:3 * r + 1] * ocmp_ref[0, r]
            + gate[:, 3 * r + 1:3 * r + 2] * o_slc[r * tq:(r + 1) * tq]
            + gate[:, 3 * r + 2:3 * r + 3] * o_win[r * tq:(r + 1) * tq])


def _attend(q_rot, k_all, v_all, unsel, o_cmp, gates):
    b, _, s, dh = q_rot.shape
    n_slc = unsel.shape[3]
    assert n_slc == dh, "the unselected one-hot fills the second half of the augmented contraction"
    tq = 512
    tk = 512
    assert tk % tq == 0 and WINDOW % tk == 0
    q_spec = pl.BlockSpec((1, GROUP, tq, dh), lambda bi, gi, qi: (bi, gi, qi, 0))
    kv_spec = _resident((1, 1, 2, s, 2 * dh), lambda bi, gi, qi: (bi, gi, 0, 0, 0))
    row_spec = lambda w: pl.BlockSpec((1, 1, tq, w), lambda bi, gi, qi: (bi, gi, qi, 0))
    return pl.pallas_call(
        functools.partial(_attend_body, tq=tq, tk=tk),
        grid=(b, N_KV, s // tq),
        in_specs=[q_spec, kv_spec, kv_spec, row_spec(n_slc), q_spec, row_spec(V7X_LANES)],
        out_specs=pl.BlockSpec((1, tq, GROUP * dh), lambda bi, gi, qi: (bi, qi, gi)),
        out_shape=jax.ShapeDtypeStruct((b, s, Q_DIM), F32),
        scratch_shapes=[pltpu.VMEM((GROUP * tq, 2 * dh), BF16), pltpu.VMEM((GROUP * tq, tk), F32),
                        pltpu.VMEM((2, GROUP * tq, 1), F32),
                        pltpu.VMEM((2, GROUP * tq, 2 * dh), F32)],
        compiler_params=_params(("parallel", "parallel", "arbitrary"), 48),
        name="attend",
    )(q_rot, k_all, v_all, unsel, o_cmp, gates)


def _out_proj_body(cn_ref, a_ref, x_ref, gn_ref, wc_ref, wa_ref, o_ref):
    an = _rms(a_ref[...], gn_ref[...]).astype(BF16)
    y = (jnp.dot(cn_ref[...], wc_ref[...], preferred_element_type=F32)
         + jnp.dot(an, wa_ref[...], preferred_element_type=F32))
    o_ref[...] = x_ref[...] + y


def _out_proj(conv_n, attn, x, nsa_g, w_out):
    t, d = x.shape
    tm = min(256, t)
    wc = w_out[:CONV_CH].astype(BF16)
    wa = w_out[CONV_CH:].astype(BF16)
    return pl.pallas_call(
        _out_proj_body,
        grid=(t // tm,),
        in_specs=[pl.BlockSpec((tm, CONV_CH), lambda i: (i, 0)),
                  pl.BlockSpec((tm, Q_DIM), lambda i: (i, 0)),
                  pl.BlockSpec((tm, d), lambda i: (i, 0)),
                  pl.BlockSpec((1, Q_DIM), lambda i: (0, 0)),
                  _resident((CONV_CH, d), lambda i: (0, 0)),
                  _resident((Q_DIM, d), lambda i: (0, 0))],
        out_specs=pl.BlockSpec((tm, d), lambda i: (i, 0)),
        out_shape=jax.ShapeDtypeStruct((t, d), F32),
        compiler_params=_params(("parallel",), 40),
        name="out_proj",
    )(conv_n, attn, x, nsa_g.reshape(1, Q_DIM), wc, wa)


def kernel(x, ffn1_norm, ffn1_w_gate, ffn1_w_up, ffn1_w_down, mix_norm, w_in, cmp_pos_k, cmp_pos_v, cmp_k_w1, cmp_k_w2, cmp_v_w1, cmp_v_w2, conv_dw_w, conv_dw_b, conv_ln_g, conv_ln_b, conv_pw_w, out_norm_conv, out_norm_nsa, w_out, ffn2_norm, ffn2_w_gate, ffn2_w_up, ffn2_w_down, final_norm):
    b, s, d = x.shape
    assert s % (SLC_BLOCK * V7X_LANES) == 0, "selection blocks must fill whole 128-lane rows"
    depth = ffn1_norm.shape[0]
    y = x.reshape(b * s, d)
    for l in range(depth):
        y = _ffn(y, ffn1_norm[l], ffn1_w_gate[l], ffn1_w_up[l], ffn1_w_down[l])
        u, q_raw, q_rot, kc, vc, k_all, v_all, gates = _in_proj(y, mix_norm[l], w_in[l], b, s)
        conv_n = _conv(u.reshape(b, s, CONV_CH), conv_dw_w[l], conv_dw_b[l], conv_ln_g[l], conv_ln_b[l],
                       conv_pw_w[l], out_norm_conv[l])
        k_cmp, v_cmp = _compress(kc, vc, cmp_pos_k[l], cmp_pos_v[l],
                                 cmp_k_w1[l], cmp_k_w2[l], cmp_v_w1[l], cmp_v_w2[l])
        o_cmp, unsel = _cmp_select(q_raw, k_cmp, v_cmp)
        attn = _attend(q_rot, k_all, v_all, unsel, o_cmp, gates)
        y = _out_proj(conv_n.reshape(b * s, CONV_CH), attn.reshape(b * s, Q_DIM), y, out_norm_nsa[l], w_out[l])
        y = _ffn(y, ffn2_norm[l], ffn2_w_gate[l], ffn2_w_up[l], ffn2_w_down[l],
                 final_g=final_norm if l == depth - 1 else None)
    return y.reshape(b, s, d)
```

```python
import functools

import numpy as np
import jax
import jax.numpy as jnp
from jax import lax
from jax.experimental import pallas as pl
from jax.experimental.pallas import tpu as pltpu

F32 = jnp.float32
BF16 = jnp.bfloat16

V7X_LANES = 128
V7X_SUBLANES = 8
V7X_VMEM_BYTES = 64 * 2 ** 20

CONV_CH = 512
N_HEADS = 12
HEAD_DIM = 128
N_KV = 3
GROUP = N_HEADS // N_KV
CONV_K = 31
CMP_BLOCK = 32
CMP_STRIDE = 16
CMP_HIDDEN = 256
SLC_BLOCK = 64
N_SELECT = 16
N_FORCED = 3
WINDOW = 512
ROPE_THETA = 10000.0
EPS = 1e-6
MASKED = -1e30
LOG2_E = 1.4426950408889634
SLC, WIN = 0, 1

KV_DIM = N_KV * HEAD_DIM
Q_DIM = N_HEADS * HEAD_DIM
CMP_SUB = 128
CONV_HALO = 32

_NT = (((1,), (1,)), ((), ()))


def _rms(x, g):
    return x * lax.rsqrt(jnp.mean(x * x, axis=-1, keepdims=True) + EPS) * g


def _params(semantics, vmem_mib):
    return pltpu.CompilerParams(dimension_semantics=semantics, vmem_limit_bytes=vmem_mib * 2 ** 20)


def _resident(shape, index_map):
    return pl.BlockSpec(shape, index_map, pipeline_mode=pl.Buffered(1))


def _ffn_body(*refs, n_main, has_tail, final_norm):
    refs = list(refs)
    x_ref, g_ref, wg_ref, wu_ref, wd_ref = refs[:5]
    del refs[:5]
    if has_tail:
        tail_refs = refs[:3]
        del refs[:3]
    if final_norm:
        fg_ref = refs.pop(0)
    o_ref, h_scr = refs
    j = pl.program_id(1)

    @pl.when(j == 0)
    def _():
        x = x_ref[...]
        h_scr[...] = _rms(x, g_ref[...]).astype(BF16)
        o_ref[...] = x

    def hidden_slab(wg, wu, wd):
        h = h_scr[...]
        a = jnp.dot(h, wg[...], preferred_element_type=F32)
        b = jnp.dot(h, wu[...], preferred_element_type=F32)
        z = (a * jax.nn.sigmoid(a) * b).astype(BF16)
        o_ref[...] += 0.5 * jnp.dot(z, wd[...], preferred_element_type=F32)

    if has_tail:
        pl.when(j < n_main)(functools.partial(hidden_slab, wg_ref, wu_ref, wd_ref))
        pl.when(j == n_main)(functools.partial(hidden_slab, *tail_refs))
    else:
        hidden_slab(wg_ref, wu_ref, wd_ref)

    if final_norm:
        @pl.when(j == n_main + has_tail - 1)
        def _():
            o_ref[...] = _rms(o_ref[...], fg_ref[...])


def _ffn(x, g, w_gate, w_up, w_down, final_g=None):
    t, d = x.shape
    f = w_gate.shape[1]
    tm = min(512, t)
    tf = 512
    n_main, f_tail = divmod(f, tf)
    assert f_tail % V7X_LANES == 0 and n_main >= 1
    has_tail = int(f_tail > 0)
    f_main = n_main * tf
    wg, wu, wd = w_gate.astype(BF16), w_up.astype(BF16), w_down.astype(BF16)
    final_norm = final_g is not None
    row = pl.BlockSpec((tm, d), lambda i, j: (i, 0))
    vec = pl.BlockSpec((1, d), lambda i, j: (0, 0))
    slab = lambda i, j: (0, jnp.minimum(j, n_main - 1))
    in_specs = [row, vec,
                pl.BlockSpec((d, tf), slab),
                pl.BlockSpec((d, tf), slab),
                pl.BlockSpec((tf, d), lambda i, j: (jnp.minimum(j, n_main - 1), 0))]
    args = [x, g.reshape(1, d), wg, wu, wd]
    if has_tail:
        in_specs += [_resident((d, f_tail), lambda i, j: (0, 0)),
                     _resident((d, f_tail), lambda i, j: (0, 0)),
                     _resident((f_tail, d), lambda i, j: (0, 0))]
        args += [wg[:, f_main:], wu[:, f_main:], wd[f_main:]]
    if final_norm:
        in_specs.append(vec)
        args.append(final_g.reshape(1, d))
    return pl.pallas_call(
        functools.partial(_ffn_body, n_main=n_main, has_tail=has_tail, final_norm=final_norm),
        grid=(t // tm, n_main + has_tail),
        in_specs=in_specs,
        out_specs=row,
        out_shape=jax.ShapeDtypeStruct((t, d), F32),
        scratch_shapes=[pltpu.VMEM((tm, d), BF16)],
        compiler_params=_params(("parallel", "arbitrary"), 48),
        name="ffn_final" if final_norm else "ffn",
    )(*args)


def _rope(x, cos2, sin2):
    return x * cos2 + pltpu.roll(x, HEAD_DIM // 2, 1) * sin2


def _in_proj_body(x_ref, g_ref, w_ref, cos_ref, sin_ref, blk_ref,
                  u_ref, qraw_ref, qrot_ref, kc_ref, vc_ref, k_ref, v_ref, gate_ref):
    h = _rms(x_ref[...], g_ref[...]).astype(BF16)
    cos2 = cos_ref[...]
    sin2 = sin_ref[...]
    scale = HEAD_DIM ** -0.5

    def proj(c0, width):
        return jnp.dot(h, w_ref[:, c0:c0 + width], preferred_element_type=F32)

    glu = proj(0, 2 * CONV_CH)
    u_ref[...] = glu[:, :CONV_CH] * jax.nn.sigmoid(glu[:, CONV_CH:])

    def head(cols, i):
        return cols[:, i * HEAD_DIM:(i + 1) * HEAD_DIM]

    c0 = 2 * CONV_CH
    for gk in range(N_KV):
        cols = proj(c0, GROUP * HEAD_DIM) * (scale * LOG2_E)
        for r in range(GROUP):
            qh = head(cols, r)
            qraw_ref[0, gk * GROUP + r] = qh.astype(BF16)
            qrot_ref[0, gk * GROUP + r] = _rope(qh, cos2, sin2).astype(BF16)
        c0 += GROUP * HEAD_DIM
    cols = proj(c0, 2 * KV_DIM)
    for gk in range(N_KV):
        kc_ref[0, gk] = head(cols, gk)
        vc_ref[0, gk] = head(cols, N_KV + gk)
    c0 += 2 * KV_DIM
    lo, hi = slice(0, HEAD_DIM), slice(HEAD_DIM, 2 * HEAD_DIM)
    for branch in (SLC, WIN):
        cols = proj(c0, 2 * KV_DIM)
        for gk in range(N_KV):
            k_ref[0, gk, branch, :, lo] = _rope(head(cols, gk), cos2, sin2).astype(BF16)
            v_ref[0, gk, branch, :, lo] = head(cols, N_KV + gk).astype(BF16)
        c0 += 2 * KV_DIM
    cols = jax.nn.sigmoid(proj(c0, N_KV * V7X_LANES))
    for gk in range(N_KV):
        k_ref[0, gk, SLC, :, hi] = blk_ref[...]
        k_ref[0, gk, WIN, :, hi] = jnp.zeros(blk_ref.shape, BF16)
        v_ref[0, gk, SLC, :, hi] = jnp.ones(blk_ref.shape, BF16)
        v_ref[0, gk, WIN, :, hi] = jnp.ones(blk_ref.shape, BF16)
        gate_ref[0, gk] = head(cols, gk)


def _in_proj(x, g, w_in, batch, seq):
    t, d = x.shape
    tm = min(256, seq)
    n_s = seq // tm
    main = 2 * CONV_CH + Q_DIM + 6 * KV_DIM
    gate_w = w_in[:, main:].reshape(d, N_KV, GROUP * 3)
    gate_w = jnp.pad(gate_w, ((0, 0), (0, 0), (0, V7X_LANES - GROUP * 3))).reshape(d, N_KV * V7X_LANES)
    w = jnp.concatenate([w_in[:, :main], gate_w], axis=1).astype(BF16)
    n_w = w.shape[1]

    inv = jnp.power(ROPE_THETA, -jnp.arange(0, HEAD_DIM, 2, dtype=F32) / HEAD_DIM)
    ang = jnp.arange(seq, dtype=F32)[:, None] * inv[None, :]
    cos2 = jnp.concatenate([jnp.cos(ang), jnp.cos(ang)], axis=1)
    sin2 = jnp.concatenate([-jnp.sin(ang), jnp.sin(ang)], axis=1)

    key_blk = np.arange(seq)[:, None] // SLC_BLOCK == np.arange(seq // SLC_BLOCK)[None, :]
    blk_mask = jnp.asarray(np.where(key_blk, -(2.0 ** 100), 0.0), BF16)

    def heads(n, dtype):
        return (jax.ShapeDtypeStruct((batch, n, seq, HEAD_DIM), dtype),
                pl.BlockSpec((1, n, tm, HEAD_DIM), lambda i: (i // n_s, 0, i % n_s, 0)))

    stacked = (jax.ShapeDtypeStruct((batch, N_KV, 2, seq, 2 * HEAD_DIM), BF16),
               pl.BlockSpec((1, N_KV, 2, tm, 2 * HEAD_DIM), lambda i: (i // n_s, 0, 0, i % n_s, 0)))
    outs = [(jax.ShapeDtypeStruct((t, CONV_CH), F32), pl.BlockSpec((tm, CONV_CH), lambda i: (i, 0))),
            heads(N_HEADS, BF16), heads(N_HEADS, BF16),
            heads(N_KV, F32), heads(N_KV, F32),
            stacked, stacked,
            heads(N_KV, F32)]
    table = pl.BlockSpec((tm, HEAD_DIM), lambda i: (i % n_s, 0))
    return pl.pallas_call(
        _in_proj_body,
        grid=(t // tm,),
        in_specs=[pl.BlockSpec((tm, d), lambda i: (i, 0)),
                  pl.BlockSpec((1, d), lambda i: (0, 0)),
                  _resident((d, n_w), lambda i: (0, 0)),
                  table, table, table],
        out_specs=[o[1] for o in outs],
        out_shape=[o[0] for o in outs],
        compiler_params=_params(("parallel",), 48),
        name="in_proj",
    )(x, g.reshape(1, d), w, cos2, sin2, blk_mask)


CONV_ROWS = 32


def _conv_body(u_ref, halo_ref, dw_ref, db_ref, lg_ref, lb_ref, pw_ref, og_ref, o_ref, ext_scr, y_scr, *, ts):
    i = pl.program_id(1)
    ext_scr[0, 0:CONV_HALO, :] = jnp.where(i == 0, 0.0, halo_ref[0])
    ext_scr[0, CONV_HALO:CONV_HALO + ts, :] = u_ref[0]
    moved = CONV_HALO + ts - V7X_SUBLANES
    for s in range(1, V7X_SUBLANES):
        ext_scr[s, 0:moved, :] = ext_scr[0, s:s + moved, :]
    first = CONV_HALO - (CONV_K - 1)
    for c in range(ts // CONV_ROWS):
        r0 = c * CONV_ROWS
        acc = jnp.broadcast_to(db_ref[...], (CONV_ROWS, CONV_CH))
        for k in range(CONV_K):
            s, base = (first + k) % V7X_SUBLANES, (first + k) // V7X_SUBLANES * V7X_SUBLANES
            acc = acc + dw_ref[k:k + 1, :] * ext_scr[s, r0 + base:r0 + base + CONV_ROWS, :]
        mu = jnp.mean(acc, axis=-1, keepdims=True)
        xc = acc - mu
        var = jnp.mean(xc * xc, axis=-1, keepdims=True)
        y = xc * lax.rsqrt(var + EPS) * lg_ref[...] + lb_ref[...]
        y_scr[r0:r0 + CONV_ROWS, :] = (y * jax.nn.sigmoid(y)).astype(BF16)
    z = jnp.dot(y_scr[...], pw_ref[...], preferred_element_type=F32)
    o_ref[0] = _rms(z, og_ref[...]).astype(BF16)


def _conv(u, dw_w, dw_b, ln_g, ln_b, pw_w, out_g):
    b, s, c = u.shape
    ts = min(256, s)
    per = ts // CONV_HALO
    vec = pl.BlockSpec((1, c), lambda bi, i: (0, 0))
    return pl.pallas_call(
        functools.partial(_conv_body, ts=ts),
        grid=(b, s // ts),
        in_specs=[pl.BlockSpec((1, ts, c), lambda bi, i: (bi, i, 0)),
                  pl.BlockSpec((1, CONV_HALO, c), lambda bi, i: (bi, jnp.maximum(i * per - 1, 0), 0)),
                  pl.BlockSpec((CONV_K, c), lambda bi, i: (0, 0)),
                  vec, vec, vec,
                  pl.BlockSpec((c, c), lambda bi, i: (0, 0)),
                  vec],
        out_specs=pl.BlockSpec((1, ts, c), lambda bi, i: (bi, i, 0)),
        out_shape=jax.ShapeDtypeStruct((b, s, c), BF16),
        scratch_shapes=[pltpu.VMEM((V7X_SUBLANES, CONV_HALO + ts, c), F32), pltpu.VMEM((ts, c), BF16)],
        compiler_params=_params(("parallel", "parallel"), 32),
        name="conv",
    )(u, u, dw_w, dw_b.reshape(1, c), ln_g.reshape(1, c), ln_b.reshape(1, c), pw_w.astype(BF16),
      out_g.reshape(1, c))


def _compress_one(u_ref, pos_ref, w1_ref, w2_ref, o_ref):
    u = u_ref[0, 0]
    half = u.shape[1]
    nu = u.shape[0]
    top = jnp.dot((u + pos_ref[0:1, :]).astype(BF16), w1_ref[0:half, :], preferred_element_type=F32)
    bot = jnp.dot((u + pos_ref[1:2, :]).astype(BF16), w1_ref[half:2 * half, :], preferred_element_type=F32)
    hid = top + pltpu.roll(bot, nu - 1, 0)
    hid = hid * jax.nn.sigmoid(hid)
    o_ref[0, 0] = jnp.dot(hid.astype(BF16), w2_ref[...], preferred_element_type=F32).astype(BF16)


def _compress_body(uk_ref, uv_ref, pk_ref, pv_ref, kw1_ref, kw2_ref, vw1_ref, vw2_ref, ok_ref, ov_ref):
    _compress_one(uk_ref, pk_ref, kw1_ref, kw2_ref, ok_ref)
    _compress_one(uv_ref, pv_ref, vw1_ref, vw2_ref, ov_ref)


def _compress(kc, vc, pos_k, pos_v, kw1, kw2, vw1, vw2):
    b, g, s, dh = kc.shape
    nu = s // CMP_STRIDE
    unit = CMP_STRIDE * dh
    units = lambda a: a.reshape(b, g, nu, unit)
    pos = lambda p: p.reshape(CMP_BLOCK // CMP_STRIDE, unit)
    u_spec = pl.BlockSpec((1, 1, nu, unit), lambda bi, gi: (bi, gi, 0, 0))
    full = lambda shape: pl.BlockSpec(shape, lambda bi, gi: (0,) * len(shape))
    o_spec = pl.BlockSpec((1, 1, nu, dh), lambda bi, gi: (bi, gi, 0, 0))
    o_shape = jax.ShapeDtypeStruct((b, g, nu, dh), BF16)
    return pl.pallas_call(
        _compress_body,
        grid=(b, g),
        in_specs=[u_spec, u_spec, full((2, unit)), full((2, unit)),
                  full((CMP_BLOCK * dh, CMP_HIDDEN)), full((CMP_HIDDEN, dh)),
                  full((CMP_BLOCK * dh, CMP_HIDDEN)), full((CMP_HIDDEN, dh))],
        out_specs=[o_spec, o_spec],
        out_shape=[o_shape, o_shape],
        compiler_params=_params(("parallel", "parallel"), 40),
        name="compress",
    )(units(kc), units(vc), pos(pos_k), pos(pos_v),
      kw1.astype(BF16), kw2.astype(BF16), vw1.astype(BF16), vw2.astype(BF16))


def _cmp_select_prefix(q_ref, kc_ref, vc_ref, ct_ref, ocmp_ref, sel_ref, *, tq, nc, nb):
    for sub in range(tq // CMP_SUB):
        _cmp_select_sub(q_ref, kc_ref, vc_ref, ct_ref, ocmp_ref, sel_ref,
                        row0=sub * CMP_SUB, q0=pl.program_id(2) * tq + sub * CMP_SUB, tq=CMP_SUB, nc=nc, nb=nb)


def _cmp_select_sub(q_ref, kc_ref, vc_ref, ct_ref, ocmp_ref, sel_ref, *, row0, q0, tq, nc, nb):
    n_slc = ct_ref.shape[0]
    q = q_ref[0, :, row0:row0 + tq, :].reshape(GROUP * tq, HEAD_DIM)
    s = lax.dot_general(q, kc_ref[0, 0, 0:nc, :], _NT, preferred_element_type=F32).reshape(GROUP, tq, nc)
    t = q0 + lax.broadcasted_iota(jnp.int32, (tq, nc), 0)
    cmp_end = lax.broadcasted_iota(jnp.int32, (tq, nc), 1) * CMP_STRIDE + (CMP_BLOCK - 1)
    s = s + jnp.where(cmp_end <= t, 0.0, MASKED)[None]
    m = jnp.max(s, axis=-1, keepdims=True)
    m = jnp.where(m > 0.5 * MASKED, m, 0.0)
    e = jnp.exp2(s - m)
    p = e * (1.0 / jnp.maximum(jnp.sum(e, axis=-1, keepdims=True), 1e-30))
    o = jnp.dot(p.reshape(GROUP * tq, nc).astype(BF16), vc_ref[0, 0, 0:nc, :], preferred_element_type=F32)
    ocmp_ref[0, :, row0:row0 + tq, :] = o.reshape(GROUP, tq, HEAD_DIM)

    psum = p[0] + p[1] + p[2] + p[3]
    hi = psum.astype(BF16)
    r1 = psum - hi.astype(F32)
    mid = r1.astype(BF16)
    lo = (r1 - mid.astype(F32)).astype(BF16)
    ct = ct_ref[0:nb, 0:nc]
    imp = (lax.dot_general(ct, hi, _NT, preferred_element_type=F32)
           + lax.dot_general(ct, mid, _NT, preferred_element_type=F32)
           + lax.dot_general(ct, lo, _NT, preferred_element_type=F32))

    blk = lax.broadcasted_iota(jnp.int32, (nb, tq), 0)
    jt = (q0 + lax.broadcasted_iota(jnp.int32, (nb, tq), 1)) // SLC_BLOCK
    forced = (blk == 0) | (blk == jt) | (blk == jt - 1)
    val = jnp.where((blk >= 1) & (blk < jt - 1), imp, -1.0)
    sel = jnp.where(forced, 1.0, 0.0)
    for _ in range(N_SELECT - N_FORCED):
        best = jnp.max(val, axis=0, keepdims=True)
        first = jnp.min(jnp.where(val == best, blk, n_slc), axis=0, keepdims=True)
        hit = (blk == first) & (best >= 0.0)
        sel = jnp.where(hit, 1.0, sel)
        val = jnp.where(hit, -1.0, val)
    unsel = jnp.concatenate([1.0 - sel, jnp.ones((n_slc - nb, tq), F32)], axis=0) if nb < n_slc else 1.0 - sel
    sel_ref[0, 0, row0:row0 + tq, :] = unsel.T.astype(BF16)


def _cmp_select_body(q_ref, kc_ref, vc_ref, ct_ref, ocmp_ref, sel_ref, *, tq):
    nu = kc_ref.shape[2]
    n_slc = ct_ref.shape[0]
    per_slc = nu // n_slc
    visible = (pl.program_id(2) * tq + tq - CMP_BLOCK) // CMP_STRIDE + 1
    n_prefix = nu // V7X_LANES
    need = jnp.clip((visible + V7X_LANES - 1) // V7X_LANES, 1, n_prefix)
    for v in range(1, n_prefix + 1):
        nc = v * V7X_LANES
        pl.when(need == v)(functools.partial(
            _cmp_select_prefix, q_ref, kc_ref, vc_ref, ct_ref, ocmp_ref, sel_ref, tq=tq, nc=nc, nb=nc // per_slc))


def _cmp_to_slc_t(nu, n_slc):
    per_slc = SLC_BLOCK // CMP_STRIDE
    c = np.arange(nu)[None, :]
    j = np.arange(n_slc)[:, None]
    m = np.zeros((n_slc, nu), np.float32)
    for unit in range(CMP_BLOCK // CMP_STRIDE):
        m += ((c + unit) // per_slc == j)
    m[:, nu - 1] = 0.0
    return jnp.asarray(m, BF16)


def _cmp_select(q_raw, k_cmp, v_cmp):
    b, _, s, dh = q_raw.shape
    nu = k_cmp.shape[2]
    n_slc = s // SLC_BLOCK
    tq = 2 * CMP_SUB
    for q_end in range(tq, s + 1, tq):
        prefix = -(-((q_end - CMP_BLOCK) // CMP_STRIDE + 1) // V7X_LANES) * V7X_LANES
        assert prefix * n_slc // nu >= (q_end - 1) // SLC_BLOCK + 1
    kv_spec = pl.BlockSpec((1, 1, nu, dh), lambda bi, gi, qi: (bi, gi, 0, 0))
    return pl.pallas_call(
        functools.partial(_cmp_select_body, tq=tq),
        grid=(b, N_KV, s // tq),
        in_specs=[pl.BlockSpec((1, GROUP, tq, dh), lambda bi, gi, qi: (bi, gi, qi, 0)),
                  kv_spec, kv_spec,
                  pl.BlockSpec((n_slc, nu), lambda bi, gi, qi: (0, 0))],
        out_specs=[pl.BlockSpec((1, GROUP, tq, dh), lambda bi, gi, qi: (bi, gi, qi, 0)),
                   pl.BlockSpec((1, 1, tq, n_slc), lambda bi, gi, qi: (bi, gi, qi, 0))],
        out_shape=[jax.ShapeDtypeStruct((b, N_HEADS, s, dh), F32),
                   jax.ShapeDtypeStruct((b, N_KV, s, n_slc), BF16)],
        compiler_params=_params(("parallel", "parallel", "parallel"), 32),
        name="cmp_select",
    )(q_raw, k_cmp, v_cmp, _cmp_to_slc_t(nu, n_slc))


def _attend_body(q_ref, k_ref, v_ref, unsel_ref, ocmp_ref, g_ref, o_ref,
                 qa_scr, s_scr, m_scr, acc_scr, *, tq, tk):
    q0 = pl.program_id(2) * tq
    rows = GROUP * tq
    seq = k_ref.shape[3]
    t = q0 + lax.broadcasted_iota(jnp.int32, (tq, 1), 0)

    unsel = unsel_ref[0, 0]
    for r in range(GROUP):
        qa_scr[r * tq:(r + 1) * tq, 0:HEAD_DIM] = q_ref[0, r]
        qa_scr[r * tq:(r + 1) * tq, HEAD_DIM:2 * HEAD_DIM] = unsel
    m_scr[...] = jnp.full(m_scr.shape, MASKED, F32)
    acc_scr[...] = jnp.zeros(acc_scr.shape, F32)

    last = q0 // tk
    n_win = jnp.minimum(last + 1, WINDOW // tk + 1)
    n_tiles = last + 1 + n_win

    def tile(i):
        branch = (i > last).astype(jnp.int32)
        kt = i - branch * n_win
        return branch, pl.multiple_of(kt * tk, tk)

    def scores(i):
        branch, k0 = tile(i)
        s_scr[...] = lax.dot_general(qa_scr[...], k_ref[0, 0, branch, pl.ds(k0, tk), :], _NT,
                                     preferred_element_type=F32)

    def softmax_pv(i, boundary):
        branch, k0 = tile(i)
        if boundary:
            kpos = k0 + lax.broadcasted_iota(jnp.int32, (tq, tk), 1)
            reach = jnp.where(branch == WIN, WINDOW, seq)
            bias = jnp.where((kpos <= t) & (kpos > t - reach), 0.0, MASKED)
            s = (s_scr[...].reshape(GROUP, tq, tk) + bias[None]).reshape(rows, tk)
        else:
            s = s_scr[...]
        m_old = m_scr[branch]
        m_new = jnp.maximum(m_old, jnp.max(s, axis=-1, keepdims=True))
        p = jnp.exp2(s - m_new).astype(BF16)
        pv = jnp.dot(p, v_ref[0, 0, branch, pl.ds(k0, tk), :], preferred_element_type=F32)
        acc_scr[branch] = jnp.exp2(m_old - m_new) * acc_scr[branch] + pv
        m_scr[branch] = m_new

    scores(0)

    def interior_step(i, carry):
        softmax_pv(i, False)
        scores(i + 1)
        return carry

    def boundary_step(i, carry):
        softmax_pv(i, True)
        scores(i + 1)
        return carry

    lax.fori_loop(0, last, interior_step, 0)
    lax.fori_loop(last, n_tiles - 1, boundary_step, 0)
    softmax_pv(n_tiles - 1, True)

    def normalized(branch):
        acc = acc_scr[branch]
        return acc[:, 0:HEAD_DIM] * (1.0 / acc[:, HEAD_DIM:2 * HEAD_DIM])

    o_slc = normalized(SLC)
    o_win = normalized(WIN)
    gate = g_ref[0, 0]
    for r in range(GROUP):
        o_ref[0, :, r * HEAD_DIM:(r + 1) * HEAD_DIM] = (
            gate[:, 3 * r:3 * r + 1] * ocmp_ref[0, r]
            + gate[:, 3 * r + 1:3 * r + 2] * o_slc[r * tq:(r + 1) * tq]
            + gate[:, 3 * r + 2:3 * r + 3] * o_win[r * tq:(r + 1) * tq])


def _attend(q_rot, k_all, v_all, unsel, o_cmp, gates):
    b, _, s, dh = q_rot.shape
    n_slc = unsel.shape[3]
    assert n_slc == dh, "the unselected one-hot fills the second half of the augmented contraction"
    tq = 512
    tk = 512
    assert tk % tq == 0 and WINDOW % tk == 0
    q_spec = pl.BlockSpec((1, GROUP, tq, dh), lambda bi, gi, qi: (bi, gi, qi, 0))
    kv_spec = _resident((1, 1, 2, s, 2 * dh), lambda bi, gi, qi: (bi, gi, 0, 0, 0))
    row_spec = lambda w: pl.BlockSpec((1, 1, tq, w), lambda bi, gi, qi: (bi, gi, qi, 0))
    return pl.pallas_call(
        functools.partial(_attend_body, tq=tq, tk=tk),
        grid=(b, N_KV, s // tq),
        in_specs=[q_spec, kv_spec, kv_spec, row_spec(n_slc), q_spec, row_spec(V7X_LANES)],
        out_specs=pl.BlockSpec((1, tq, GROUP * dh), lambda bi, gi, qi: (bi, qi, gi)),
        out_shape=jax.ShapeDtypeStruct((b, s, Q_DIM), F32),
        scratch_shapes=[pltpu.VMEM((GROUP * tq, 2 * dh), BF16), pltpu.VMEM((GROUP * tq, tk), F32),
                        pltpu.VMEM((2, GROUP * tq, 1), F32),
                        pltpu.VMEM((2, GROUP * tq, 2 * dh), F32)],
        compiler_params=_params(("parallel", "parallel", "arbitrary"), 48),
        name="attend",
    )(q_rot, k_all, v_all, unsel, o_cmp, gates)


def _out_proj_body(cn_ref, a_ref, x_ref, gn_ref, wc_ref, wa_ref, o_ref):
    an = _rms(a_ref[...], gn_ref[...]).astype(BF16)
    y = (jnp.dot(cn_ref[...], wc_ref[...], preferred_element_type=F32)
         + jnp.dot(an, wa_ref[...], preferred_element_type=F32))
    o_ref[...] = x_ref[...] + y


def _out_proj(conv_n, attn, x, nsa_g, w_out):
    t, d = x.shape
    tm = min(256, t)
    wc = w_out[:CONV_CH].astype(BF16)
    wa = w_out[CONV_CH:].astype(BF16)
    return pl.pallas_call(
        _out_proj_body,
        grid=(t // tm,),
        in_specs=[pl.BlockSpec((tm, CONV_CH), lambda i: (i, 0)),
                  pl.BlockSpec((tm, Q_DIM), lambda i: (i, 0)),
                  pl.BlockSpec((tm, d), lambda i: (i, 0)),
                  pl.BlockSpec((1, Q_DIM), lambda i: (0, 0)),
                  _resident((CONV_CH, d), lambda i: (0, 0)),
                  _resident((Q_DIM, d), lambda i: (0, 0))],
        out_specs=pl.BlockSpec((tm, d), lambda i: (i, 0)),
        out_shape=jax.ShapeDtypeStruct((t, d), F32),
        compiler_params=_params(("parallel",), 40),
        name="out_proj",
    )(conv_n, attn, x, nsa_g.reshape(1, Q_DIM), wc, wa)


def kernel(x, ffn1_norm, ffn1_w_gate, ffn1_w_up, ffn1_w_down, mix_norm, w_in, cmp_pos_k, cmp_pos_v, cmp_k_w1, cmp_k_w2, cmp_v_w1, cmp_v_w2, conv_dw_w, conv_dw_b, conv_ln_g, conv_ln_b, conv_pw_w, out_norm_conv, out_norm_nsa, w_out, ffn2_norm, ffn2_w_gate, ffn2_w_up, ffn2_w_down, final_norm):
    b, s, d = x.shape
    assert s % (SLC_BLOCK * V7X_LANES) == 0, "selection blocks must fill whole 128-lane rows"
    depth = ffn1_norm.shape[0]
    y = x.reshape(b * s, d)
    for l in range(depth):
        y = _ffn(y, ffn1_norm[l], ffn1_w_gate[l], ffn1_w_up[l], ffn1_w_down[l])
        u, q_raw, q_rot, kc, vc, k_all, v_all, gates = _in_proj(y, mix_norm[l], w_in[l], b, s)
        conv_n = _conv(u.reshape(b, s, CONV_CH), conv_dw_w[l], conv_dw_b[l], conv_ln_g[l], conv_ln_b[l],
                       conv_pw_w[l], out_norm_conv[l])
        k_cmp, v_cmp = _compress(kc, vc, cmp_pos_k[l], cmp_pos_v[l],
                                 cmp_k_w1[l], cmp_k_w2[l], cmp_v_w1[l], cmp_v_w2[l])
        o_cmp, unsel = _cmp_select(q_raw, k_cmp, v_cmp)
        attn = _attend(q_rot, k_all, v_all, unsel, o_cmp, gates)
        y = _out_proj(conv_n.reshape(b * s, CONV_CH), attn.reshape(b * s, Q_DIM), y, out_norm_nsa[l], w_out[l])
        y = _ffn(y, ffn2_norm[l], ffn2_w_gate[l], ffn2_w_up[l], ffn2_w_down[l],
                 final_g=final_norm if l == depth - 1 else None)
    return y.reshape(b, s, d)
```

```python
import functools

import numpy as np
import jax
import jax.numpy as jnp
from jax import lax
from jax.experimental import pallas as pl
from jax.experimental.pallas import tpu as pltpu

F32 = jnp.float32
BF16 = jnp.bfloat16

V7X_LANES = 128
V7X_SUBLANES = 8
V7X_VMEM_BYTES = 64 * 2 ** 20

CONV_CH = 512
N_HEADS = 12
HEAD_DIM = 128
N_KV = 3
GROUP = N_HEADS // N_KV
CONV_K = 31
CMP_BLOCK = 32
CMP_STRIDE = 16
CMP_HIDDEN = 256
SLC_BLOCK = 64
N_SELECT = 16
N_FORCED = 3
WINDOW = 512
ROPE_THETA = 10000.0
EPS = 1e-6
MASKED = -1e30
LOG2_E = 1.4426950408889634
SLC, WIN = 0, 1

KV_DIM = N_KV * HEAD_DIM
Q_DIM = N_HEADS * HEAD_DIM
CMP_SUB = 128
CONV_HALO = 32

_NT = (((1,), (1,)), ((), ()))


def _rms(x, g):
    return x * lax.rsqrt(jnp.mean(x * x, axis=-1, keepdims=True) + EPS) * g


def _params(semantics, vmem_mib):
    return pltpu.CompilerParams(dimension_semantics=semantics, vmem_limit_bytes=vmem_mib * 2 ** 20)


def _resident(shape, index_map):
    return pl.BlockSpec(shape, index_map, pipeline_mode=pl.Buffered(1))


def _ffn_body(*refs, n_main, has_tail, final_norm):
    refs = list(refs)
    x_ref, g_ref, wg_ref, wu_ref, wd_ref = refs[:5]
    del refs[:5]
    if has_tail:
        tail_refs = refs[:3]
        del refs[:3]
    if final_norm:
        fg_ref = refs.pop(0)
    o_ref, h_scr = refs
    j = pl.program_id(1)

    @pl.when(j == 0)
    def _():
        x = x_ref[...]
        h_scr[...] = _rms(x, g_ref[...]).astype(BF16)
        o_ref[...] = x

    def hidden_slab(wg, wu, wd):
        h = h_scr[...]
        a = jnp.dot(h, wg[...], preferred_element_type=F32)
        b = jnp.dot(h, wu[...], preferred_element_type=F32)
        z = (a * jax.nn.sigmoid(a) * b).astype(BF16)
        o_ref[...] += 0.5 * jnp.dot(z, wd[...], preferred_element_type=F32)

    if has_tail:
        pl.when(j < n_main)(functools.partial(hidden_slab, wg_ref, wu_ref, wd_ref))
        pl.when(j == n_main)(functools.partial(hidden_slab, *tail_refs))
    else:
        hidden_slab(wg_ref, wu_ref, wd_ref)

    if final_norm:
        @pl.when(j == n_main + has_tail - 1)
        def _():
            o_ref[...] = _rms(o_ref[...], fg_ref[...])


def _ffn(x, g, w_gate, w_up, w_down, final_g=None):
    t, d = x.shape
    f = w_gate.shape[1]
    tm = min(512, t)
    tf = 512
    n_main, f_tail = divmod(f, tf)
    assert f_tail % V7X_LANES == 0 and n_main >= 1
    has_tail = int(f_tail > 0)
    f_main = n_main * tf
    wg, wu, wd = w_gate.astype(BF16), w_up.astype(BF16), w_down.astype(BF16)
    final_norm = final_g is not None
    row = pl.BlockSpec((tm, d), lambda i, j: (i, 0))
    vec = pl.BlockSpec((1, d), lambda i, j: (0, 0))
    slab = lambda i, j: (0, jnp.minimum(j, n_main - 1))
    in_specs = [row, vec,
                pl.BlockSpec((d, tf), slab),
                pl.BlockSpec((d, tf), slab),
                pl.BlockSpec((tf, d), lambda i, j: (jnp.minimum(j, n_main - 1), 0))]
    args = [x, g.reshape(1, d), wg, wu, wd]
    if has_tail:
        in_specs += [_resident((d, f_tail), lambda i, j: (0, 0)),
                     _resident((d, f_tail), lambda i, j: (0, 0)),
                     _resident((f_tail, d), lambda i, j: (0, 0))]
        args += [wg[:, f_main:], wu[:, f_main:], wd[f_main:]]
    if final_norm:
        in_specs.append(vec)
        args.append(final_g.reshape(1, d))
    return pl.pallas_call(
        functools.partial(_ffn_body, n_main=n_main, has_tail=has_tail, final_norm=final_norm),
        grid=(t // tm, n_main + has_tail),
        in_specs=in_specs,
        out_specs=row,
        out_shape=jax.ShapeDtypeStruct((t, d), F32),
        scratch_shapes=[pltpu.VMEM((tm, d), BF16)],
        compiler_params=_params(("parallel", "arbitrary"), 48),
        name="ffn_final" if final_norm else "ffn",
    )(*args)


def _rope(x, cos2, sin2):
    return x * cos2 + pltpu.roll(x, HEAD_DIM // 2, 1) * sin2


def _in_proj_body(x_ref, g_ref, w_ref, wgate_ref, cos_ref, sin_ref, blk_ref,
                  u_ref, qraw_ref, qrot_ref, kc_ref, vc_ref, k_ref, v_ref, gate_ref, kv_scr):
    h = _rms(x_ref[...], g_ref[...]).astype(BF16)
    cos2 = cos_ref[...]
    sin2 = sin_ref[...]
    scale = HEAD_DIM ** -0.5

    def proj(c0, width):
        return jnp.dot(h, w_ref[:, c0:c0 + width], preferred_element_type=F32)

    glu = proj(0, 2 * CONV_CH)
    u_ref[...] = glu[:, :CONV_CH] * jax.nn.sigmoid(glu[:, CONV_CH:])

    def head(cols, i):
        return cols[:, i * HEAD_DIM:(i + 1) * HEAD_DIM]

    c0 = 2 * CONV_CH
    for gk in range(N_KV):
        cols = proj(c0, GROUP * HEAD_DIM) * (scale * LOG2_E)
        for r in range(GROUP):
            qh = head(cols, r)
            qraw_ref[0, gk * GROUP + r] = qh.astype(BF16)
            qrot_ref[0, gk * GROUP + r] = _rope(qh, cos2, sin2).astype(BF16)
        c0 += GROUP * HEAD_DIM
    cols = proj(c0, 2 * KV_DIM)
    n_unit = kv_scr.shape[1] // CMP_STRIDE
    for ref, first in ((kc_ref, 0), (vc_ref, N_KV)):
        for gk in range(N_KV):
            kv_scr[first + gk] = head(cols, first + gk)
            for slot in range(CMP_STRIDE):
                ref[0, gk, :, slot * HEAD_DIM:(slot + 1) * HEAD_DIM] = kv_scr[
                    first + gk, pl.ds(slot, n_unit, stride=CMP_STRIDE), :]
    c0 += 2 * KV_DIM
    lo, hi = slice(0, HEAD_DIM), slice(HEAD_DIM, 2 * HEAD_DIM)
    for branch in (SLC, WIN):
        cols = proj(c0, 2 * KV_DIM)
        for gk in range(N_KV):
            k_ref[0, gk, branch, :, lo] = _rope(head(cols, gk), cos2, sin2).astype(BF16)
            v_ref[0, gk, branch, :, lo] = head(cols, N_KV + gk).astype(BF16)
        c0 += 2 * KV_DIM
    cols = jax.nn.sigmoid(jnp.dot(h, wgate_ref[...], preferred_element_type=F32))
    for gk in range(N_KV):
        k_ref[0, gk, SLC, :, hi] = blk_ref[...]
        k_ref[0, gk, WIN, :, hi] = jnp.zeros(blk_ref.shape, BF16)
        v_ref[0, gk, SLC, :, hi] = jnp.ones(blk_ref.shape, BF16)
        v_ref[0, gk, WIN, :, hi] = jnp.ones(blk_ref.shape, BF16)
        gate_ref[0, gk] = head(cols, gk)


def _in_proj(x, g, w_in, batch, seq):
    t, d = x.shape
    tm = min(256, seq)
    n_s = seq // tm
    main = 2 * CONV_CH + Q_DIM + 6 * KV_DIM
    gate_w = w_in[:, main:].reshape(d, N_KV, GROUP * 3)
    gate_w = jnp.pad(gate_w, ((0, 0), (0, 0), (0, V7X_LANES - GROUP * 3))).reshape(d, N_KV * V7X_LANES)
    gate_w = gate_w.astype(BF16)
    w = w_in[:, :main].astype(BF16)

    inv = jnp.power(ROPE_THETA, -jnp.arange(0, HEAD_DIM, 2, dtype=F32) / HEAD_DIM)
    ang = jnp.arange(seq, dtype=F32)[:, None] * inv[None, :]
    cos2 = jnp.concatenate([jnp.cos(ang), jnp.cos(ang)], axis=1)
    sin2 = jnp.concatenate([-jnp.sin(ang), jnp.sin(ang)], axis=1)

    key_blk = np.arange(seq)[:, None] // SLC_BLOCK == np.arange(seq // SLC_BLOCK)[None, :]
    blk_mask = jnp.asarray(np.where(key_blk, -(2.0 ** 100), 0.0), BF16)

    def heads(n, dtype):
        return (jax.ShapeDtypeStruct((batch, n, seq, HEAD_DIM), dtype),
                pl.BlockSpec((1, n, tm, HEAD_DIM), lambda i: (i // n_s, 0, i % n_s, 0)))

    stacked = (jax.ShapeDtypeStruct((batch, N_KV, 2, seq, 2 * HEAD_DIM), BF16),
               pl.BlockSpec((1, N_KV, 2, tm, 2 * HEAD_DIM), lambda i: (i // n_s, 0, 0, i % n_s, 0)))
    unit = CMP_STRIDE * HEAD_DIM
    units = (jax.ShapeDtypeStruct((batch, N_KV, seq // CMP_STRIDE, unit), F32),
             pl.BlockSpec((1, N_KV, tm // CMP_STRIDE, unit), lambda i: (i // n_s, 0, i % n_s, 0)))
    outs = [(jax.ShapeDtypeStruct((t, CONV_CH), F32), pl.BlockSpec((tm, CONV_CH), lambda i: (i, 0))),
            heads(N_HEADS, BF16), heads(N_HEADS, BF16),
            units, units,
            stacked, stacked,
            heads(N_KV, F32)]
    table = pl.BlockSpec((tm, HEAD_DIM), lambda i: (i % n_s, 0))
    return pl.pallas_call(
        _in_proj_body,
        grid=(t // tm,),
        in_specs=[pl.BlockSpec((tm, d), lambda i: (i, 0)),
                  pl.BlockSpec((1, d), lambda i: (0, 0)),
                  _resident((d, main), lambda i: (0, 0)),
                  _resident(gate_w.shape, lambda i: (0, 0)),
                  table, table, table],
        out_specs=[o[1] for o in outs],
        out_shape=[o[0] for o in outs],
        scratch_shapes=[pltpu.VMEM((2 * N_KV, tm, HEAD_DIM), F32)],
        compiler_params=_params(("parallel",), 48),
        name="in_proj",
    )(x, g.reshape(1, d), w, gate_w, cos2, sin2, blk_mask)


CONV_ROWS = 32


def _conv_body(u_ref, halo_ref, dw_ref, db_ref, lg_ref, lb_ref, pw_ref, og_ref, o_ref, ext_scr, y_scr, *, ts):
    i = pl.program_id(1)
    ext_scr[0, 0:CONV_HALO, :] = jnp.where(i == 0, 0.0, halo_ref[0])
    ext_scr[0, CONV_HALO:CONV_HALO + ts, :] = u_ref[0]
    moved = CONV_HALO + ts - V7X_SUBLANES
    for s in range(1, V7X_SUBLANES):
        ext_scr[s, 0:moved, :] = ext_scr[0, s:s + moved, :]
    first = CONV_HALO - (CONV_K - 1)
    for c in range(ts // CONV_ROWS):
        r0 = c * CONV_ROWS
        acc = jnp.broadcast_to(db_ref[...], (CONV_ROWS, CONV_CH))
        for k in range(CONV_K):
            s, base = (first + k) % V7X_SUBLANES, (first + k) // V7X_SUBLANES * V7X_SUBLANES
            acc = acc + dw_ref[k:k + 1, :] * ext_scr[s, r0 + base:r0 + base + CONV_ROWS, :]
        mu = jnp.mean(acc, axis=-1, keepdims=True)
        xc = acc - mu
        var = jnp.mean(xc * xc, axis=-1, keepdims=True)
        y = xc * lax.rsqrt(var + EPS) * lg_ref[...] + lb_ref[...]
        y_scr[r0:r0 + CONV_ROWS, :] = (y * jax.nn.sigmoid(y)).astype(BF16)
    z = jnp.dot(y_scr[...], pw_ref[...], preferred_element_type=F32)
    o_ref[0] = _rms(z, og_ref[...]).astype(BF16)


def _conv(u, dw_w, dw_b, ln_g, ln_b, pw_w, out_g):
    b, s, c = u.shape
    ts = min(256, s)
    per = ts // CONV_HALO
    vec = pl.BlockSpec((1, c), lambda bi, i: (0, 0))
    return pl.pallas_call(
        functools.partial(_conv_body, ts=ts),
        grid=(b, s // ts),
        in_specs=[pl.BlockSpec((1, ts, c), lambda bi, i: (bi, i, 0)),
                  pl.BlockSpec((1, CONV_HALO, c), lambda bi, i: (bi, jnp.maximum(i * per - 1, 0), 0)),
                  pl.BlockSpec((CONV_K, c), lambda bi, i: (0, 0)),
                  vec, vec, vec,
                  pl.BlockSpec((c, c), lambda bi, i: (0, 0)),
                  vec],
        out_specs=pl.BlockSpec((1, ts, c), lambda bi, i: (bi, i, 0)),
        out_shape=jax.ShapeDtypeStruct((b, s, c), BF16),
        scratch_shapes=[pltpu.VMEM((V7X_SUBLANES, CONV_HALO + ts, c), F32), pltpu.VMEM((ts, c), BF16)],
        compiler_params=_params(("parallel", "parallel"), 32),
        name="conv",
    )(u, u, dw_w, dw_b.reshape(1, c), ln_g.reshape(1, c), ln_b.reshape(1, c), pw_w.astype(BF16),
      out_g.reshape(1, c))


def _compress_one(u_ref, pos_ref, w1_ref, w2_ref, o_ref):
    u = u_ref[0, 0]
    half = u.shape[1]
    nu = u.shape[0]
    top = jnp.dot((u + pos_ref[0:1, :]).astype(BF16), w1_ref[0:half, :], preferred_element_type=F32)
    bot = jnp.dot((u + pos_ref[1:2, :]).astype(BF16), w1_ref[half:2 * half, :], preferred_element_type=F32)
    hid = top + pltpu.roll(bot, nu - 1, 0)
    hid = hid * jax.nn.sigmoid(hid)
    o_ref[0, 0] = jnp.dot(hid.astype(BF16), w2_ref[...], preferred_element_type=F32).astype(BF16)


def _compress_body(uk_ref, uv_ref, pk_ref, pv_ref, kw1_ref, kw2_ref, vw1_ref, vw2_ref, ok_ref, ov_ref):
    _compress_one(uk_ref, pk_ref, kw1_ref, kw2_ref, ok_ref)
    _compress_one(uv_ref, pv_ref, vw1_ref, vw2_ref, ov_ref)


def _compress(kc, vc, pos_k, pos_v, kw1, kw2, vw1, vw2):
    b, g, nu, unit = kc.shape
    dh = unit // CMP_STRIDE
    pos = lambda p: p.reshape(CMP_BLOCK // CMP_STRIDE, unit)
    u_spec = pl.BlockSpec((1, 1, nu, unit), lambda bi, gi: (bi, gi, 0, 0))
    full = lambda shape: pl.BlockSpec(shape, lambda bi, gi: (0,) * len(shape))
    o_spec = pl.BlockSpec((1, 1, nu, dh), lambda bi, gi: (bi, gi, 0, 0))
    o_shape = jax.ShapeDtypeStruct((b, g, nu, dh), BF16)
    return pl.pallas_call(
        _compress_body,
        grid=(b, g),
        in_specs=[u_spec, u_spec, full((2, unit)), full((2, unit)),
                  full((CMP_BLOCK * dh, CMP_HIDDEN)), full((CMP_HIDDEN, dh)),
                  full((CMP_BLOCK * dh, CMP_HIDDEN)), full((CMP_HIDDEN, dh))],
        out_specs=[o_spec, o_spec],
        out_shape=[o_shape, o_shape],
        compiler_params=_params(("parallel", "parallel"), 40),
        name="compress",
    )(kc, vc, pos(pos_k), pos(pos_v),
      kw1.astype(BF16), kw2.astype(BF16), vw1.astype(BF16), vw2.astype(BF16))


def _cmp_select_prefix(q_ref, kc_ref, vc_ref, ct_ref, ocmp_ref, sel_ref, *, tq, nc, nb):
    for sub in range(tq // CMP_SUB):
        _cmp_select_sub(q_ref, kc_ref, vc_ref, ct_ref, ocmp_ref, sel_ref,
                        row0=sub * CMP_SUB, q0=pl.program_id(2) * tq + sub * CMP_SUB, tq=CMP_SUB, nc=nc, nb=nb)


def _cmp_select_sub(q_ref, kc_ref, vc_ref, ct_ref, ocmp_ref, sel_ref, *, row0, q0, tq, nc, nb):
    n_slc = ct_ref.shape[0]
    q = q_ref[0, :, row0:row0 + tq, :].reshape(GROUP * tq, HEAD_DIM)
    s = lax.dot_general(q, kc_ref[0, 0, 0:nc, :], _NT, preferred_element_type=F32).reshape(GROUP, tq, nc)
    t = q0 + lax.broadcasted_iota(jnp.int32, (tq, nc), 0)
    cmp_end = lax.broadcasted_iota(jnp.int32, (tq, nc), 1) * CMP_STRIDE + (CMP_BLOCK - 1)
    s = s + jnp.where(cmp_end <= t, 0.0, MASKED)[None]
    m = jnp.max(s, axis=-1, keepdims=True)
    m = jnp.where(m > 0.5 * MASKED, m, 0.0)
    e = jnp.exp2(s - m)
    p = e * (1.0 / jnp.maximum(jnp.sum(e, axis=-1, keepdims=True), 1e-30))
    o = jnp.dot(p.reshape(GROUP * tq, nc).astype(BF16), vc_ref[0, 0, 0:nc, :], preferred_element_type=F32)
    ocmp_ref[0, :, row0:row0 + tq, :] = o.reshape(GROUP, tq, HEAD_DIM)

    psum = p[0] + p[1] + p[2] + p[3]
    hi = psum.astype(BF16)
    r1 = psum - hi.astype(F32)
    mid = r1.astype(BF16)
    lo = (r1 - mid.astype(F32)).astype(BF16)
    ct = ct_ref[0:nb, 0:nc]
    imp = (lax.dot_general(ct, hi, _NT, preferred_element_type=F32)
           + lax.dot_general(ct, mid, _NT, preferred_element_type=F32)
           + lax.dot_general(ct, lo, _NT, preferred_element_type=F32))

    blk = lax.broadcasted_iota(jnp.int32, (nb, tq), 0)
    jt = (q0 + lax.broadcasted_iota(jnp.int32, (nb, tq), 1)) // SLC_BLOCK
    forced = (blk == 0) | (blk == jt) | (blk == jt - 1)
    val = jnp.where((blk >= 1) & (blk < jt - 1), imp, -1.0)
    sel = jnp.where(forced, 1.0, 0.0)
    for _ in range(N_SELECT - N_FORCED):
        best = jnp.max(val, axis=0, keepdims=True)
        first = jnp.min(jnp.where(val == best, blk, n_slc), axis=0, keepdims=True)
        hit = (blk == first) & (best >= 0.0)
        sel = jnp.where(hit, 1.0, sel)
        val = jnp.where(hit, -1.0, val)
    unsel = jnp.concatenate([1.0 - sel, jnp.ones((n_slc - nb, tq), F32)], axis=0) if nb < n_slc else 1.0 - sel
    sel_ref[0, 0, row0:row0 + tq, :] = unsel.T.astype(BF16)


def _cmp_select_body(q_ref, kc_ref, vc_ref, ct_ref, ocmp_ref, sel_ref, *, tq):
    nu = kc_ref.shape[2]
    n_slc = ct_ref.shape[0]
    per_slc = nu // n_slc
    visible = (pl.program_id(2) * tq + tq - CMP_BLOCK) // CMP_STRIDE + 1
    n_prefix = nu // V7X_LANES
    need = jnp.clip((visible + V7X_LANES - 1) // V7X_LANES, 1, n_prefix)
    for v in range(1, n_prefix + 1):
        nc = v * V7X_LANES
        pl.when(need == v)(functools.partial(
            _cmp_select_prefix, q_ref, kc_ref, vc_ref, ct_ref, ocmp_ref, sel_ref, tq=tq, nc=nc, nb=nc // per_slc))


def _cmp_to_slc_t(nu, n_slc):
    per_slc = SLC_BLOCK // CMP_STRIDE
    c = np.arange(nu)[None, :]
    j = np.arange(n_slc)[:, None]
    m = np.zeros((n_slc, nu), np.float32)
    for unit in range(CMP_BLOCK // CMP_STRIDE):
        m += ((c + unit) // per_slc == j)
    m[:, nu - 1] = 0.0
    return jnp.asarray(m, BF16)


def _cmp_select(q_raw, k_cmp, v_cmp):
    b, _, s, dh = q_raw.shape
    nu = k_cmp.shape[2]
    n_slc = s // SLC_BLOCK
    tq = 4 * CMP_SUB
    for q_end in range(tq, s + 1, tq):
        prefix = -(-((q_end - CMP_BLOCK) // CMP_STRIDE + 1) // V7X_LANES) * V7X_LANES
        assert prefix * n_slc // nu >= (q_end - 1) // SLC_BLOCK + 1
    kv_spec = pl.BlockSpec((1, 1, nu, dh), lambda bi, gi, qi: (bi, gi, 0, 0))
    return pl.pallas_call(
        functools.partial(_cmp_select_body, tq=tq),
        grid=(b, N_KV, s // tq),
        in_specs=[pl.BlockSpec((1, GROUP, tq, dh), lambda bi, gi, qi: (bi, gi, qi, 0)),
                  kv_spec, kv_spec,
                  pl.BlockSpec((n_slc, nu), lambda bi, gi, qi: (0, 0))],
        out_specs=[pl.BlockSpec((1, GROUP, tq, dh), lambda bi, gi, qi: (bi, gi, qi, 0)),
                   pl.BlockSpec((1, 1, tq, n_slc), lambda bi, gi, qi: (bi, gi, qi, 0))],
        out_shape=[jax.ShapeDtypeStruct((b, N_HEADS, s, dh), F32),
                   jax.ShapeDtypeStruct((b, N_KV, s, n_slc), BF16)],
        compiler_params=_params(("parallel", "parallel", "parallel"), 32),
        name="cmp_select",
    )(q_raw, k_cmp, v_cmp, _cmp_to_slc_t(nu, n_slc))


LOWER, UPPER = 0, 1


def _attend_body(q_ref, k_ref, v_ref, tri_ref, unsel_ref, ocmp_ref, g_ref, o_ref,
                 qa_scr, s_scr, m_scr, acc_scr, *, tq, tk):
    q0 = pl.program_id(2) * tq
    rows = GROUP * tq

    unsel = unsel_ref[0, 0]
    for r in range(GROUP):
        qa_scr[r * tq:(r + 1) * tq, 0:HEAD_DIM] = q_ref[0, r]
        qa_scr[r * tq:(r + 1) * tq, HEAD_DIM:2 * HEAD_DIM] = unsel
    m_scr[...] = jnp.full(m_scr.shape, MASKED, F32)
    acc_scr[...] = jnp.zeros(acc_scr.shape, F32)

    last = q0 // tk
    n_win = jnp.minimum(last + 1, WINDOW // tk + 1)
    n_tiles = last + 1 + n_win

    def tile(i):
        branch = (i > last).astype(jnp.int32)
        kt = i - branch * n_win
        return branch, kt, pl.multiple_of(kt * tk, tk)

    def scores(i):
        branch, _, k0 = tile(i)
        s_scr[...] = lax.dot_general(qa_scr[...], k_ref[0, 0, branch, pl.ds(k0, tk), :], _NT,
                                     preferred_element_type=F32)

    def softmax_pv(i, boundary):
        branch, kt, k0 = tile(i)
        if boundary:
            kind = jnp.where(kt == last, LOWER, UPPER)
            s = (s_scr[...].reshape(GROUP, tq, tk) + tri_ref[kind][None]).reshape(rows, tk)
        else:
            s = s_scr[...]
        m_old = m_scr[branch]
        m_new = jnp.maximum(m_old, jnp.max(s, axis=-1, keepdims=True))
        p = jnp.exp2(s - m_new).astype(BF16)
        pv = jnp.dot(p, v_ref[0, 0, branch, pl.ds(k0, tk), :], preferred_element_type=F32)
        acc_scr[branch] = jnp.exp2(m_old - m_new) * acc_scr[branch] + pv
        m_scr[branch] = m_new

    scores(0)

    def interior_step(i, carry):
        softmax_pv(i, False)
        scores(i + 1)
        return carry

    def boundary_step(i, carry):
        softmax_pv(i, True)
        scores(i + 1)
        return carry

    lax.fori_loop(0, last, interior_step, 0)
    lax.fori_loop(last, n_tiles - 1, boundary_step, 0)
    softmax_pv(n_tiles - 1, True)

    def normalized(branch):
        acc = acc_scr[branch]
        return acc[:, 0:HEAD_DIM] * (1.0 / acc[:, HEAD_DIM:2 * HEAD_DIM])

    o_slc = normalized(SLC)
    o_win = normalized(WIN)
    gate = g_ref[0, 0]
    for r in range(GROUP):
        o_ref[0, :, r * HEAD_DIM:(r + 1) * HEAD_DIM] = (
            gate[:, 3 * r:3 * r + 1] * ocmp_ref[0, r]
            + gate[:, 3 * r + 1:3 * r + 2] * o_slc[r * tq:(r + 1) * tq]
            + gate[:, 3 * r + 2:3 * r + 3] * o_win[r * tq:(r + 1) * tq])


def _attend(q_rot, k_all, v_all, unsel, o_cmp, gates):
    b, _, s, dh = q_rot.shape
    n_slc = unsel.shape[3]
    assert n_slc == dh, "the unselected one-hot fills the second half of the augmented contraction"
    tq = tk = WINDOW
    row, col = np.arange(tq)[:, None], np.arange(tk)[None, :]
    tri = jnp.asarray(np.stack([np.where(col <= row, 0.0, MASKED),
                                np.where(col > row, 0.0, MASKED)]), F32)
    q_spec = pl.BlockSpec((1, GROUP, tq, dh), lambda bi, gi, qi: (bi, gi, qi, 0))
    kv_spec = _resident((1, 1, 2, s, 2 * dh), lambda bi, gi, qi: (bi, gi, 0, 0, 0))
    row_spec = lambda w: pl.BlockSpec((1, 1, tq, w), lambda bi, gi, qi: (bi, gi, qi, 0))
    rows = GROUP * tq
    return pl.pallas_call(
        functools.partial(_attend_body, tq=tq, tk=tk),
        grid=(b, N_KV, s // tq),
        in_specs=[q_spec, kv_spec, kv_spec, _resident(tri.shape, lambda bi, gi, qi: (0, 0, 0)),
                  row_spec(n_slc), q_spec, row_spec(V7X_LANES)],
        out_specs=pl.BlockSpec((1, tq, GROUP * dh), lambda bi, gi, qi: (bi, qi, gi)),
        out_shape=jax.ShapeDtypeStruct((b, s, Q_DIM), F32),
        scratch_shapes=[pltpu.VMEM((rows, 2 * dh), BF16),
                        pltpu.VMEM((rows, tk), F32),
                        pltpu.VMEM((2, rows, 1), F32),
                        pltpu.VMEM((2, rows, 2 * dh), F32)],
        compiler_params=_params(("parallel", "parallel", "arbitrary"), 48),
        name="attend",
    )(q_rot, k_all, v_all, tri, unsel, o_cmp, gates)


def _out_proj_body(cn_ref, a_ref, x_ref, gn_ref, wc_ref, wa_ref, o_ref):
    an = _rms(a_ref[...], gn_ref[...]).astype(BF16)
    y = (jnp.dot(cn_ref[...], wc_ref[...], preferred_element_type=F32)
         + jnp.dot(an, wa_ref[...], preferred_element_type=F32))
    o_ref[...] = x_ref[...] + y


def _out_proj(conv_n, attn, x, nsa_g, w_out):
    t, d = x.shape
    tm = min(512, t)
    wc = w_out[:CONV_CH].astype(BF16)
    wa = w_out[CONV_CH:].astype(BF16)
    return pl.pallas_call(
        _out_proj_body,
        grid=(t // tm,),
        in_specs=[pl.BlockSpec((tm, CONV_CH), lambda i: (i, 0)),
                  pl.BlockSpec((tm, Q_DIM), lambda i: (i, 0)),
                  pl.BlockSpec((tm, d), lambda i: (i, 0)),
                  pl.BlockSpec((1, Q_DIM), lambda i: (0, 0)),
                  _resident((CONV_CH, d), lambda i: (0, 0)),
                  _resident((Q_DIM, d), lambda i: (0, 0))],
        out_specs=pl.BlockSpec((tm, d), lambda i: (i, 0)),
        out_shape=jax.ShapeDtypeStruct((t, d), F32),
        compiler_params=_params(("parallel",), 40),
        name="out_proj",
    )(conv_n, attn, x, nsa_g.reshape(1, Q_DIM), wc, wa)


def kernel(x, ffn1_norm, ffn1_w_gate, ffn1_w_up, ffn1_w_down, mix_norm, w_in, cmp_pos_k, cmp_pos_v, cmp_k_w1, cmp_k_w2, cmp_v_w1, cmp_v_w2, conv_dw_w, conv_dw_b, conv_ln_g, conv_ln_b, conv_pw_w, out_norm_conv, out_norm_nsa, w_out, ffn2_norm, ffn2_w_gate, ffn2_w_up, ffn2_w_down, final_norm):
    b, s, d = x.shape
    assert s % (SLC_BLOCK * V7X_LANES) == 0, "selection blocks must fill whole 128-lane rows"
    depth = ffn1_norm.shape[0]
    y = x.reshape(b * s, d)
    for l in range(depth):
        y = _ffn(y, ffn1_norm[l], ffn1_w_gate[l], ffn1_w_up[l], ffn1_w_down[l])
        u, q_raw, q_rot, kc, vc, k_all, v_all, gates = _in_proj(y, mix_norm[l], w_in[l], b, s)
        conv_n = _conv(u.reshape(b, s, CONV_CH), conv_dw_w[l], conv_dw_b[l], conv_ln_g[l], conv_ln_b[l],
                       conv_pw_w[l], out_norm_conv[l])
        k_cmp, v_cmp = _compress(kc, vc, cmp_pos_k[l], cmp_pos_v[l],
                                 cmp_k_w1[l], cmp_k_w2[l], cmp_v_w1[l], cmp_v_w2[l])
        o_cmp, unsel = _cmp_select(q_raw, k_cmp, v_cmp)
        attn = _attend(q_rot, k_all, v_all, unsel, o_cmp, gates)
        y = _out_proj(conv_n.reshape(b * s, CONV_CH), attn.reshape(b * s, Q_DIM), y, out_norm_nsa[l], w_out[l])
        y = _ffn(y, ffn2_norm[l], ffn2_w_gate[l], ffn2_w_up[l], ffn2_w_down[l],
                 final_g=final_norm if l == depth - 1 else None)
    return y.reshape(b, s, d)
```

```python
import functools

import numpy as np
import jax
import jax.numpy as jnp
from jax import lax
from jax.experimental import pallas as pl
from jax.experimental.pallas import tpu as pltpu

F32 = jnp.float32
BF16 = jnp.bfloat16

V7X_LANES = 128
V7X_SUBLANES = 8
V7X_VMEM_BYTES = 64 * 2 ** 20

CONV_CH = 512
N_HEADS = 12
HEAD_DIM = 128
N_KV = 3
GROUP = N_HEADS // N_KV
CONV_K = 31
CMP_BLOCK = 32
CMP_STRIDE = 16
CMP_HIDDEN = 256
SLC_BLOCK = 64
N_SELECT = 16
N_FORCED = 3
WINDOW = 512
ROPE_THETA = 10000.0
EPS = 1e-6
MASKED = -1e30
LOG2_E = 1.4426950408889634
SLC, WIN = 0, 1

KV_DIM = N_KV * HEAD_DIM
Q_DIM = N_HEADS * HEAD_DIM
CMP_SUB = 128
CONV_HALO = 32

_NT = (((1,), (1,)), ((), ()))


def _rms(x, g):
    return x * lax.rsqrt(jnp.mean(x * x, axis=-1, keepdims=True) + EPS) * g


def _params(semantics, vmem_mib):
    return pltpu.CompilerParams(dimension_semantics=semantics, vmem_limit_bytes=vmem_mib * 2 ** 20)


def _resident(shape, index_map):
    return pl.BlockSpec(shape, index_map, pipeline_mode=pl.Buffered(1))


def _ffn_body(*refs, n_main, has_tail, final_norm):
    refs = list(refs)
    x_ref, g_ref, wg_ref, wu_ref, wd_ref = refs[:5]
    del refs[:5]
    if has_tail:
        tail_refs = refs[:3]
        del refs[:3]
    if final_norm:
        fg_ref = refs.pop(0)
    o_ref, h_scr = refs
    j = pl.program_id(1)

    @pl.when(j == 0)
    def _():
        x = x_ref[...]
        h_scr[...] = _rms(x, g_ref[...]).astype(BF16)
        o_ref[...] = x

    def hidden_slab(wg, wu, wd):
        h = h_scr[...]
        a = jnp.dot(h, wg[...], preferred_element_type=F32)
        b = jnp.dot(h, wu[...], preferred_element_type=F32)
        z = (a * jax.nn.sigmoid(a) * b).astype(BF16)
        o_ref[...] += 0.5 * jnp.dot(z, wd[...], preferred_element_type=F32)

    if has_tail:
        pl.when(j < n_main)(functools.partial(hidden_slab, wg_ref, wu_ref, wd_ref))
        pl.when(j == n_main)(functools.partial(hidden_slab, *tail_refs))
    else:
        hidden_slab(wg_ref, wu_ref, wd_ref)

    if final_norm:
        @pl.when(j == n_main + has_tail - 1)
        def _():
            o_ref[...] = _rms(o_ref[...], fg_ref[...])


def _ffn(x, g, w_gate, w_up, w_down, final_g=None):
    t, d = x.shape
    f = w_gate.shape[1]
    tm = min(512, t)
    tf = 512
    n_main, f_tail = divmod(f, tf)
    assert f_tail % V7X_LANES == 0 and n_main >= 1
    has_tail = int(f_tail > 0)
    f_main = n_main * tf
    wg, wu, wd = w_gate.astype(BF16), w_up.astype(BF16), w_down.astype(BF16)
    final_norm = final_g is not None
    row = pl.BlockSpec((tm, d), lambda i, j: (i, 0))
    vec = pl.BlockSpec((1, d), lambda i, j: (0, 0))
    slab = lambda i, j: (0, jnp.minimum(j, n_main - 1))
    in_specs = [row, vec,
                pl.BlockSpec((d, tf), slab),
                pl.BlockSpec((d, tf), slab),
                pl.BlockSpec((tf, d), lambda i, j: (jnp.minimum(j, n_main - 1), 0))]
    args = [x, g.reshape(1, d), wg, wu, wd]
    if has_tail:
        in_specs += [_resident((d, f_tail), lambda i, j: (0, 0)),
                     _resident((d, f_tail), lambda i, j: (0, 0)),
                     _resident((f_tail, d), lambda i, j: (0, 0))]
        args += [wg[:, f_main:], wu[:, f_main:], wd[f_main:]]
    if final_norm:
        in_specs.append(vec)
        args.append(final_g.reshape(1, d))
    return pl.pallas_call(
        functools.partial(_ffn_body, n_main=n_main, has_tail=has_tail, final_norm=final_norm),
        grid=(t // tm, n_main + has_tail),
        in_specs=in_specs,
        out_specs=row,
        out_shape=jax.ShapeDtypeStruct((t, d), F32),
        scratch_shapes=[pltpu.VMEM((tm, d), BF16)],
        compiler_params=_params(("parallel", "arbitrary"), 48),
        name="ffn_final" if final_norm else "ffn",
    )(*args)


def _rope(x, cos2, sin2):
    return x * cos2 + pltpu.roll(x, HEAD_DIM // 2, 1) * sin2


def _in_proj_body(x_ref, g_ref, w_ref, wgate_ref, cos_ref, sin_ref, blk_ref,
                  u_ref, qraw_ref, qrot_ref, kc_ref, vc_ref, k_ref, v_ref, gate_ref, kv_scr):
    h = _rms(x_ref[...], g_ref[...]).astype(BF16)
    cos2 = cos_ref[...]
    sin2 = sin_ref[...]
    scale = HEAD_DIM ** -0.5

    def proj(c0, width):
        return jnp.dot(h, w_ref[:, c0:c0 + width], preferred_element_type=F32)

    glu = proj(0, 2 * CONV_CH)
    u_ref[...] = glu[:, :CONV_CH] * jax.nn.sigmoid(glu[:, CONV_CH:])

    def head(cols, i):
        return cols[:, i * HEAD_DIM:(i + 1) * HEAD_DIM]

    c0 = 2 * CONV_CH
    for gk in range(N_KV):
        cols = proj(c0, GROUP * HEAD_DIM) * (scale * LOG2_E)
        for r in range(GROUP):
            qh = head(cols, r)
            qraw_ref[0, gk * GROUP + r] = qh.astype(BF16)
            qrot_ref[0, gk * GROUP + r] = _rope(qh, cos2, sin2).astype(BF16)
        c0 += GROUP * HEAD_DIM
    cols = proj(c0, 2 * KV_DIM)
    n_unit = kv_scr.shape[1] // CMP_STRIDE
    for ref, first in ((kc_ref, 0), (vc_ref, N_KV)):
        for gk in range(N_KV):
            kv_scr[first + gk] = head(cols, first + gk)
            for slot in range(CMP_STRIDE):
                ref[0, gk, :, slot * HEAD_DIM:(slot + 1) * HEAD_DIM] = kv_scr[
                    first + gk, pl.ds(slot, n_unit, stride=CMP_STRIDE), :]
    c0 += 2 * KV_DIM
    lo, hi = slice(0, HEAD_DIM), slice(HEAD_DIM, 2 * HEAD_DIM)
    for branch in (SLC, WIN):
        cols = proj(c0, 2 * KV_DIM)
        for gk in range(N_KV):
            k_ref[0, gk, branch, :, lo] = _rope(head(cols, gk), cos2, sin2).astype(BF16)
            v_ref[0, gk, branch, :, lo] = head(cols, N_KV + gk).astype(BF16)
        c0 += 2 * KV_DIM
    cols = jax.nn.sigmoid(jnp.dot(h, wgate_ref[...], preferred_element_type=F32))
    for gk in range(N_KV):
        k_ref[0, gk, SLC, :, hi] = blk_ref[...]
        k_ref[0, gk, WIN, :, hi] = jnp.zeros(blk_ref.shape, BF16)
        v_ref[0, gk, SLC, :, hi] = jnp.ones(blk_ref.shape, BF16)
        v_ref[0, gk, WIN, :, hi] = jnp.ones(blk_ref.shape, BF16)
        gate_ref[0, gk] = head(cols, gk)


def _in_proj(x, g, w_in, batch, seq):
    t, d = x.shape
    tm = min(256, seq)
    n_s = seq // tm
    main = 2 * CONV_CH + Q_DIM + 6 * KV_DIM
    gate_w = w_in[:, main:].reshape(d, N_KV, GROUP * 3)
    gate_w = jnp.pad(gate_w, ((0, 0), (0, 0), (0, V7X_LANES - GROUP * 3))).reshape(d, N_KV * V7X_LANES)
    gate_w = gate_w.astype(BF16)
    w = w_in[:, :main].astype(BF16)

    inv = jnp.power(ROPE_THETA, -jnp.arange(0, HEAD_DIM, 2, dtype=F32) / HEAD_DIM)
    ang = jnp.arange(seq, dtype=F32)[:, None] * inv[None, :]
    cos2 = jnp.concatenate([jnp.cos(ang), jnp.cos(ang)], axis=1)
    sin2 = jnp.concatenate([-jnp.sin(ang), jnp.sin(ang)], axis=1)

    key_blk = np.arange(seq)[:, None] // SLC_BLOCK == np.arange(seq // SLC_BLOCK)[None, :]
    blk_mask = jnp.asarray(np.where(key_blk, -(2.0 ** 100), 0.0), BF16)

    def heads(n, dtype):
        return (jax.ShapeDtypeStruct((batch, n, seq, HEAD_DIM), dtype),
                pl.BlockSpec((1, n, tm, HEAD_DIM), lambda i: (i // n_s, 0, i % n_s, 0)))

    stacked = (jax.ShapeDtypeStruct((batch, N_KV, 2, seq, 2 * HEAD_DIM), BF16),
               pl.BlockSpec((1, N_KV, 2, tm, 2 * HEAD_DIM), lambda i: (i // n_s, 0, 0, i % n_s, 0)))
    unit = CMP_STRIDE * HEAD_DIM
    units = (jax.ShapeDtypeStruct((batch, N_KV, seq // CMP_STRIDE, unit), F32),
             pl.BlockSpec((1, N_KV, tm // CMP_STRIDE, unit), lambda i: (i // n_s, 0, i % n_s, 0)))
    outs = [(jax.ShapeDtypeStruct((t, CONV_CH), F32), pl.BlockSpec((tm, CONV_CH), lambda i: (i, 0))),
            heads(N_HEADS, BF16), heads(N_HEADS, BF16),
            units, units,
            stacked, stacked,
            heads(N_KV, F32)]
    table = pl.BlockSpec((tm, HEAD_DIM), lambda i: (i % n_s, 0))
    return pl.pallas_call(
        _in_proj_body,
        grid=(t // tm,),
        in_specs=[pl.BlockSpec((tm, d), lambda i: (i, 0)),
                  pl.BlockSpec((1, d), lambda i: (0, 0)),
                  _resident((d, main), lambda i: (0, 0)),
                  _resident(gate_w.shape, lambda i: (0, 0)),
                  table, table, table],
        out_specs=[o[1] for o in outs],
        out_shape=[o[0] for o in outs],
        scratch_shapes=[pltpu.VMEM((2 * N_KV, tm, HEAD_DIM), F32)],
        compiler_params=_params(("parallel",), 48),
        name="in_proj",
    )(x, g.reshape(1, d), w, gate_w, cos2, sin2, blk_mask)


CONV_ROWS = 32


def _conv_body(u_ref, halo_ref, dw_ref, db_ref, lg_ref, lb_ref, pw_ref, og_ref, o_ref, ext_scr, y_scr, *, ts):
    i = pl.program_id(1)
    ext_scr[0, 0:CONV_HALO, :] = jnp.where(i == 0, 0.0, halo_ref[0])
    ext_scr[0, CONV_HALO:CONV_HALO + ts, :] = u_ref[0]
    moved = CONV_HALO + ts - V7X_SUBLANES
    for s in range(1, V7X_SUBLANES):
        ext_scr[s, 0:moved, :] = ext_scr[0, s:s + moved, :]
    first = CONV_HALO - (CONV_K - 1)
    for c in range(ts // CONV_ROWS):
        r0 = c * CONV_ROWS
        acc = jnp.broadcast_to(db_ref[...], (CONV_ROWS, CONV_CH))
        for k in range(CONV_K):
            s, base = (first + k) % V7X_SUBLANES, (first + k) // V7X_SUBLANES * V7X_SUBLANES
            acc = acc + dw_ref[k:k + 1, :] * ext_scr[s, r0 + base:r0 + base + CONV_ROWS, :]
        mu = jnp.mean(acc, axis=-1, keepdims=True)
        xc = acc - mu
        var = jnp.mean(xc * xc, axis=-1, keepdims=True)
        y = xc * lax.rsqrt(var + EPS) * lg_ref[...] + lb_ref[...]
        y_scr[r0:r0 + CONV_ROWS, :] = (y * jax.nn.sigmoid(y)).astype(BF16)
    z = jnp.dot(y_scr[...], pw_ref[...], preferred_element_type=F32)
    o_ref[0] = _rms(z, og_ref[...]).astype(BF16)


def _conv(u, dw_w, dw_b, ln_g, ln_b, pw_w, out_g):
    b, s, c = u.shape
    ts = min(256, s)
    per = ts // CONV_HALO
    vec = pl.BlockSpec((1, c), lambda bi, i: (0, 0))
    return pl.pallas_call(
        functools.partial(_conv_body, ts=ts),
        grid=(b, s // ts),
        in_specs=[pl.BlockSpec((1, ts, c), lambda bi, i: (bi, i, 0)),
                  pl.BlockSpec((1, CONV_HALO, c), lambda bi, i: (bi, jnp.maximum(i * per - 1, 0), 0)),
                  pl.BlockSpec((CONV_K, c), lambda bi, i: (0, 0)),
                  vec, vec, vec,
                  pl.BlockSpec((c, c), lambda bi, i: (0, 0)),
                  vec],
        out_specs=pl.BlockSpec((1, ts, c), lambda bi, i: (bi, i, 0)),
        out_shape=jax.ShapeDtypeStruct((b, s, c), BF16),
        scratch_shapes=[pltpu.VMEM((V7X_SUBLANES, CONV_HALO + ts, c), F32), pltpu.VMEM((ts, c), BF16)],
        compiler_params=_params(("parallel", "parallel"), 32),
        name="conv",
    )(u, u, dw_w, dw_b.reshape(1, c), ln_g.reshape(1, c), ln_b.reshape(1, c), pw_w.astype(BF16),
      out_g.reshape(1, c))


def _compress_one(u_ref, pos_ref, w1_ref, w2_ref, o_ref):
    u = u_ref[0, 0]
    half = u.shape[1]
    nu = u.shape[0]
    top = jnp.dot((u + pos_ref[0:1, :]).astype(BF16), w1_ref[0:half, :], preferred_element_type=F32)
    bot = jnp.dot((u + pos_ref[1:2, :]).astype(BF16), w1_ref[half:2 * half, :], preferred_element_type=F32)
    hid = top + pltpu.roll(bot, nu - 1, 0)
    hid = hid * jax.nn.sigmoid(hid)
    o_ref[0, 0] = jnp.dot(hid.astype(BF16), w2_ref[...], preferred_element_type=F32).astype(BF16)


def _compress_body(uk_ref, uv_ref, pk_ref, pv_ref, kw1_ref, kw2_ref, vw1_ref, vw2_ref, ok_ref, ov_ref):
    _compress_one(uk_ref, pk_ref, kw1_ref, kw2_ref, ok_ref)
    _compress_one(uv_ref, pv_ref, vw1_ref, vw2_ref, ov_ref)


def _compress(kc, vc, pos_k, pos_v, kw1, kw2, vw1, vw2):
    b, g, nu, unit = kc.shape
    dh = unit // CMP_STRIDE
    pos = lambda p: p.reshape(CMP_BLOCK // CMP_STRIDE, unit)
    u_spec = pl.BlockSpec((1, 1, nu, unit), lambda bi, gi: (bi, gi, 0, 0))
    full = lambda shape: pl.BlockSpec(shape, lambda bi, gi: (0,) * len(shape))
    o_spec = pl.BlockSpec((1, 1, nu, dh), lambda bi, gi: (bi, gi, 0, 0))
    o_shape = jax.ShapeDtypeStruct((b, g, nu, dh), BF16)
    return pl.pallas_call(
        _compress_body,
        grid=(b, g),
        in_specs=[u_spec, u_spec, full((2, unit)), full((2, unit)),
                  full((CMP_BLOCK * dh, CMP_HIDDEN)), full((CMP_HIDDEN, dh)),
                  full((CMP_BLOCK * dh, CMP_HIDDEN)), full((CMP_HIDDEN, dh))],
        out_specs=[o_spec, o_spec],
        out_shape=[o_shape, o_shape],
        compiler_params=_params(("parallel", "parallel"), 40),
        name="compress",
    )(kc, vc, pos(pos_k), pos(pos_v),
      kw1.astype(BF16), kw2.astype(BF16), vw1.astype(BF16), vw2.astype(BF16))


def _cmp_select_prefix(q_ref, kc_ref, vc_ref, ct_ref, ocmp_ref, sel_ref, *, tq, nc, nb):
    for sub in range(tq // CMP_SUB):
        _cmp_select_sub(q_ref, kc_ref, vc_ref, ct_ref, ocmp_ref, sel_ref,
                        row0=sub * CMP_SUB, q0=pl.program_id(2) * tq + sub * CMP_SUB, tq=CMP_SUB, nc=nc, nb=nb)


def _cmp_select_sub(q_ref, kc_ref, vc_ref, ct_ref, ocmp_ref, sel_ref, *, row0, q0, tq, nc, nb):
    n_slc = ct_ref.shape[0]
    q = q_ref[0, :, row0:row0 + tq, :].reshape(GROUP * tq, HEAD_DIM)
    s = lax.dot_general(q, kc_ref[0, 0, 0:nc, :], _NT, preferred_element_type=F32).reshape(GROUP, tq, nc)
    t = q0 + lax.broadcasted_iota(jnp.int32, (tq, nc), 0)
    cmp_end = lax.broadcasted_iota(jnp.int32, (tq, nc), 1) * CMP_STRIDE + (CMP_BLOCK - 1)
    s = s + jnp.where(cmp_end <= t, 0.0, MASKED)[None]
    m = jnp.max(s, axis=-1, keepdims=True)
    m = jnp.where(m > 0.5 * MASKED, m, 0.0)
    e = jnp.exp2(s - m)
    p = e * (1.0 / jnp.maximum(jnp.sum(e, axis=-1, keepdims=True), 1e-30))
    o = jnp.dot(p.reshape(GROUP * tq, nc).astype(BF16), vc_ref[0, 0, 0:nc, :], preferred_element_type=F32)
    ocmp_ref[0, :, row0:row0 + tq, :] = o.reshape(GROUP, tq, HEAD_DIM)

    psum = p[0] + p[1] + p[2] + p[3]
    hi = psum.astype(BF16)
    r1 = psum - hi.astype(F32)
    mid = r1.astype(BF16)
    lo = (r1 - mid.astype(F32)).astype(BF16)
    ct = ct_ref[0:nb, 0:nc]
    imp = (lax.dot_general(ct, hi, _NT, preferred_element_type=F32)
           + lax.dot_general(ct, mid, _NT, preferred_element_type=F32)
           + lax.dot_general(ct, lo, _NT, preferred_element_type=F32))

    blk = lax.broadcasted_iota(jnp.int32, (nb, tq), 0)
    jt = (q0 + lax.broadcasted_iota(jnp.int32, (nb, tq), 1)) // SLC_BLOCK
    forced = (blk == 0) | (blk == jt) | (blk == jt - 1)
    candidate = (blk >= 1) & (blk < jt - 1)
    val = jnp.where(candidate, imp, -1.0)
    for _ in range(N_SELECT - N_FORCED):
        best = jnp.max(val, axis=0, keepdims=True)
        first = jnp.min(jnp.where(val == best, blk, n_slc), axis=0, keepdims=True)
        val = jnp.where((blk == first) & (best >= 0.0), -1.0, val)
    unsel = jnp.where(forced | (candidate & (val < 0.0)), 0.0, 1.0)
    if nb < n_slc:
        unsel = jnp.concatenate([unsel, jnp.ones((n_slc - nb, tq), F32)], axis=0)
    sel_ref[0, 0, row0:row0 + tq, :] = unsel.T.astype(BF16)


def _cmp_select_body(q_ref, kc_ref, vc_ref, ct_ref, ocmp_ref, sel_ref, *, tq):
    nu = kc_ref.shape[2]
    n_slc = ct_ref.shape[0]
    per_slc = nu // n_slc
    visible = (pl.program_id(2) * tq + tq - CMP_BLOCK) // CMP_STRIDE + 1
    n_prefix = nu // V7X_LANES
    need = jnp.clip((visible + V7X_LANES - 1) // V7X_LANES, 1, n_prefix)
    for v in range(1, n_prefix + 1):
        nc = v * V7X_LANES
        pl.when(need == v)(functools.partial(
            _cmp_select_prefix, q_ref, kc_ref, vc_ref, ct_ref, ocmp_ref, sel_ref, tq=tq, nc=nc, nb=nc // per_slc))


def _cmp_to_slc_t(nu, n_slc):
    per_slc = SLC_BLOCK // CMP_STRIDE
    c = np.arange(nu)[None, :]
    j = np.arange(n_slc)[:, None]
    m = np.zeros((n_slc, nu), np.float32)
    for unit in range(CMP_BLOCK // CMP_STRIDE):
        m += ((c + unit) // per_slc == j)
    m[:, nu - 1] = 0.0
    return jnp.asarray(m, BF16)


def _cmp_select(q_raw, k_cmp, v_cmp):
    b, _, s, dh = q_raw.shape
    nu = k_cmp.shape[2]
    n_slc = s // SLC_BLOCK
    tq = 4 * CMP_SUB
    for q_end in range(tq, s + 1, tq):
        prefix = -(-((q_end - CMP_BLOCK) // CMP_STRIDE + 1) // V7X_LANES) * V7X_LANES
        assert prefix * n_slc // nu >= (q_end - 1) // SLC_BLOCK + 1
    kv_spec = pl.BlockSpec((1, 1, nu, dh), lambda bi, gi, qi: (bi, gi, 0, 0))
    return pl.pallas_call(
        functools.partial(_cmp_select_body, tq=tq),
        grid=(b, N_KV, s // tq),
        in_specs=[pl.BlockSpec((1, GROUP, tq, dh), lambda bi, gi, qi: (bi, gi, qi, 0)),
                  kv_spec, kv_spec,
                  pl.BlockSpec((n_slc, nu), lambda bi, gi, qi: (0, 0))],
        out_specs=[pl.BlockSpec((1, GROUP, tq, dh), lambda bi, gi, qi: (bi, gi, qi, 0)),
                   pl.BlockSpec((1, 1, tq, n_slc), lambda bi, gi, qi: (bi, gi, qi, 0))],
        out_shape=[jax.ShapeDtypeStruct((b, N_HEADS, s, dh), F32),
                   jax.ShapeDtypeStruct((b, N_KV, s, n_slc), BF16)],
        compiler_params=_params(("parallel", "parallel", "parallel"), 32),
        name="cmp_select",
    )(q_raw, k_cmp, v_cmp, _cmp_to_slc_t(nu, n_slc))


LOWER, UPPER = 0, 1


def _attend_body(q_ref, k_ref, v_ref, tri_ref, unsel_ref, ocmp_ref, g_ref, o_ref,
                 qa_scr, s_scr, m_scr, acc_scr, *, tq, tk):
    q0 = pl.program_id(2) * tq
    rows = GROUP * tq

    unsel = unsel_ref[0, 0]
    for r in range(GROUP):
        qa_scr[r * tq:(r + 1) * tq, 0:HEAD_DIM] = q_ref[0, r]
        qa_scr[r * tq:(r + 1) * tq, HEAD_DIM:2 * HEAD_DIM] = unsel
    m_scr[...] = jnp.full(m_scr.shape, MASKED, F32)
    acc_scr[...] = jnp.zeros(acc_scr.shape, F32)

    last = q0 // tk
    n_win = jnp.minimum(last + 1, WINDOW // tk + 1)
    n_tiles = last + 1 + n_win

    def tile(i):
        branch = (i > last).astype(jnp.int32)
        kt = i - branch * n_win
        return branch, kt, pl.multiple_of(kt * tk, tk)

    def scores(i):
        branch, _, k0 = tile(i)
        s_scr[...] = lax.dot_general(qa_scr[...], k_ref[0, 0, branch, pl.ds(k0, tk), :], _NT,
                                     preferred_element_type=F32)

    def softmax_pv(i, boundary):
        branch, kt, k0 = tile(i)
        if boundary:
            kind = jnp.where(kt == last, LOWER, UPPER)
            s = (s_scr[...].reshape(GROUP, tq, tk) + tri_ref[kind][None]).reshape(rows, tk)
        else:
            s = s_scr[...]
        m_old = m_scr[branch]
        m_new = jnp.maximum(m_old, jnp.max(s, axis=-1, keepdims=True))
        p = jnp.exp2((s - m_new).astype(BF16))
        pv = jnp.dot(p, v_ref[0, 0, branch, pl.ds(k0, tk), :], preferred_element_type=F32)
        acc_scr[branch] = jnp.exp2(m_old - m_new) * acc_scr[branch] + pv
        m_scr[branch] = m_new

    scores(0)

    def interior_step(i, carry):
        softmax_pv(i, False)
        scores(i + 1)
        return carry

    def boundary_step(i, carry):
        softmax_pv(i, True)
        scores(i + 1)
        return carry

    lax.fori_loop(0, last, interior_step, 0)
    lax.fori_loop(last, n_tiles - 1, boundary_step, 0)
    softmax_pv(n_tiles - 1, True)

    def normalized(branch):
        acc = acc_scr[branch]
        return acc[:, 0:HEAD_DIM] * (1.0 / acc[:, HEAD_DIM:2 * HEAD_DIM])

    o_slc = normalized(SLC)
    o_win = normalized(WIN)
    gate = g_ref[0, 0]
    for r in range(GROUP):
        o_ref[0, :, r * HEAD_DIM:(r + 1) * HEAD_DIM] = (
            gate[:, 3 * r:3 * r + 1] * ocmp_ref[0, r]
            + gate[:, 3 * r + 1:3 * r + 2] * o_slc[r * tq:(r + 1) * tq]
            + gate[:, 3 * r + 2:3 * r + 3] * o_win[r * tq:(r + 1) * tq])


def _attend(q_rot, k_all, v_all, unsel, o_cmp, gates):
    b, _, s, dh = q_rot.shape
    n_slc = unsel.shape[3]
    assert n_slc == dh, "the unselected one-hot fills the second half of the augmented contraction"
    tq = tk = WINDOW
    row, col = np.arange(tq)[:, None], np.arange(tk)[None, :]
    tri = jnp.asarray(np.stack([np.where(col <= row, 0.0, MASKED),
                                np.where(col > row, 0.0, MASKED)]), F32)
    q_spec = pl.BlockSpec((1, GROUP, tq, dh), lambda bi, gi, qi: (bi, gi, qi, 0))
    kv_spec = _resident((1, 1, 2, s, 2 * dh), lambda bi, gi, qi: (bi, gi, 0, 0, 0))
    row_spec = lambda w: pl.BlockSpec((1, 1, tq, w), lambda bi, gi, qi: (bi, gi, qi, 0))
    rows = GROUP * tq
    return pl.pallas_call(
        functools.partial(_attend_body, tq=tq, tk=tk),
        grid=(b, N_KV, s // tq),
        in_specs=[q_spec, kv_spec, kv_spec, _resident(tri.shape, lambda bi, gi, qi: (0, 0, 0)),
                  row_spec(n_slc), q_spec, row_spec(V7X_LANES)],
        out_specs=pl.BlockSpec((1, tq, GROUP * dh), lambda bi, gi, qi: (bi, qi, gi)),
        out_shape=jax.ShapeDtypeStruct((b, s, Q_DIM), F32),
        scratch_shapes=[pltpu.VMEM((rows, 2 * dh), BF16),
                        pltpu.VMEM((rows, tk), F32),
                        pltpu.VMEM((2, rows, 1), F32),
                        pltpu.VMEM((2, rows, 2 * dh), F32)],
        compiler_params=_params(("parallel", "parallel", "arbitrary"), 48),
        name="attend",
    )(q_rot, k_all, v_all, tri, unsel, o_cmp, gates)


def _out_proj_body(cn_ref, a_ref, x_ref, gn_ref, wc_ref, wa_ref, o_ref):
    an = _rms(a_ref[...], gn_ref[...]).astype(BF16)
    y = (jnp.dot(cn_ref[...], wc_ref[...], preferred_element_type=F32)
         + jnp.dot(an, wa_ref[...], preferred_element_type=F32))
    o_ref[...] = x_ref[...] + y


def _out_proj(conv_n, attn, x, nsa_g, w_out):
    t, d = x.shape
    tm = min(512, t)
    wc = w_out[:CONV_CH].astype(BF16)
    wa = w_out[CONV_CH:].astype(BF16)
    return pl.pallas_call(
        _out_proj_body,
        grid=(t // tm,),
        in_specs=[pl.BlockSpec((tm, CONV_CH), lambda i: (i, 0)),
                  pl.BlockSpec((tm, Q_DIM), lambda i: (i, 0)),
                  pl.BlockSpec((tm, d), lambda i: (i, 0)),
                  pl.BlockSpec((1, Q_DIM), lambda i: (0, 0)),
                  _resident((CONV_CH, d), lambda i: (0, 0)),
                  _resident((Q_DIM, d), lambda i: (0, 0))],
        out_specs=pl.BlockSpec((tm, d), lambda i: (i, 0)),
        out_shape=jax.ShapeDtypeStruct((t, d), F32),
        compiler_params=_params(("parallel",), 40),
        name="out_proj",
    )(conv_n, attn, x, nsa_g.reshape(1, Q_DIM), wc, wa)


def kernel(x, ffn1_norm, ffn1_w_gate, ffn1_w_up, ffn1_w_down, mix_norm, w_in, cmp_pos_k, cmp_pos_v, cmp_k_w1, cmp_k_w2, cmp_v_w1, cmp_v_w2, conv_dw_w, conv_dw_b, conv_ln_g, conv_ln_b, conv_pw_w, out_norm_conv, out_norm_nsa, w_out, ffn2_norm, ffn2_w_gate, ffn2_w_up, ffn2_w_down, final_norm):
    b, s, d = x.shape
    assert s % (SLC_BLOCK * V7X_LANES) == 0, "selection blocks must fill whole 128-lane rows"
    depth = ffn1_norm.shape[0]
    y = x.reshape(b * s, d)
    for l in range(depth):
        y = _ffn(y, ffn1_norm[l], ffn1_w_gate[l], ffn1_w_up[l], ffn1_w_down[l])
        u, q_raw, q_rot, kc, vc, k_all, v_all, gates = _in_proj(y, mix_norm[l], w_in[l], b, s)
        conv_n = _conv(u.reshape(b, s, CONV_CH), conv_dw_w[l], conv_dw_b[l], conv_ln_g[l], conv_ln_b[l],
                       conv_pw_w[l], out_norm_conv[l])
        k_cmp, v_cmp = _compress(kc, vc, cmp_pos_k[l], cmp_pos_v[l],
                                 cmp_k_w1[l], cmp_k_w2[l], cmp_v_w1[l], cmp_v_w2[l])
        o_cmp, unsel = _cmp_select(q_raw, k_cmp, v_cmp)
        attn = _attend(q_rot, k_all, v_all, unsel, o_cmp, gates)
        y = _out_proj(conv_n.reshape(b * s, CONV_CH), attn.reshape(b * s, Q_DIM), y, out_norm_nsa[l], w_out[l])
        y = _ffn(y, ffn2_norm[l], ffn2_w_gate[l], ffn2_w_up[l], ffn2_w_down[l],
                 final_g=final_norm if l == depth - 1 else None)
    return y.reshape(b, s, d)
```

```python
import functools

import numpy as np
import jax
import jax.numpy as jnp
from jax import lax
from jax.experimental import pallas as pl
from jax.experimental.pallas import tpu as pltpu

F32 = jnp.float32
BF16 = jnp.bfloat16

V7X_LANES = 128
V7X_SUBLANES = 8
V7X_VMEM_BYTES = 64 * 2 ** 20

CONV_CH = 512
N_HEADS = 12
HEAD_DIM = 128
N_KV = 3
GROUP = N_HEADS // N_KV
CONV_K = 31
CMP_BLOCK = 32
CMP_STRIDE = 16
CMP_HIDDEN = 256
SLC_BLOCK = 64
N_SELECT = 16
N_FORCED = 3
WINDOW = 512
ROPE_THETA = 10000.0
EPS = 1e-6
MASKED = -1e30
LOG2_E = 1.4426950408889634
SLC, WIN = 0, 1

KV_DIM = N_KV * HEAD_DIM
Q_DIM = N_HEADS * HEAD_DIM
CMP_SUB = 128
CONV_HALO = 32

_NT = (((1,), (1,)), ((), ()))


def _rms(x, g):
    return x * lax.rsqrt(jnp.mean(x * x, axis=-1, keepdims=True) + EPS) * g


def _params(semantics, vmem_mib):
    return pltpu.CompilerParams(dimension_semantics=semantics, vmem_limit_bytes=vmem_mib * 2 ** 20)


def _resident(shape, index_map):
    return pl.BlockSpec(shape, index_map, pipeline_mode=pl.Buffered(1))


def _ffn_body(*refs, n_main, has_tail, final_norm):
    refs = list(refs)
    x_ref, g_ref, wg_ref, wu_ref, wd_ref = refs[:5]
    del refs[:5]
    if has_tail:
        tail_refs = refs[:3]
        del refs[:3]
    if final_norm:
        fg_ref = refs.pop(0)
    o_ref, h_scr = refs
    j = pl.program_id(1)

    @pl.when(j == 0)
    def _():
        x = x_ref[...]
        h_scr[...] = _rms(x, g_ref[...]).astype(BF16)
        o_ref[...] = x

    def hidden_slab(wg, wu, wd):
        h = h_scr[...]
        a = jnp.dot(h, wg[...], preferred_element_type=F32)
        b = jnp.dot(h, wu[...], preferred_element_type=F32)
        z = (a * jax.nn.sigmoid(a) * b).astype(BF16)
        o_ref[...] += 0.5 * jnp.dot(z, wd[...], preferred_element_type=F32)

    if has_tail:
        pl.when(j < n_main)(functools.partial(hidden_slab, wg_ref, wu_ref, wd_ref))
        pl.when(j == n_main)(functools.partial(hidden_slab, *tail_refs))
    else:
        hidden_slab(wg_ref, wu_ref, wd_ref)

    if final_norm:
        @pl.when(j == n_main + has_tail - 1)
        def _():
            o_ref[...] = _rms(o_ref[...], fg_ref[...])


def _ffn(x, g, w_gate, w_up, w_down, final_g=None):
    t, d = x.shape
    f = w_gate.shape[1]
    tm = min(512, t)
    tf = 512
    n_main, f_tail = divmod(f, tf)
    assert f_tail % V7X_LANES == 0 and n_main >= 1
    has_tail = int(f_tail > 0)
    f_main = n_main * tf
    wg, wu, wd = w_gate.astype(BF16), w_up.astype(BF16), w_down.astype(BF16)
    final_norm = final_g is not None
    row = pl.BlockSpec((tm, d), lambda i, j: (i, 0))
    vec = pl.BlockSpec((1, d), lambda i, j: (0, 0))
    slab = lambda i, j: (0, jnp.minimum(j, n_main - 1))
    in_specs = [row, vec,
                pl.BlockSpec((d, tf), slab),
                pl.BlockSpec((d, tf), slab),
                pl.BlockSpec((tf, d), lambda i, j: (jnp.minimum(j, n_main - 1), 0))]
    args = [x, g.reshape(1, d), wg, wu, wd]
    if has_tail:
        in_specs += [_resident((d, f_tail), lambda i, j: (0, 0)),
                     _resident((d, f_tail), lambda i, j: (0, 0)),
                     _resident((f_tail, d), lambda i, j: (0, 0))]
        args += [wg[:, f_main:], wu[:, f_main:], wd[f_main:]]
    if final_norm:
        in_specs.append(vec)
        args.append(final_g.reshape(1, d))
    return pl.pallas_call(
        functools.partial(_ffn_body, n_main=n_main, has_tail=has_tail, final_norm=final_norm),
        grid=(t // tm, n_main + has_tail),
        in_specs=in_specs,
        out_specs=row,
        out_shape=jax.ShapeDtypeStruct((t, d), F32),
        scratch_shapes=[pltpu.VMEM((tm, d), BF16)],
        compiler_params=_params(("parallel", "arbitrary"), 48),
        name="ffn_final" if final_norm else "ffn",
    )(*args)


def _rope(x, cos2, sin2):
    return x * cos2 + pltpu.roll(x, HEAD_DIM // 2, 1) * sin2


def _in_proj_body(x_ref, g_ref, w_ref, wgate_ref, cos_ref, sin_ref, blk_ref,
                  u_ref, qraw_ref, qrot_ref, kc_ref, vc_ref, k_ref, v_ref, gate_ref, kv_scr):
    h = _rms(x_ref[...], g_ref[...]).astype(BF16)
    cos2 = cos_ref[...]
    sin2 = sin_ref[...]
    scale = HEAD_DIM ** -0.5

    def proj(c0, width):
        return jnp.dot(h, w_ref[:, c0:c0 + width], preferred_element_type=F32)

    glu = proj(0, 2 * CONV_CH)
    u_ref[...] = glu[:, :CONV_CH] * jax.nn.sigmoid(glu[:, CONV_CH:])

    def head(cols, i):
        return cols[:, i * HEAD_DIM:(i + 1) * HEAD_DIM]

    c0 = 2 * CONV_CH
    for gk in range(N_KV):
        cols = proj(c0, GROUP * HEAD_DIM) * (scale * LOG2_E)
        for r in range(GROUP):
            qh = head(cols, r)
            qraw_ref[0, gk * GROUP + r] = qh.astype(BF16)
            qrot_ref[0, gk * GROUP + r] = _rope(qh, cos2, sin2).astype(BF16)
        c0 += GROUP * HEAD_DIM
    cols = proj(c0, 2 * KV_DIM)
    n_unit = kv_scr.shape[1] // CMP_STRIDE
    for ref, first in ((kc_ref, 0), (vc_ref, N_KV)):
        for gk in range(N_KV):
            kv_scr[first + gk] = head(cols, first + gk)
            for slot in range(CMP_STRIDE):
                ref[0, gk, :, slot * HEAD_DIM:(slot + 1) * HEAD_DIM] = kv_scr[
                    first + gk, pl.ds(slot, n_unit, stride=CMP_STRIDE), :]
    c0 += 2 * KV_DIM
    lo, hi = slice(0, HEAD_DIM), slice(HEAD_DIM, 2 * HEAD_DIM)
    for branch in (SLC, WIN):
        cols = proj(c0, 2 * KV_DIM)
        for gk in range(N_KV):
            k_ref[0, gk, branch, :, lo] = _rope(head(cols, gk), cos2, sin2).astype(BF16)
            v_ref[0, gk, branch, :, lo] = head(cols, N_KV + gk).astype(BF16)
        c0 += 2 * KV_DIM
    cols = jax.nn.sigmoid(jnp.dot(h, wgate_ref[...], preferred_element_type=F32))
    for gk in range(N_KV):
        k_ref[0, gk, SLC, :, hi] = blk_ref[...]
        k_ref[0, gk, WIN, :, hi] = jnp.zeros(blk_ref.shape, BF16)
        v_ref[0, gk, SLC, :, hi] = jnp.ones(blk_ref.shape, BF16)
        v_ref[0, gk, WIN, :, hi] = jnp.ones(blk_ref.shape, BF16)
        gate_ref[0, gk] = head(cols, gk)


def _in_proj(x, g, w_in, batch, seq):
    t, d = x.shape
    tm = min(256, seq)
    n_s = seq // tm
    main = 2 * CONV_CH + Q_DIM + 6 * KV_DIM
    gate_w = w_in[:, main:].reshape(d, N_KV, GROUP * 3)
    gate_w = jnp.pad(gate_w, ((0, 0), (0, 0), (0, V7X_LANES - GROUP * 3))).reshape(d, N_KV * V7X_LANES)
    gate_w = gate_w.astype(BF16)
    w = w_in[:, :main].astype(BF16)

    inv = jnp.power(ROPE_THETA, -jnp.arange(0, HEAD_DIM, 2, dtype=F32) / HEAD_DIM)
    ang = jnp.arange(seq, dtype=F32)[:, None] * inv[None, :]
    cos2 = jnp.concatenate([jnp.cos(ang), jnp.cos(ang)], axis=1)
    sin2 = jnp.concatenate([-jnp.sin(ang), jnp.sin(ang)], axis=1)

    key_blk = np.arange(seq)[:, None] // SLC_BLOCK == np.arange(seq // SLC_BLOCK)[None, :]
    blk_mask = jnp.asarray(np.where(key_blk, -(2.0 ** 100), 0.0), BF16)

    def heads(n, dtype):
        return (jax.ShapeDtypeStruct((batch, n, seq, HEAD_DIM), dtype),
                pl.BlockSpec((1, n, tm, HEAD_DIM), lambda i: (i // n_s, 0, i % n_s, 0)))

    stacked = (jax.ShapeDtypeStruct((batch, N_KV, 2, seq, 2 * HEAD_DIM), BF16),
               pl.BlockSpec((1, N_KV, 2, tm, 2 * HEAD_DIM), lambda i: (i // n_s, 0, 0, i % n_s, 0)))
    unit = CMP_STRIDE * HEAD_DIM
    units = (jax.ShapeDtypeStruct((batch, N_KV, seq // CMP_STRIDE, unit), F32),
             pl.BlockSpec((1, N_KV, tm // CMP_STRIDE, unit), lambda i: (i // n_s, 0, i % n_s, 0)))
    outs = [(jax.ShapeDtypeStruct((t, CONV_CH), F32), pl.BlockSpec((tm, CONV_CH), lambda i: (i, 0))),
            heads(N_HEADS, BF16), heads(N_HEADS, BF16),
            units, units,
            stacked, stacked,
            heads(N_KV, F32)]
    table = pl.BlockSpec((tm, HEAD_DIM), lambda i: (i % n_s, 0))
    return pl.pallas_call(
        _in_proj_body,
        grid=(t // tm,),
        in_specs=[pl.BlockSpec((tm, d), lambda i: (i, 0)),
                  pl.BlockSpec((1, d), lambda i: (0, 0)),
                  _resident((d, main), lambda i: (0, 0)),
                  _resident(gate_w.shape, lambda i: (0, 0)),
                  table, table, table],
        out_specs=[o[1] for o in outs],
        out_shape=[o[0] for o in outs],
        scratch_shapes=[pltpu.VMEM((2 * N_KV, tm, HEAD_DIM), F32)],
        compiler_params=_params(("parallel",), 48),
        name="in_proj",
    )(x, g.reshape(1, d), w, gate_w, cos2, sin2, blk_mask)


CONV_ROWS = 32


def _conv_body(u_ref, halo_ref, dw_ref, db_ref, lg_ref, lb_ref, pw_ref, og_ref, o_ref, ext_scr, y_scr, *, ts):
    i = pl.program_id(1)
    ext_scr[0, 0:CONV_HALO, :] = jnp.where(i == 0, 0.0, halo_ref[0])
    ext_scr[0, CONV_HALO:CONV_HALO + ts, :] = u_ref[0]
    moved = CONV_HALO + ts - V7X_SUBLANES
    for s in range(1, V7X_SUBLANES):
        ext_scr[s, 0:moved, :] = ext_scr[0, s:s + moved, :]
    first = CONV_HALO - (CONV_K - 1)
    for c in range(ts // CONV_ROWS):
        r0 = c * CONV_ROWS
        acc = jnp.broadcast_to(db_ref[...], (CONV_ROWS, CONV_CH))
        for k in range(CONV_K):
            s, base = (first + k) % V7X_SUBLANES, (first + k) // V7X_SUBLANES * V7X_SUBLANES
            acc = acc + dw_ref[k:k + 1, :] * ext_scr[s, r0 + base:r0 + base + CONV_ROWS, :]
        mu = jnp.mean(acc, axis=-1, keepdims=True)
        xc = acc - mu
        var = jnp.mean(xc * xc, axis=-1, keepdims=True)
        y = xc * lax.rsqrt(var + EPS) * lg_ref[...] + lb_ref[...]
        y_scr[r0:r0 + CONV_ROWS, :] = (y * jax.nn.sigmoid(y)).astype(BF16)
    z = jnp.dot(y_scr[...], pw_ref[...], preferred_element_type=F32)
    o_ref[0] = _rms(z, og_ref[...]).astype(BF16)


def _conv(u, dw_w, dw_b, ln_g, ln_b, pw_w, out_g):
    b, s, c = u.shape
    ts = min(256, s)
    per = ts // CONV_HALO
    vec = pl.BlockSpec((1, c), lambda bi, i: (0, 0))
    return pl.pallas_call(
        functools.partial(_conv_body, ts=ts),
        grid=(b, s // ts),
        in_specs=[pl.BlockSpec((1, ts, c), lambda bi, i: (bi, i, 0)),
                  pl.BlockSpec((1, CONV_HALO, c), lambda bi, i: (bi, jnp.maximum(i * per - 1, 0), 0)),
                  pl.BlockSpec((CONV_K, c), lambda bi, i: (0, 0)),
                  vec, vec, vec,
                  pl.BlockSpec((c, c), lambda bi, i: (0, 0)),
                  vec],
        out_specs=pl.BlockSpec((1, ts, c), lambda bi, i: (bi, i, 0)),
        out_shape=jax.ShapeDtypeStruct((b, s, c), BF16),
        scratch_shapes=[pltpu.VMEM((V7X_SUBLANES, CONV_HALO + ts, c), F32), pltpu.VMEM((ts, c), BF16)],
        compiler_params=_params(("parallel", "parallel"), 32),
        name="conv",
    )(u, u, dw_w, dw_b.reshape(1, c), ln_g.reshape(1, c), ln_b.reshape(1, c), pw_w.astype(BF16),
      out_g.reshape(1, c))


def _compress_one(u_ref, pos_ref, w1_ref, w2_ref, o_ref):
    u = u_ref[0, 0]
    half = u.shape[1]
    nu = u.shape[0]
    top = jnp.dot((u + pos_ref[0:1, :]).astype(BF16), w1_ref[0:half, :], preferred_element_type=F32)
    bot = jnp.dot((u + pos_ref[1:2, :]).astype(BF16), w1_ref[half:2 * half, :], preferred_element_type=F32)
    hid = top + pltpu.roll(bot, nu - 1, 0)
    hid = hid * jax.nn.sigmoid(hid)
    o_ref[0, 0] = jnp.dot(hid.astype(BF16), w2_ref[...], preferred_element_type=F32).astype(BF16)


def _compress_body(uk_ref, uv_ref, pk_ref, pv_ref, kw1_ref, kw2_ref, vw1_ref, vw2_ref, ok_ref, ov_ref):
    _compress_one(uk_ref, pk_ref, kw1_ref, kw2_ref, ok_ref)
    _compress_one(uv_ref, pv_ref, vw1_ref, vw2_ref, ov_ref)


def _compress(kc, vc, pos_k, pos_v, kw1, kw2, vw1, vw2):
    b, g, nu, unit = kc.shape
    dh = unit // CMP_STRIDE
    pos = lambda p: p.reshape(CMP_BLOCK // CMP_STRIDE, unit)
    u_spec = pl.BlockSpec((1, 1, nu, unit), lambda bi, gi: (bi, gi, 0, 0))
    full = lambda shape: pl.BlockSpec(shape, lambda bi, gi: (0,) * len(shape))
    o_spec = pl.BlockSpec((1, 1, nu, dh), lambda bi, gi: (bi, gi, 0, 0))
    o_shape = jax.ShapeDtypeStruct((b, g, nu, dh), BF16)
    return pl.pallas_call(
        _compress_body,
        grid=(b, g),
        in_specs=[u_spec, u_spec, full((2, unit)), full((2, unit)),
                  full((CMP_BLOCK * dh, CMP_HIDDEN)), full((CMP_HIDDEN, dh)),
                  full((CMP_BLOCK * dh, CMP_HIDDEN)), full((CMP_HIDDEN, dh))],
        out_specs=[o_spec, o_spec],
        out_shape=[o_shape, o_shape],
        compiler_params=_params(("parallel", "parallel"), 40),
        name="compress",
    )(kc, vc, pos(pos_k), pos(pos_v),
      kw1.astype(BF16), kw2.astype(BF16), vw1.astype(BF16), vw2.astype(BF16))


def _cmp_select_prefix(q_ref, kc_ref, vc_ref, ct_ref, ocmp_ref, sel_ref, *, tq, nc, nb):
    for sub in range(tq // CMP_SUB):
        _cmp_select_sub(q_ref, kc_ref, vc_ref, ct_ref, ocmp_ref, sel_ref,
                        row0=sub * CMP_SUB, q0=pl.program_id(2) * tq + sub * CMP_SUB, tq=CMP_SUB, nc=nc, nb=nb)


def _cmp_select_sub(q_ref, kc_ref, vc_ref, ct_ref, ocmp_ref, sel_ref, *, row0, q0, tq, nc, nb):
    n_slc = ct_ref.shape[0]
    q = q_ref[0, :, row0:row0 + tq, :].reshape(GROUP * tq, HEAD_DIM)
    s = lax.dot_general(q, kc_ref[0, 0, 0:nc, :], _NT, preferred_element_type=F32).reshape(GROUP, tq, nc)
    t = q0 + lax.broadcasted_iota(jnp.int32, (tq, nc), 0)
    cmp_end = lax.broadcasted_iota(jnp.int32, (tq, nc), 1) * CMP_STRIDE + (CMP_BLOCK - 1)
    s = s + jnp.where(cmp_end <= t, 0.0, MASKED)[None]
    m = jnp.max(s, axis=-1, keepdims=True)
    m = jnp.where(m > 0.5 * MASKED, m, 0.0)
    e = jnp.exp2(s - m)
    p = e * (1.0 / jnp.maximum(jnp.sum(e, axis=-1, keepdims=True), 1e-30))
    o = jnp.dot(p.reshape(GROUP * tq, nc).astype(BF16), vc_ref[0, 0, 0:nc, :], preferred_element_type=F32)
    ocmp_ref[0, :, row0:row0 + tq, :] = o.reshape(GROUP, tq, HEAD_DIM)

    psum = p[0] + p[1] + p[2] + p[3]
    hi = psum.astype(BF16)
    r1 = psum - hi.astype(F32)
    mid = r1.astype(BF16)
    lo = (r1 - mid.astype(F32)).astype(BF16)
    ct = ct_ref[0:nb, 0:nc]
    imp = (lax.dot_general(ct, hi, _NT, preferred_element_type=F32)
           + lax.dot_general(ct, mid, _NT, preferred_element_type=F32)
           + lax.dot_general(ct, lo, _NT, preferred_element_type=F32))

    blk = lax.broadcasted_iota(jnp.int32, (nb, tq), 0)
    jt = (q0 + lax.broadcasted_iota(jnp.int32, (nb, tq), 1)) // SLC_BLOCK
    forced = (blk == 0) | (blk == jt) | (blk == jt - 1)
    candidate = (blk >= 1) & (blk < jt - 1)
    val = jnp.where(candidate, imp, -1.0)
    for _ in range(N_SELECT - N_FORCED):
        best = jnp.max(val, axis=0, keepdims=True)
        first = jnp.min(jnp.where(val == best, blk, n_slc), axis=0, keepdims=True)
        val = jnp.where((blk == first) & (best >= 0.0), -1.0, val)
    unsel = jnp.where(forced | (candidate & (val < 0.0)), 0.0, 1.0)
    if nb < n_slc:
        unsel = jnp.concatenate([unsel, jnp.ones((n_slc - nb, tq), F32)], axis=0)
    sel_ref[0, 0, row0:row0 + tq, :] = unsel.T.astype(BF16)


def _cmp_select_body(q_ref, kc_ref, vc_ref, ct_ref, ocmp_ref, sel_ref, *, tq):
    nu = kc_ref.shape[2]
    n_slc = ct_ref.shape[0]
    per_slc = nu // n_slc
    visible = (pl.program_id(2) * tq + tq - CMP_BLOCK) // CMP_STRIDE + 1
    n_prefix = nu // V7X_LANES
    need = jnp.clip((visible + V7X_LANES - 1) // V7X_LANES, 1, n_prefix)
    for v in range(1, n_prefix + 1):
        nc = v * V7X_LANES
        pl.when(need == v)(functools.partial(
            _cmp_select_prefix, q_ref, kc_ref, vc_ref, ct_ref, ocmp_ref, sel_ref, tq=tq, nc=nc, nb=nc // per_slc))


def _cmp_to_slc_t(nu, n_slc):
    per_slc = SLC_BLOCK // CMP_STRIDE
    c = np.arange(nu)[None, :]
    j = np.arange(n_slc)[:, None]
    m = np.zeros((n_slc, nu), np.float32)
    for unit in range(CMP_BLOCK // CMP_STRIDE):
        m += ((c + unit) // per_slc == j)
    m[:, nu - 1] = 0.0
    return jnp.asarray(m, BF16)


def _cmp_select(q_raw, k_cmp, v_cmp):
    b, _, s, dh = q_raw.shape
    nu = k_cmp.shape[2]
    n_slc = s // SLC_BLOCK
    tq = 4 * CMP_SUB
    for q_end in range(tq, s + 1, tq):
        prefix = -(-((q_end - CMP_BLOCK) // CMP_STRIDE + 1) // V7X_LANES) * V7X_LANES
        assert prefix * n_slc // nu >= (q_end - 1) // SLC_BLOCK + 1
    kv_spec = pl.BlockSpec((1, 1, nu, dh), lambda bi, gi, qi: (bi, gi, 0, 0))
    return pl.pallas_call(
        functools.partial(_cmp_select_body, tq=tq),
        grid=(b, N_KV, s // tq),
        in_specs=[pl.BlockSpec((1, GROUP, tq, dh), lambda bi, gi, qi: (bi, gi, qi, 0)),
                  kv_spec, kv_spec,
                  pl.BlockSpec((n_slc, nu), lambda bi, gi, qi: (0, 0))],
        out_specs=[pl.BlockSpec((1, GROUP, tq, dh), lambda bi, gi, qi: (bi, gi, qi, 0)),
                   pl.BlockSpec((1, 1, tq, n_slc), lambda bi, gi, qi: (bi, gi, qi, 0))],
        out_shape=[jax.ShapeDtypeStruct((b, N_HEADS, s, dh), F32),
                   jax.ShapeDtypeStruct((b, N_KV, s, n_slc), BF16)],
        compiler_params=_params(("parallel", "parallel", "parallel"), 32),
        name="cmp_select",
    )(q_raw, k_cmp, v_cmp, _cmp_to_slc_t(nu, n_slc))


LOWER, UPPER = 0, 1


def _attend_body(q_ref, k_ref, v_ref, tri_ref, unsel_ref, ocmp_ref, g_ref, o_ref,
                 qa_scr, s_scr, m_scr, acc_scr, *, tq, tk):
    q0 = pl.program_id(2) * tq
    rows = GROUP * tq

    unsel = unsel_ref[0, 0]
    for r in range(GROUP):
        qa_scr[r * tq:(r + 1) * tq, 0:HEAD_DIM] = q_ref[0, r]
        qa_scr[r * tq:(r + 1) * tq, HEAD_DIM:2 * HEAD_DIM] = unsel
    m_scr[...] = jnp.full(m_scr.shape, MASKED, F32)
    acc_scr[...] = jnp.zeros(acc_scr.shape, F32)

    last = q0 // tk
    n_win = jnp.minimum(last + 1, WINDOW // tk + 1)
    n_tiles = last + 1 + n_win

    def tile(i):
        branch = (i > last).astype(jnp.int32)
        kt = i - branch * n_win
        return branch, kt, pl.multiple_of(kt * tk, tk)

    def scores(i):
        branch, _, k0 = tile(i)
        s_scr[...] = lax.dot_general(qa_scr[...], k_ref[0, 0, branch, pl.ds(k0, tk), :], _NT,
                                     preferred_element_type=F32)

    def softmax_pv(i, boundary):
        branch, kt, k0 = tile(i)
        if boundary:
            kind = jnp.where(kt == last, LOWER, UPPER)
            s = (s_scr[...].reshape(GROUP, tq, tk) + tri_ref[kind][None]).reshape(rows, tk)
        else:
            s = s_scr[...]
        m_old = m_scr[branch]
        m_new = jnp.maximum(m_old, jnp.max(s, axis=-1, keepdims=True))
        p = jnp.exp2(s - m_new).astype(BF16)
        pv = jnp.dot(p, v_ref[0, 0, branch, pl.ds(k0, tk), :], preferred_element_type=F32)
        acc_scr[branch] = jnp.exp2(m_old - m_new) * acc_scr[branch] + pv
        m_scr[branch] = m_new

    scores(0)

    def interior_step(i, carry):
        softmax_pv(i, False)
        scores(i + 1)
        return carry

    def boundary_step(i, carry):
        softmax_pv(i, True)
        scores(i + 1)
        return carry

    lax.fori_loop(0, last, interior_step, 0)
    lax.fori_loop(last, n_tiles - 1, boundary_step, 0)
    softmax_pv(n_tiles - 1, True)

    def normalized(branch):
        acc = acc_scr[branch]
        return acc[:, 0:HEAD_DIM] * (1.0 / acc[:, HEAD_DIM:2 * HEAD_DIM])

    o_slc = normalized(SLC)
    o_win = normalized(WIN)
    gate = g_ref[0, 0]
    for r in range(GROUP):
        o_ref[0, :, r * HEAD_DIM:(r + 1) * HEAD_DIM] = (
            gate[:, 3 * r:3 * r + 1] * ocmp_ref[0, r]
            + gate[:, 3 * r + 1:3 * r + 2] * o_slc[r * tq:(r + 1) * tq]
            + gate[:, 3 * r + 2:3 * r + 3] * o_win[r * tq:(r + 1) * tq])


def _attend(q_rot, k_all, v_all, unsel, o_cmp, gates):
    b, _, s, dh = q_rot.shape
    n_slc = unsel.shape[3]
    assert n_slc == dh, "the unselected one-hot fills the second half of the augmented contraction"
    tq = tk = WINDOW
    row, col = np.arange(tq)[:, None], np.arange(tk)[None, :]
    tri = jnp.asarray(np.stack([np.where(col <= row, 0.0, MASKED),
                                np.where(col > row, 0.0, MASKED)]), F32)
    q_spec = pl.BlockSpec((1, GROUP, tq, dh), lambda bi, gi, qi: (bi, gi, qi, 0))
    kv_spec = _resident((1, 1, 2, s, 2 * dh), lambda bi, gi, qi: (bi, gi, 0, 0, 0))
    row_spec = lambda w: pl.BlockSpec((1, 1, tq, w), lambda bi, gi, qi: (bi, gi, qi, 0))
    rows = GROUP * tq
    return pl.pallas_call(
        functools.partial(_attend_body, tq=tq, tk=tk),
        grid=(b, N_KV, s // tq),
        in_specs=[q_spec, kv_spec, kv_spec, _resident(tri.shape, lambda bi, gi, qi: (0, 0, 0)),
                  row_spec(n_slc), q_spec, row_spec(V7X_LANES)],
        out_specs=pl.BlockSpec((1, tq, GROUP * dh), lambda bi, gi, qi: (bi, qi, gi)),
        out_shape=jax.ShapeDtypeStruct((b, s, Q_DIM), F32),
        scratch_shapes=[pltpu.VMEM((rows, 2 * dh), BF16),
                        pltpu.VMEM((rows, tk), F32),
                        pltpu.VMEM((2, rows, 1), F32),
                        pltpu.VMEM((2, rows, 2 * dh), F32)],
        compiler_params=_params(("parallel", "parallel", "arbitrary"), 48),
        name="attend",
    )(q_rot, k_all, v_all, tri, unsel, o_cmp, gates)


def _out_proj_body(cn_ref, a_ref, x_ref, gn_ref, wc_ref, wa_ref, o_ref):
    an = _rms(a_ref[...], gn_ref[...]).astype(BF16)
    y = (jnp.dot(cn_ref[...], wc_ref[...], preferred_element_type=F32)
         + jnp.dot(an, wa_ref[...], preferred_element_type=F32))
    o_ref[...] = x_ref[...] + y


def _out_proj(conv_n, attn, x, nsa_g, w_out):
    t, d = x.shape
    tm = min(512, t)
    wc = w_out[:CONV_CH].astype(BF16)
    wa = w_out[CONV_CH:].astype(BF16)
    return pl.pallas_call(
        _out_proj_body,
        grid=(t // tm,),
        in_specs=[pl.BlockSpec((tm, CONV_CH), lambda i: (i, 0)),
                  pl.BlockSpec((tm, Q_DIM), lambda i: (i, 0)),
                  pl.BlockSpec((tm, d), lambda i: (i, 0)),
                  pl.BlockSpec((1, Q_DIM), lambda i: (0, 0)),
                  _resident((CONV_CH, d), lambda i: (0, 0)),
                  _resident((Q_DIM, d), lambda i: (0, 0))],
        out_specs=pl.BlockSpec((tm, d), lambda i: (i, 0)),
        out_shape=jax.ShapeDtypeStruct((t, d), F32),
        compiler_params=_params(("parallel",), 40),
        name="out_proj",
    )(conv_n, attn, x, nsa_g.reshape(1, Q_DIM), wc, wa)


def kernel(x, ffn1_norm, ffn1_w_gate, ffn1_w_up, ffn1_w_down, mix_norm, w_in, cmp_pos_k, cmp_pos_v, cmp_k_w1, cmp_k_w2, cmp_v_w1, cmp_v_w2, conv_dw_w, conv_dw_b, conv_ln_g, conv_ln_b, conv_pw_w, out_norm_conv, out_norm_nsa, w_out, ffn2_norm, ffn2_w_gate, ffn2_w_up, ffn2_w_down, final_norm):
    b, s, d = x.shape
    assert s % (SLC_BLOCK * V7X_LANES) == 0, "selection blocks must fill whole 128-lane rows"
    depth = ffn1_norm.shape[0]
    y = x.reshape(b * s, d)
    for l in range(depth):
        y = _ffn(y, ffn1_norm[l], ffn1_w_gate[l], ffn1_w_up[l], ffn1_w_down[l])
        u, q_raw, q_rot, kc, vc, k_all, v_all, gates = _in_proj(y, mix_norm[l], w_in[l], b, s)
        conv_n = _conv(u.reshape(b, s, CONV_CH), conv_dw_w[l], conv_dw_b[l], conv_ln_g[l], conv_ln_b[l],
                       conv_pw_w[l], out_norm_conv[l])
        k_cmp, v_cmp = _compress(kc, vc, cmp_pos_k[l], cmp_pos_v[l],
                                 cmp_k_w1[l], cmp_k_w2[l], cmp_v_w1[l], cmp_v_w2[l])
        o_cmp, unsel = _cmp_select(q_raw, k_cmp, v_cmp)
        attn = _attend(q_rot, k_all, v_all, unsel, o_cmp, gates)
        y = _out_proj(conv_n.reshape(b * s, CONV_CH), attn.reshape(b * s, Q_DIM), y, out_norm_nsa[l], w_out[l])
        y = _ffn(y, ffn2_norm[l], ffn2_w_gate[l], ffn2_w_up[l], ffn2_w_down[l],
                 final_g=final_norm if l == depth - 1 else None)
    return y.reshape(b, s, d)
```

```python
import functools

import numpy as np
import jax
import jax.numpy as jnp
from jax import lax
from jax.experimental import pallas as pl
from jax.experimental.pallas import tpu as pltpu

F32 = jnp.float32
BF16 = jnp.bfloat16

V7X_LANES = 128
V7X_SUBLANES = 8

CONV_CH = 512
N_HEADS = 12
HEAD_DIM = 128
N_KV = 3
GROUP = N_HEADS // N_KV
CONV_K = 31
CMP_BLOCK = 32
CMP_STRIDE = 16
CMP_HIDDEN = 256
SLC_BLOCK = 64
N_SELECT = 16
N_FORCED = 3
WINDOW = 512
ROPE_THETA = 10000.0
EPS = 1e-6
MASKED = -1e30
LOG2_E = 1.4426950408889634
SLC, WIN = 0, 1

KV_DIM = N_KV * HEAD_DIM
Q_DIM = N_HEADS * HEAD_DIM
CMP_SUB = 128
CONV_HALO = 32

_NT = (((1,), (1,)), ((), ()))


def _rms(x, g):
    return x * lax.rsqrt(jnp.mean(x * x, axis=-1, keepdims=True) + EPS) * g


def _params(semantics, vmem_mib):
    return pltpu.CompilerParams(dimension_semantics=semantics, vmem_limit_bytes=vmem_mib * 2 ** 20)


def _resident(shape, index_map):
    return pl.BlockSpec(shape, index_map, pipeline_mode=pl.Buffered(1))


def _ffn_body(*refs, n_main, has_tail, final_norm):
    refs = list(refs)
    x_ref, g_ref, wg_ref, wu_ref, wd_ref = refs[:5]
    del refs[:5]
    if has_tail:
        tail_refs = refs[:3]
        del refs[:3]
    if final_norm:
        fg_ref = refs.pop(0)
    o_ref, h_scr = refs
    j = pl.program_id(1)

    @pl.when(j == 0)
    def _():
        x = x_ref[...]
        h_scr[...] = _rms(x, g_ref[...]).astype(BF16)
        o_ref[...] = x

    def hidden_slab(wg, wu, wd):
        h = h_scr[...]
        a = jnp.dot(h, wg[...], preferred_element_type=F32)
        b = jnp.dot(h, wu[...], preferred_element_type=F32)
        z = (a * jax.nn.sigmoid(a) * b).astype(BF16)
        o_ref[...] += 0.5 * jnp.dot(z, wd[...], preferred_element_type=F32)

    if has_tail:
        pl.when(j < n_main)(functools.partial(hidden_slab, wg_ref, wu_ref, wd_ref))
        pl.when(j == n_main)(functools.partial(hidden_slab, *tail_refs))
    else:
        hidden_slab(wg_ref, wu_ref, wd_ref)

    if final_norm:
        @pl.when(j == n_main + has_tail - 1)
        def _():
            o_ref[...] = _rms(o_ref[...], fg_ref[...])


def _ffn(x, g, w_gate, w_up, w_down, final_g=None):
    t, d = x.shape
    f = w_gate.shape[1]
    tm = min(512, t)
    tf = 512
    n_main, f_tail = divmod(f, tf)
    assert f_tail % V7X_LANES == 0 and n_main >= 1
    has_tail = int(f_tail > 0)
    f_main = n_main * tf
    wg, wu, wd = w_gate.astype(BF16), w_up.astype(BF16), w_down.astype(BF16)
    final_norm = final_g is not None
    row = pl.BlockSpec((tm, d), lambda i, j: (i, 0))
    vec = pl.BlockSpec((1, d), lambda i, j: (0, 0))
    slab = lambda i, j: (0, jnp.minimum(j, n_main - 1))
    in_specs = [row, vec,
                pl.BlockSpec((d, tf), slab),
                pl.BlockSpec((d, tf), slab),
                pl.BlockSpec((tf, d), lambda i, j: (jnp.minimum(j, n_main - 1), 0))]
    args = [x, g.reshape(1, d), wg, wu, wd]
    if has_tail:
        in_specs += [_resident((d, f_tail), lambda i, j: (0, 0)),
                     _resident((d, f_tail), lambda i, j: (0, 0)),
                     _resident((f_tail, d), lambda i, j: (0, 0))]
        args += [wg[:, f_main:], wu[:, f_main:], wd[f_main:]]
    if final_norm:
        in_specs.append(vec)
        args.append(final_g.reshape(1, d))
    return pl.pallas_call(
        functools.partial(_ffn_body, n_main=n_main, has_tail=has_tail, final_norm=final_norm),
        grid=(t // tm, n_main + has_tail),
        in_specs=in_specs,
        out_specs=row,
        out_shape=jax.ShapeDtypeStruct((t, d), F32),
        scratch_shapes=[pltpu.VMEM((tm, d), BF16)],
        compiler_params=_params(("parallel", "arbitrary"), 48),
        name="ffn_final" if final_norm else "ffn",
    )(*args)


def _rope(x, cos2, sin2):
    return x * cos2 + pltpu.roll(x, HEAD_DIM // 2, 1) * sin2


def _in_proj_body(x_ref, g_ref, w_ref, wgate_ref, cos_ref, sin_ref, blk_ref,
                  u_ref, qraw_ref, qrot_ref, kc_ref, vc_ref, k_ref, v_ref, gate_ref, kv_scr):
    h = _rms(x_ref[...], g_ref[...]).astype(BF16)
    cos2 = cos_ref[...]
    sin2 = sin_ref[...]
    scale = HEAD_DIM ** -0.5

    def proj(c0, width):
        return jnp.dot(h, w_ref[:, c0:c0 + width], preferred_element_type=F32)

    glu = proj(0, 2 * CONV_CH)
    u_ref[...] = glu[:, :CONV_CH] * jax.nn.sigmoid(glu[:, CONV_CH:])

    def head(cols, i):
        return cols[:, i * HEAD_DIM:(i + 1) * HEAD_DIM]

    c0 = 2 * CONV_CH
    for gk in range(N_KV):
        cols = proj(c0, GROUP * HEAD_DIM) * (scale * LOG2_E)
        for r in range(GROUP):
            qh = head(cols, r)
            qraw_ref[0, gk * GROUP + r] = qh.astype(BF16)
            qrot_ref[0, gk * GROUP + r] = _rope(qh, cos2, sin2).astype(BF16)
        c0 += GROUP * HEAD_DIM
    cols = proj(c0, 2 * KV_DIM)
    n_unit = kv_scr.shape[1] // CMP_STRIDE
    for ref, first in ((kc_ref, 0), (vc_ref, N_KV)):
        for gk in range(N_KV):
            kv_scr[first + gk] = head(cols, first + gk)
            for slot in range(CMP_STRIDE):
                ref[0, gk, :, slot * HEAD_DIM:(slot + 1) * HEAD_DIM] = kv_scr[
                    first + gk, pl.ds(slot, n_unit, stride=CMP_STRIDE), :]
    c0 += 2 * KV_DIM
    lo, hi = slice(0, HEAD_DIM), slice(HEAD_DIM, 2 * HEAD_DIM)
    for branch in (SLC, WIN):
        cols = proj(c0, 2 * KV_DIM)
        for gk in range(N_KV):
            k_ref[0, gk, branch, :, lo] = _rope(head(cols, gk), cos2, sin2).astype(BF16)
            v_ref[0, gk, branch, :, lo] = head(cols, N_KV + gk).astype(BF16)
        c0 += 2 * KV_DIM
    cols = jax.nn.sigmoid(jnp.dot(h, wgate_ref[...], preferred_element_type=F32))
    for gk in range(N_KV):
        k_ref[0, gk, SLC, :, hi] = blk_ref[...]
        k_ref[0, gk, WIN, :, hi] = jnp.zeros(blk_ref.shape, BF16)
        v_ref[0, gk, SLC, :, hi] = jnp.ones(blk_ref.shape, BF16)
        v_ref[0, gk, WIN, :, hi] = jnp.ones(blk_ref.shape, BF16)
        gate_ref[0, gk] = head(cols, gk)


def _split_w_in_body(w_ref, main_ref, gate_ref):
    main = main_ref.shape[1]
    n_gate = w_ref.shape[1] - main
    main_ref[...] = w_ref[:, 0:main].astype(BF16)
    gate_ref[...] = jnp.zeros(gate_ref.shape, F32)
    gate_ref[:, 0:n_gate] = w_ref[:, main:main + n_gate]


def _split_w_in(w_in, main):
    d, n_in = w_in.shape
    assert main % V7X_LANES == 0 and n_in - main <= V7X_LANES
    tr = 512 if d % 512 == 0 else d
    return pl.pallas_call(
        _split_w_in_body,
        grid=(d // tr,),
        in_specs=[pl.BlockSpec((tr, n_in), lambda i: (i, 0))],
        out_specs=[pl.BlockSpec((tr, main), lambda i: (i, 0)), pl.BlockSpec((tr, V7X_LANES), lambda i: (i, 0))],
        out_shape=[jax.ShapeDtypeStruct((d, main), BF16), jax.ShapeDtypeStruct((d, V7X_LANES), F32)],
        compiler_params=_params(("parallel",), 40),
        name="split_w_in",
    )(w_in)


def _in_proj(x, g, w_in, batch, seq):
    t, d = x.shape
    tm = min(256, seq)
    n_s = seq // tm
    main = 2 * CONV_CH + Q_DIM + 6 * KV_DIM
    w, gate_cols = _split_w_in(w_in, main)
    gate_w = gate_cols[:, :N_HEADS * 3].reshape(d, N_KV, GROUP * 3)
    gate_w = jnp.pad(gate_w, ((0, 0), (0, 0), (0, V7X_LANES - GROUP * 3))).reshape(d, N_KV * V7X_LANES)
    gate_w = gate_w.astype(BF16)

    inv = jnp.power(ROPE_THETA, -jnp.arange(0, HEAD_DIM, 2, dtype=F32) / HEAD_DIM)
    ang = jnp.arange(seq, dtype=F32)[:, None] * inv[None, :]
    cos2 = jnp.concatenate([jnp.cos(ang), jnp.cos(ang)], axis=1)
    sin2 = jnp.concatenate([-jnp.sin(ang), jnp.sin(ang)], axis=1)

    key_blk = np.arange(seq)[:, None] // SLC_BLOCK == np.arange(seq // SLC_BLOCK)[None, :]
    blk_mask = jnp.asarray(np.where(key_blk, -(2.0 ** 100), 0.0), BF16)

    def heads(n, dtype):
        return (jax.ShapeDtypeStruct((batch, n, seq, HEAD_DIM), dtype),
                pl.BlockSpec((1, n, tm, HEAD_DIM), lambda i: (i // n_s, 0, i % n_s, 0)))

    stacked = (jax.ShapeDtypeStruct((batch, N_KV, 2, seq, 2 * HEAD_DIM), BF16),
               pl.BlockSpec((1, N_KV, 2, tm, 2 * HEAD_DIM), lambda i: (i // n_s, 0, 0, i % n_s, 0)))
    unit = CMP_STRIDE * HEAD_DIM
    units = (jax.ShapeDtypeStruct((batch, N_KV, seq // CMP_STRIDE, unit), F32),
             pl.BlockSpec((1, N_KV, tm // CMP_STRIDE, unit), lambda i: (i // n_s, 0, i % n_s, 0)))
    outs = [(jax.ShapeDtypeStruct((t, CONV_CH), F32), pl.BlockSpec((tm, CONV_CH), lambda i: (i, 0))),
            heads(N_HEADS, BF16), heads(N_HEADS, BF16),
            units, units,
            stacked, stacked,
            heads(N_KV, F32)]
    table = pl.BlockSpec((tm, HEAD_DIM), lambda i: (i % n_s, 0))
    return pl.pallas_call(
        _in_proj_body,
        grid=(t // tm,),
        in_specs=[pl.BlockSpec((tm, d), lambda i: (i, 0)),
                  pl.BlockSpec((1, d), lambda i: (0, 0)),
                  _resident((d, main), lambda i: (0, 0)),
                  _resident(gate_w.shape, lambda i: (0, 0)),
                  table, table, table],
        out_specs=[o[1] for o in outs],
        out_shape=[o[0] for o in outs],
        scratch_shapes=[pltpu.VMEM((2 * N_KV, tm, HEAD_DIM), F32)],
        compiler_params=_params(("parallel",), 48),
        name="in_proj",
    )(x, g.reshape(1, d), w, gate_w, cos2, sin2, blk_mask)


CONV_ROWS = 32


def _conv_body(u_ref, halo_ref, dw_ref, db_ref, lg_ref, lb_ref, pw_ref, og_ref, o_ref, ext_scr, y_scr, *, ts):
    i = pl.program_id(1)
    ext_scr[0, 0:CONV_HALO, :] = jnp.where(i == 0, 0.0, halo_ref[0])
    ext_scr[0, CONV_HALO:CONV_HALO + ts, :] = u_ref[0]
    moved = CONV_HALO + ts - V7X_SUBLANES
    for s in range(1, V7X_SUBLANES):
        ext_scr[s, 0:moved, :] = ext_scr[0, s:s + moved, :]
    first = CONV_HALO - (CONV_K - 1)
    for c in range(ts // CONV_ROWS):
        r0 = c * CONV_ROWS
        acc = jnp.broadcast_to(db_ref[...], (CONV_ROWS, CONV_CH))
        for k in range(CONV_K):
            s, base = (first + k) % V7X_SUBLANES, (first + k) // V7X_SUBLANES * V7X_SUBLANES
            acc = acc + dw_ref[k:k + 1, :] * ext_scr[s, r0 + base:r0 + base + CONV_ROWS, :]
        mu = jnp.mean(acc, axis=-1, keepdims=True)
        xc = acc - mu
        var = jnp.mean(xc * xc, axis=-1, keepdims=True)
        y = xc * lax.rsqrt(var + EPS) * lg_ref[...] + lb_ref[...]
        y_scr[r0:r0 + CONV_ROWS, :] = (y * jax.nn.sigmoid(y)).astype(BF16)
    z = jnp.dot(y_scr[...], pw_ref[...], preferred_element_type=F32)
    o_ref[0] = _rms(z, og_ref[...]).astype(BF16)


def _conv(u, dw_w, dw_b, ln_g, ln_b, pw_w, out_g):
    b, s, c = u.shape
    ts = min(256, s)
    per = ts // CONV_HALO
    vec = pl.BlockSpec((1, c), lambda bi, i: (0, 0))
    return pl.pallas_call(
        functools.partial(_conv_body, ts=ts),
        grid=(b, s // ts),
        in_specs=[pl.BlockSpec((1, ts, c), lambda bi, i: (bi, i, 0)),
                  pl.BlockSpec((1, CONV_HALO, c), lambda bi, i: (bi, jnp.maximum(i * per - 1, 0), 0)),
                  pl.BlockSpec((CONV_K, c), lambda bi, i: (0, 0)),
                  vec, vec, vec,
                  pl.BlockSpec((c, c), lambda bi, i: (0, 0)),
                  vec],
        out_specs=pl.BlockSpec((1, ts, c), lambda bi, i: (bi, i, 0)),
        out_shape=jax.ShapeDtypeStruct((b, s, c), BF16),
        scratch_shapes=[pltpu.VMEM((V7X_SUBLANES, CONV_HALO + ts, c), F32), pltpu.VMEM((ts, c), BF16)],
        compiler_params=_params(("parallel", "parallel"), 32),
        name="conv",
    )(u, u, dw_w, dw_b.reshape(1, c), ln_g.reshape(1, c), ln_b.reshape(1, c), pw_w.astype(BF16),
      out_g.reshape(1, c))


def _compress_one(u_ref, pos_ref, w1_ref, w2_ref, o_ref):
    u = u_ref[0, 0]
    half = u.shape[1]
    nu = u.shape[0]
    top = jnp.dot((u + pos_ref[0:1, :]).astype(BF16), w1_ref[0:half, :], preferred_element_type=F32)
    bot = jnp.dot((u + pos_ref[1:2, :]).astype(BF16), w1_ref[half:2 * half, :], preferred_element_type=F32)
    hid = top + pltpu.roll(bot, nu - 1, 0)
    hid = hid * jax.nn.sigmoid(hid)
    o_ref[0, 0] = jnp.dot(hid.astype(BF16), w2_ref[...], preferred_element_type=F32).astype(BF16)


def _compress_body(uk_ref, uv_ref, pk_ref, pv_ref, kw1_ref, kw2_ref, vw1_ref, vw2_ref, ok_ref, ov_ref):
    _compress_one(uk_ref, pk_ref, kw1_ref, kw2_ref, ok_ref)
    _compress_one(uv_ref, pv_ref, vw1_ref, vw2_ref, ov_ref)


def _compress(kc, vc, pos_k, pos_v, kw1, kw2, vw1, vw2):
    b, g, nu, unit = kc.shape
    dh = unit // CMP_STRIDE
    pos = lambda p: p.reshape(CMP_BLOCK // CMP_STRIDE, unit)
    u_spec = pl.BlockSpec((1, 1, nu, unit), lambda bi, gi: (bi, gi, 0, 0))
    full = lambda shape: pl.BlockSpec(shape, lambda bi, gi: (0,) * len(shape))
    o_spec = pl.BlockSpec((1, 1, nu, dh), lambda bi, gi: (bi, gi, 0, 0))
    o_shape = jax.ShapeDtypeStruct((b, g, nu, dh), BF16)
    return pl.pallas_call(
        _compress_body,
        grid=(b, g),
        in_specs=[u_spec, u_spec, full((2, unit)), full((2, unit)),
                  full((CMP_BLOCK * dh, CMP_HIDDEN)), full((CMP_HIDDEN, dh)),
                  full((CMP_BLOCK * dh, CMP_HIDDEN)), full((CMP_HIDDEN, dh))],
        out_specs=[o_spec, o_spec],
        out_shape=[o_shape, o_shape],
        compiler_params=_params(("parallel", "parallel"), 40),
        name="compress",
    )(kc, vc, pos(pos_k), pos(pos_v),
      kw1.astype(BF16), kw2.astype(BF16), vw1.astype(BF16), vw2.astype(BF16))


def _cmp_select_prefix(q_ref, kc_ref, vc_ref, ct_ref, ocmp_ref, sel_ref, *, tq, nc, nb):
    for sub in range(tq // CMP_SUB):
        _cmp_select_sub(q_ref, kc_ref, vc_ref, ct_ref, ocmp_ref, sel_ref,
                        row0=sub * CMP_SUB, q0=pl.program_id(2) * tq + sub * CMP_SUB, tq=CMP_SUB, nc=nc, nb=nb)


def _cmp_select_sub(q_ref, kc_ref, vc_ref, ct_ref, ocmp_ref, sel_ref, *, row0, q0, tq, nc, nb):
    n_slc = ct_ref.shape[0]
    q = q_ref[0, :, row0:row0 + tq, :].reshape(GROUP * tq, HEAD_DIM)
    s = lax.dot_general(q, kc_ref[0, 0, 0:nc, :], _NT, preferred_element_type=F32).reshape(GROUP, tq, nc)
    t = q0 + lax.broadcasted_iota(jnp.int32, (tq, nc), 0)
    cmp_end = lax.broadcasted_iota(jnp.int32, (tq, nc), 1) * CMP_STRIDE + (CMP_BLOCK - 1)
    s = s + jnp.where(cmp_end <= t, 0.0, MASKED)[None]
    m = jnp.max(s, axis=-1, keepdims=True)
    m = jnp.where(m > 0.5 * MASKED, m, 0.0)
    e = jnp.exp2(s - m)
    p = e * (1.0 / jnp.maximum(jnp.sum(e, axis=-1, keepdims=True), 1e-30))
    o = jnp.dot(p.reshape(GROUP * tq, nc).astype(BF16), vc_ref[0, 0, 0:nc, :], preferred_element_type=F32)
    ocmp_ref[0, :, row0:row0 + tq, :] = o.reshape(GROUP, tq, HEAD_DIM)

    psum = p[0] + p[1] + p[2] + p[3]
    hi = psum.astype(BF16)
    r1 = psum - hi.astype(F32)
    mid = r1.astype(BF16)
    lo = (r1 - mid.astype(F32)).astype(BF16)
    ct = ct_ref[0:nb, 0:nc]
    imp = (lax.dot_general(ct, hi, _NT, preferred_element_type=F32)
           + lax.dot_general(ct, mid, _NT, preferred_element_type=F32)
           + lax.dot_general(ct, lo, _NT, preferred_element_type=F32))

    blk = lax.broadcasted_iota(jnp.int32, (nb, tq), 0)
    jt = (q0 + lax.broadcasted_iota(jnp.int32, (nb, tq), 1)) // SLC_BLOCK
    forced = (blk == 0) | (blk == jt) | (blk == jt - 1)
    candidate = (blk >= 1) & (blk < jt - 1)
    val = jnp.where(candidate, imp, -1.0)
    for _ in range(N_SELECT - N_FORCED):
        best = jnp.max(val, axis=0, keepdims=True)
        first = jnp.min(jnp.where(val == best, blk, n_slc), axis=0, keepdims=True)
        val = jnp.where((blk == first) & (best >= 0.0), -1.0, val)
    unsel = jnp.where(forced | (candidate & (val < 0.0)), 0.0, 1.0)
    if nb < n_slc:
        unsel = jnp.concatenate([unsel, jnp.ones((n_slc - nb, tq), F32)], axis=0)
    sel_ref[0, 0, row0:row0 + tq, :] = unsel.T.astype(BF16)


def _cmp_select_body(q_ref, kc_ref, vc_ref, ct_ref, ocmp_ref, sel_ref, *, tq):
    nu = kc_ref.shape[2]
    n_slc = ct_ref.shape[0]
    per_slc = nu // n_slc
    visible = (pl.program_id(2) * tq + tq - CMP_BLOCK) // CMP_STRIDE + 1
    n_prefix = nu // V7X_LANES
    need = jnp.clip((visible + V7X_LANES - 1) // V7X_LANES, 1, n_prefix)
    for v in range(1, n_prefix + 1):
        nc = v * V7X_LANES
        pl.when(need == v)(functools.partial(
            _cmp_select_prefix, q_ref, kc_ref, vc_ref, ct_ref, ocmp_ref, sel_ref, tq=tq, nc=nc, nb=nc // per_slc))


def _cmp_to_slc_t(nu, n_slc):
    per_slc = SLC_BLOCK // CMP_STRIDE
    c = np.arange(nu)[None, :]
    j = np.arange(n_slc)[:, None]
    m = np.zeros((n_slc, nu), np.float32)
    for unit in range(CMP_BLOCK // CMP_STRIDE):
        m += ((c + unit) // per_slc == j)
    m[:, nu - 1] = 0.0
    return jnp.asarray(m, BF16)


def _cmp_select(q_raw, k_cmp, v_cmp):
    b, _, s, dh = q_raw.shape
    nu = k_cmp.shape[2]
    n_slc = s // SLC_BLOCK
    tq = 4 * CMP_SUB
    for q_end in range(tq, s + 1, tq):
        prefix = -(-((q_end - CMP_BLOCK) // CMP_STRIDE + 1) // V7X_LANES) * V7X_LANES
        assert prefix * n_slc // nu >= (q_end - 1) // SLC_BLOCK + 1
    kv_spec = pl.BlockSpec((1, 1, nu, dh), lambda bi, gi, qi: (bi, gi, 0, 0))
    return pl.pallas_call(
        functools.partial(_cmp_select_body, tq=tq),
        grid=(b, N_KV, s // tq),
        in_specs=[pl.BlockSpec((1, GROUP, tq, dh), lambda bi, gi, qi: (bi, gi, qi, 0)),
                  kv_spec, kv_spec,
                  pl.BlockSpec((n_slc, nu), lambda bi, gi, qi: (0, 0))],
        out_specs=[pl.BlockSpec((1, GROUP, tq, dh), lambda bi, gi, qi: (bi, gi, qi, 0)),
                   pl.BlockSpec((1, 1, tq, n_slc), lambda bi, gi, qi: (bi, gi, qi, 0))],
        out_shape=[jax.ShapeDtypeStruct((b, N_HEADS, s, dh), F32),
                   jax.ShapeDtypeStruct((b, N_KV, s, n_slc), BF16)],
        compiler_params=_params(("parallel", "parallel", "parallel"), 32),
        name="cmp_select",
    )(q_raw, k_cmp, v_cmp, _cmp_to_slc_t(nu, n_slc))


LOWER, UPPER = 0, 1


def _attend_body(q_ref, k_ref, v_ref, tri_ref, unsel_ref, ocmp_ref, g_ref, o_ref,
                 qa_scr, s_scr, m_scr, acc_scr, *, tq, tk):
    q0 = pl.program_id(2) * tq
    rows = GROUP * tq

    unsel = unsel_ref[0, 0]
    for r in range(GROUP):
        qa_scr[r * tq:(r + 1) * tq, 0:HEAD_DIM] = q_ref[0, r]
        qa_scr[r * tq:(r + 1) * tq, HEAD_DIM:2 * HEAD_DIM] = unsel
    m_scr[...] = jnp.full(m_scr.shape, MASKED, F32)
    acc_scr[...] = jnp.zeros(acc_scr.shape, F32)

    last = q0 // tk
    n_win = jnp.minimum(last + 1, WINDOW // tk + 1)
    n_tiles = last + 1 + n_win

    def tile(i):
        branch = (i > last).astype(jnp.int32)
        kt = i - branch * n_win
        return branch, kt, pl.multiple_of(kt * tk, tk)

    def scores(i):
        branch, _, k0 = tile(i)
        s_scr[...] = lax.dot_general(qa_scr[...], k_ref[0, 0, branch, pl.ds(k0, tk), :], _NT,
                                     preferred_element_type=F32)

    def softmax_pv(i, boundary):
        branch, kt, k0 = tile(i)
        if boundary:
            kind = jnp.where(kt == last, LOWER, UPPER)
            s = (s_scr[...].reshape(GROUP, tq, tk) + tri_ref[kind][None]).reshape(rows, tk)
        else:
            s = s_scr[...]
        m_old = m_scr[branch]
        m_new = jnp.maximum(m_old, jnp.max(s, axis=-1, keepdims=True))
        p = jnp.exp2(s - m_new).astype(BF16)
        pv = jnp.dot(p, v_ref[0, 0, branch, pl.ds(k0, tk), :], preferred_element_type=F32)
        acc_scr[branch] = jnp.exp2(m_old - m_new) * acc_scr[branch] + pv
        m_scr[branch] = m_new

    scores(0)

    def interior_step(i, carry):
        softmax_pv(i, False)
        scores(i + 1)
        return carry

    def boundary_step(i, carry):
        softmax_pv(i, True)
        scores(i + 1)
        return carry

    lax.fori_loop(0, last, interior_step, 0)
    lax.fori_loop(last, n_tiles - 1, boundary_step, 0)
    softmax_pv(n_tiles - 1, True)

    def normalized(branch):
        acc = acc_scr[branch]
        return acc[:, 0:HEAD_DIM] * (1.0 / acc[:, HEAD_DIM:2 * HEAD_DIM])

    o_slc = normalized(SLC)
    o_win = normalized(WIN)
    gate = g_ref[0, 0]
    for r in range(GROUP):
        o_ref[0, :, r * HEAD_DIM:(r + 1) * HEAD_DIM] = (
            gate[:, 3 * r:3 * r + 1] * ocmp_ref[0, r]
            + gate[:, 3 * r + 1:3 * r + 2] * o_slc[r * tq:(r + 1) * tq]
            + gate[:, 3 * r + 2:3 * r + 3] * o_win[r * tq:(r + 1) * tq])


def _attend(q_rot, k_all, v_all, unsel, o_cmp, gates):
    b, _, s, dh = q_rot.shape
    n_slc = unsel.shape[3]
    assert n_slc == dh, "the unselected one-hot fills the second half of the augmented contraction"
    tq = tk = WINDOW
    row, col = np.arange(tq)[:, None], np.arange(tk)[None, :]
    tri = jnp.asarray(np.stack([np.where(col <= row, 0.0, MASKED),
                                np.where(col > row, 0.0, MASKED)]), F32)
    q_spec = pl.BlockSpec((1, GROUP, tq, dh), lambda bi, gi, qi: (bi, gi, qi, 0))
    kv_spec = _resident((1, 1, 2, s, 2 * dh), lambda bi, gi, qi: (bi, gi, 0, 0, 0))
    row_spec = lambda w: pl.BlockSpec((1, 1, tq, w), lambda bi, gi, qi: (bi, gi, qi, 0))
    rows = GROUP * tq
    return pl.pallas_call(
        functools.partial(_attend_body, tq=tq, tk=tk),
        grid=(b, N_KV, s // tq),
        in_specs=[q_spec, kv_spec, kv_spec, _resident(tri.shape, lambda bi, gi, qi: (0, 0, 0)),
                  row_spec(n_slc), q_spec, row_spec(V7X_LANES)],
        out_specs=pl.BlockSpec((1, tq, GROUP * dh), lambda bi, gi, qi: (bi, qi, gi)),
        out_shape=jax.ShapeDtypeStruct((b, s, Q_DIM), F32),
        scratch_shapes=[pltpu.VMEM((rows, 2 * dh), BF16),
                        pltpu.VMEM((rows, tk), F32),
                        pltpu.VMEM((2, rows, 1), F32),
                        pltpu.VMEM((2, rows, 2 * dh), F32)],
        compiler_params=_params(("parallel", "parallel", "arbitrary"), 48),
        name="attend",
    )(q_rot, k_all, v_all, tri, unsel, o_cmp, gates)


def _out_proj_body(cn_ref, a_ref, x_ref, gn_ref, wc_ref, wa_ref, o_ref):
    an = _rms(a_ref[...], gn_ref[...]).astype(BF16)
    y = (jnp.dot(cn_ref[...], wc_ref[...], preferred_element_type=F32)
         + jnp.dot(an, wa_ref[...], preferred_element_type=F32))
    o_ref[...] = x_ref[...] + y


def _out_proj(conv_n, attn, x, nsa_g, w_out):
    t, d = x.shape
    tm = min(512, t)
    wc = w_out[:CONV_CH].astype(BF16)
    wa = w_out[CONV_CH:].astype(BF16)
    return pl.pallas_call(
        _out_proj_body,
        grid=(t // tm,),
        in_specs=[pl.BlockSpec((tm, CONV_CH), lambda i: (i, 0)),
                  pl.BlockSpec((tm, Q_DIM), lambda i: (i, 0)),
                  pl.BlockSpec((tm, d), lambda i: (i, 0)),
                  pl.BlockSpec((1, Q_DIM), lambda i: (0, 0)),
                  _resident((CONV_CH, d), lambda i: (0, 0)),
                  _resident((Q_DIM, d), lambda i: (0, 0))],
        out_specs=pl.BlockSpec((tm, d), lambda i: (i, 0)),
        out_shape=jax.ShapeDtypeStruct((t, d), F32),
        compiler_params=_params(("parallel",), 40),
        name="out_proj",
    )(conv_n, attn, x, nsa_g.reshape(1, Q_DIM), wc, wa)


def kernel(x, ffn1_norm, ffn1_w_gate, ffn1_w_up, ffn1_w_down, mix_norm, w_in, cmp_pos_k, cmp_pos_v, cmp_k_w1, cmp_k_w2, cmp_v_w1, cmp_v_w2, conv_dw_w, conv_dw_b, conv_ln_g, conv_ln_b, conv_pw_w, out_norm_conv, out_norm_nsa, w_out, ffn2_norm, ffn2_w_gate, ffn2_w_up, ffn2_w_down, final_norm):
    b, s, d = x.shape
    assert s % (SLC_BLOCK * V7X_LANES) == 0, "selection blocks must fill whole 128-lane rows"
    depth = ffn1_norm.shape[0]
    y = x.reshape(b * s, d)
    for l in range(depth):
        y = _ffn(y, ffn1_norm[l], ffn1_w_gate[l], ffn1_w_up[l], ffn1_w_down[l])
        u, q_raw, q_rot, kc, vc, k_all, v_all, gates = _in_proj(y, mix_norm[l], w_in[l], b, s)
        conv_n = _conv(u.reshape(b, s, CONV_CH), conv_dw_w[l], conv_dw_b[l], conv_ln_g[l], conv_ln_b[l],
                       conv_pw_w[l], out_norm_conv[l])
        k_cmp, v_cmp = _compress(kc, vc, cmp_pos_k[l], cmp_pos_v[l],
                                 cmp_k_w1[l], cmp_k_w2[l], cmp_v_w1[l], cmp_v_w2[l])
        o_cmp, unsel = _cmp_select(q_raw, k_cmp, v_cmp)
        attn = _attend(q_rot, k_all, v_all, unsel, o_cmp, gates)
        y = _out_proj(conv_n.reshape(b * s, CONV_CH), attn.reshape(b * s, Q_DIM), y, out_norm_nsa[l], w_out[l])
        y = _ffn(y, ffn2_norm[l], ffn2_w_gate[l], ffn2_w_up[l], ffn2_w_down[l],
                 final_g=final_norm if l == depth - 1 else None)
    return y.reshape(b, s, d)
```

```python
import functools

import numpy as np
import jax
import jax.numpy as jnp
from jax import lax
from jax.experimental import pallas as pl
from jax.experimental.pallas import tpu as pltpu

F32 = jnp.float32
BF16 = jnp.bfloat16

V7X_LANES = 128
V7X_SUBLANES = 8
V7X_VMEM_BYTES = 64 * 2 ** 20

CONV_CH = 512
N_HEADS = 12
HEAD_DIM = 128
N_KV = 3
GROUP = N_HEADS // N_KV
CONV_K = 31
CMP_BLOCK = 32
CMP_STRIDE = 16
CMP_HIDDEN = 256
SLC_BLOCK = 64
N_SELECT = 16
N_FORCED = 3
WINDOW = 512
ROPE_THETA = 10000.0
EPS = 1e-6
MASKED = -1e30
LOG2_E = 1.4426950408889634
SLC, WIN = 0, 1

KV_DIM = N_KV * HEAD_DIM
Q_DIM = N_HEADS * HEAD_DIM
CMP_SUB = 128
CONV_HALO = 32

_NT = (((1,), (1,)), ((), ()))


def _rms(x, g):
    return x * lax.rsqrt(jnp.mean(x * x, axis=-1, keepdims=True) + EPS) * g


def _params(semantics, vmem_mib):
    return pltpu.CompilerParams(dimension_semantics=semantics, vmem_limit_bytes=vmem_mib * 2 ** 20)


def _resident(shape, index_map):
    return pl.BlockSpec(shape, index_map, pipeline_mode=pl.Buffered(1))


def _ffn_body(*refs, n_main, has_tail, final_norm):
    refs = list(refs)
    x_ref, g_ref, wg_ref, wu_ref, wd_ref = refs[:5]
    del refs[:5]
    if has_tail:
        tail_refs = refs[:3]
        del refs[:3]
    if final_norm:
        fg_ref = refs.pop(0)
    o_ref, h_scr = refs
    j = pl.program_id(1)

    @pl.when(j == 0)
    def _():
        x = x_ref[...]
        h_scr[...] = _rms(x, g_ref[...]).astype(BF16)
        o_ref[...] = x

    def hidden_slab(wg, wu, wd):
        h = h_scr[...]
        a = jnp.dot(h, wg[...], preferred_element_type=F32)
        b = jnp.dot(h, wu[...], preferred_element_type=F32)
        z = (a * jax.nn.sigmoid(a) * b).astype(BF16)
        o_ref[...] += 0.5 * jnp.dot(z, wd[...], preferred_element_type=F32)

    if has_tail:
        pl.when(j < n_main)(functools.partial(hidden_slab, wg_ref, wu_ref, wd_ref))
        pl.when(j == n_main)(functools.partial(hidden_slab, *tail_refs))
    else:
        hidden_slab(wg_ref, wu_ref, wd_ref)

    if final_norm:
        @pl.when(j == n_main + has_tail - 1)
        def _():
            o_ref[...] = _rms(o_ref[...], fg_ref[...])


def _ffn(x, g, w_gate, w_up, w_down, final_g=None):
    t, d = x.shape
    f = w_gate.shape[1]
    tm = min(512, t)
    tf = 512
    n_main, f_tail = divmod(f, tf)
    assert f_tail % V7X_LANES == 0 and n_main >= 1
    has_tail = int(f_tail > 0)
    f_main = n_main * tf
    wg, wu, wd = w_gate.astype(BF16), w_up.astype(BF16), w_down.astype(BF16)
    final_norm = final_g is not None
    row = pl.BlockSpec((tm, d), lambda i, j: (i, 0))
    vec = pl.BlockSpec((1, d), lambda i, j: (0, 0))
    slab = lambda i, j: (0, jnp.minimum(j, n_main - 1))
    in_specs = [row, vec,
                pl.BlockSpec((d, tf), slab),
                pl.BlockSpec((d, tf), slab),
                pl.BlockSpec((tf, d), lambda i, j: (jnp.minimum(j, n_main - 1), 0))]
    args = [x, g.reshape(1, d), wg, wu, wd]
    if has_tail:
        in_specs += [_resident((d, f_tail), lambda i, j: (0, 0)),
                     _resident((d, f_tail), lambda i, j: (0, 0)),
                     _resident((f_tail, d), lambda i, j: (0, 0))]
        args += [wg[:, f_main:], wu[:, f_main:], wd[f_main:]]
    if final_norm:
        in_specs.append(vec)
        args.append(final_g.reshape(1, d))
    return pl.pallas_call(
        functools.partial(_ffn_body, n_main=n_main, has_tail=has_tail, final_norm=final_norm),
        grid=(t // tm, n_main + has_tail),
        in_specs=in_specs,
        out_specs=row,
        out_shape=jax.ShapeDtypeStruct((t, d), F32),
        scratch_shapes=[pltpu.VMEM((tm, d), BF16)],
        compiler_params=_params(("parallel", "arbitrary"), 48),
        name="ffn_final" if final_norm else "ffn",
    )(*args)


def _rope(x, cos2, sin2):
    return x * cos2 + pltpu.roll(x, HEAD_DIM // 2, 1) * sin2


def _in_proj_body(x_ref, g_ref, w_ref, wgate_ref, cos_ref, sin_ref, blk_ref,
                  u_ref, qraw_ref, qrot_ref, kc_ref, vc_ref, k_ref, v_ref, gate_ref, kv_scr):
    h = _rms(x_ref[...], g_ref[...]).astype(BF16)
    cos2 = cos_ref[...]
    sin2 = sin_ref[...]
    scale = HEAD_DIM ** -0.5

    def proj(c0, width):
        return jnp.dot(h, w_ref[:, c0:c0 + width], preferred_element_type=F32)

    glu = proj(0, 2 * CONV_CH)
    u_ref[...] = glu[:, :CONV_CH] * jax.nn.sigmoid(glu[:, CONV_CH:])

    def head(cols, i):
        return cols[:, i * HEAD_DIM:(i + 1) * HEAD_DIM]

    c0 = 2 * CONV_CH
    for gk in range(N_KV):
        cols = proj(c0, GROUP * HEAD_DIM) * (scale * LOG2_E)
        for r in range(GROUP):
            qh = head(cols, r)
            qraw_ref[0, gk * GROUP + r] = qh.astype(BF16)
            qrot_ref[0, gk * GROUP + r] = _rope(qh, cos2, sin2).astype(BF16)
        c0 += GROUP * HEAD_DIM
    cols = proj(c0, 2 * KV_DIM)
    n_unit = kv_scr.shape[1] // CMP_STRIDE
    for ref, first in ((kc_ref, 0), (vc_ref, N_KV)):
        for gk in range(N_KV):
            kv_scr[first + gk] = head(cols, first + gk)
            for slot in range(CMP_STRIDE):
                ref[0, gk, :, slot * HEAD_DIM:(slot + 1) * HEAD_DIM] = kv_scr[
                    first + gk, pl.ds(slot, n_unit, stride=CMP_STRIDE), :]
    c0 += 2 * KV_DIM
    lo, hi = slice(0, HEAD_DIM), slice(HEAD_DIM, 2 * HEAD_DIM)
    for branch in (SLC, WIN):
        cols = proj(c0, 2 * KV_DIM)
        for gk in range(N_KV):
            k_ref[0, gk, branch, :, lo] = _rope(head(cols, gk), cos2, sin2).astype(BF16)
            v_ref[0, gk, branch, :, lo] = head(cols, N_KV + gk).astype(BF16)
        c0 += 2 * KV_DIM
    cols = jax.nn.sigmoid(jnp.dot(h, wgate_ref[...], preferred_element_type=F32))
    for gk in range(N_KV):
        k_ref[0, gk, SLC, :, hi] = blk_ref[...]
        k_ref[0, gk, WIN, :, hi] = jnp.zeros(blk_ref.shape, BF16)
        v_ref[0, gk, SLC, :, hi] = jnp.ones(blk_ref.shape, BF16)
        v_ref[0, gk, WIN, :, hi] = jnp.ones(blk_ref.shape, BF16)
        gate_ref[0, gk] = head(cols, gk)


def _in_proj(x, g, w_in, batch, seq):
    t, d = x.shape
    tm = min(256, seq)
    n_s = seq // tm
    main = 2 * CONV_CH + Q_DIM + 6 * KV_DIM
    gate_w = w_in[:, main:].reshape(d, N_KV, GROUP * 3)
    gate_w = jnp.pad(gate_w, ((0, 0), (0, 0), (0, V7X_LANES - GROUP * 3))).reshape(d, N_KV * V7X_LANES)
    gate_w = gate_w.astype(BF16)
    w = w_in[:, :main].astype(BF16)

    inv = jnp.power(ROPE_THETA, -jnp.arange(0, HEAD_DIM, 2, dtype=F32) / HEAD_DIM)
    ang = jnp.arange(seq, dtype=F32)[:, None] * inv[None, :]
    cos2 = jnp.concatenate([jnp.cos(ang), jnp.cos(ang)], axis=1)
    sin2 = jnp.concatenate([-jnp.sin(ang), jnp.sin(ang)], axis=1)

    key_blk = np.arange(seq)[:, None] // SLC_BLOCK == np.arange(seq // SLC_BLOCK)[None, :]
    blk_mask = jnp.asarray(np.where(key_blk, -(2.0 ** 100), 0.0), BF16)

    def heads(n, dtype):
        return (jax.ShapeDtypeStruct((batch, n, seq, HEAD_DIM), dtype),
                pl.BlockSpec((1, n, tm, HEAD_DIM), lambda i: (i // n_s, 0, i % n_s, 0)))

    stacked = (jax.ShapeDtypeStruct((batch, N_KV, 2, seq, 2 * HEAD_DIM), BF16),
               pl.BlockSpec((1, N_KV, 2, tm, 2 * HEAD_DIM), lambda i: (i // n_s, 0, 0, i % n_s, 0)))
    unit = CMP_STRIDE * HEAD_DIM
    units = (jax.ShapeDtypeStruct((batch, N_KV, seq // CMP_STRIDE, unit), F32),
             pl.BlockSpec((1, N_KV, tm // CMP_STRIDE, unit), lambda i: (i // n_s, 0, i % n_s, 0)))
    outs = [(jax.ShapeDtypeStruct((t, CONV_CH), F32), pl.BlockSpec((tm, CONV_CH), lambda i: (i, 0))),
            heads(N_HEADS, BF16), heads(N_HEADS, BF16),
            units, units,
            stacked, stacked,
            heads(N_KV, F32)]
    table = pl.BlockSpec((tm, HEAD_DIM), lambda i: (i % n_s, 0))
    return pl.pallas_call(
        _in_proj_body,
        grid=(t // tm,),
        in_specs=[pl.BlockSpec((tm, d), lambda i: (i, 0)),
                  pl.BlockSpec((1, d), lambda i: (0, 0)),
                  _resident((d, main), lambda i: (0, 0)),
                  _resident(gate_w.shape, lambda i: (0, 0)),
                  table, table, table],
        out_specs=[o[1] for o in outs],
        out_shape=[o[0] for o in outs],
        scratch_shapes=[pltpu.VMEM((2 * N_KV, tm, HEAD_DIM), F32)],
        compiler_params=_params(("parallel",), 48),
        name="in_proj",
    )(x, g.reshape(1, d), w, gate_w, cos2, sin2, blk_mask)


CONV_ROWS = 32


def _conv_body(u_ref, halo_ref, dw_ref, db_ref, lg_ref, lb_ref, pw_ref, og_ref, o_ref, ext_scr, y_scr, *, ts):
    i = pl.program_id(1)
    ext_scr[0, 0:CONV_HALO, :] = jnp.where(i == 0, 0.0, halo_ref[0])
    ext_scr[0, CONV_HALO:CONV_HALO + ts, :] = u_ref[0]
    moved = CONV_HALO + ts - V7X_SUBLANES
    for s in range(1, V7X_SUBLANES):
        ext_scr[s, 0:moved, :] = ext_scr[0, s:s + moved, :]
    first = CONV_HALO - (CONV_K - 1)
    for c in range(ts // CONV_ROWS):
        r0 = c * CONV_ROWS
        acc = jnp.broadcast_to(db_ref[...], (CONV_ROWS, CONV_CH))
        for k in range(CONV_K):
            s, base = (first + k) % V7X_SUBLANES, (first + k) // V7X_SUBLANES * V7X_SUBLANES
            acc = acc + dw_ref[k:k + 1, :] * ext_scr[s, r0 + base:r0 + base + CONV_ROWS, :]
        mu = jnp.mean(acc, axis=-1, keepdims=True)
        xc = acc - mu
        var = jnp.mean(xc * xc, axis=-1, keepdims=True)
        y = xc * lax.rsqrt(var + EPS) * lg_ref[...] + lb_ref[...]
        y_scr[r0:r0 + CONV_ROWS, :] = (y * jax.nn.sigmoid(y)).astype(BF16)
    z = jnp.dot(y_scr[...], pw_ref[...], preferred_element_type=F32)
    o_ref[0] = _rms(z, og_ref[...]).astype(BF16)


def _conv(u, dw_w, dw_b, ln_g, ln_b, pw_w, out_g):
    b, s, c = u.shape
    ts = min(256, s)
    per = ts // CONV_HALO
    vec = pl.BlockSpec((1, c), lambda bi, i: (0, 0))
    return pl.pallas_call(
        functools.partial(_conv_body, ts=ts),
        grid=(b, s // ts),
        in_specs=[pl.BlockSpec((1, ts, c), lambda bi, i: (bi, i, 0)),
                  pl.BlockSpec((1, CONV_HALO, c), lambda bi, i: (bi, jnp.maximum(i * per - 1, 0), 0)),
                  pl.BlockSpec((CONV_K, c), lambda bi, i: (0, 0)),
                  vec, vec, vec,
                  pl.BlockSpec((c, c), lambda bi, i: (0, 0)),
                  vec],
        out_specs=pl.BlockSpec((1, ts, c), lambda bi, i: (bi, i, 0)),
        out_shape=jax.ShapeDtypeStruct((b, s, c), BF16),
        scratch_shapes=[pltpu.VMEM((V7X_SUBLANES, CONV_HALO + ts, c), F32), pltpu.VMEM((ts, c), BF16)],
        compiler_params=_params(("parallel", "parallel"), 32),
        name="conv",
    )(u, u, dw_w, dw_b.reshape(1, c), ln_g.reshape(1, c), ln_b.reshape(1, c), pw_w.astype(BF16),
      out_g.reshape(1, c))


def _compress_one(u_ref, pos_ref, w1_ref, w2_ref, o_ref):
    u = u_ref[0, 0]
    half = u.shape[1]
    nu = u.shape[0]
    top = jnp.dot((u + pos_ref[0:1, :]).astype(BF16), w1_ref[0:half, :], preferred_element_type=F32)
    bot = jnp.dot((u + pos_ref[1:2, :]).astype(BF16), w1_ref[half:2 * half, :], preferred_element_type=F32)
    hid = top + pltpu.roll(bot, nu - 1, 0)
    hid = hid * jax.nn.sigmoid(hid)
    o_ref[0, 0] = jnp.dot(hid.astype(BF16), w2_ref[...], preferred_element_type=F32).astype(BF16)


def _compress_body(uk_ref, uv_ref, pk_ref, pv_ref, kw1_ref, kw2_ref, vw1_ref, vw2_ref, ok_ref, ov_ref):
    _compress_one(uk_ref, pk_ref, kw1_ref, kw2_ref, ok_ref)
    _compress_one(uv_ref, pv_ref, vw1_ref, vw2_ref, ov_ref)


def _compress(kc, vc, pos_k, pos_v, kw1, kw2, vw1, vw2):
    b, g, nu, unit = kc.shape
    dh = unit // CMP_STRIDE
    pos = lambda p: p.reshape(CMP_BLOCK // CMP_STRIDE, unit)
    u_spec = pl.BlockSpec((1, 1, nu, unit), lambda bi, gi: (bi, gi, 0, 0))
    full = lambda shape: pl.BlockSpec(shape, lambda bi, gi: (0,) * len(shape))
    o_spec = pl.BlockSpec((1, 1, nu, dh), lambda bi, gi: (bi, gi, 0, 0))
    o_shape = jax.ShapeDtypeStruct((b, g, nu, dh), BF16)
    return pl.pallas_call(
        _compress_body,
        grid=(b, g),
        in_specs=[u_spec, u_spec, full((2, unit)), full((2, unit)),
                  full((CMP_BLOCK * dh, CMP_HIDDEN)), full((CMP_HIDDEN, dh)),
                  full((CMP_BLOCK * dh, CMP_HIDDEN)), full((CMP_HIDDEN, dh))],
        out_specs=[o_spec, o_spec],
        out_shape=[o_shape, o_shape],
        compiler_params=_params(("parallel", "parallel"), 40),
        name="compress",
    )(kc, vc, pos(pos_k), pos(pos_v),
      kw1.astype(BF16), kw2.astype(BF16), vw1.astype(BF16), vw2.astype(BF16))


def _cmp_select_prefix(q_ref, kc_ref, vc_ref, ct_ref, ocmp_ref, sel_ref, *, tq, nc, nb):
    for sub in range(tq // CMP_SUB):
        _cmp_select_sub(q_ref, kc_ref, vc_ref, ct_ref, ocmp_ref, sel_ref,
                        row0=sub * CMP_SUB, q0=pl.program_id(2) * tq + sub * CMP_SUB, tq=CMP_SUB, nc=nc, nb=nb)


def _cmp_select_sub(q_ref, kc_ref, vc_ref, ct_ref, ocmp_ref, sel_ref, *, row0, q0, tq, nc, nb):
    n_slc = ct_ref.shape[0]
    q = q_ref[0, :, row0:row0 + tq, :].reshape(GROUP * tq, HEAD_DIM)
    s = lax.dot_general(q, kc_ref[0, 0, 0:nc, :], _NT, preferred_element_type=F32).reshape(GROUP, tq, nc)
    t = q0 + lax.broadcasted_iota(jnp.int32, (tq, nc), 0)
    cmp_end = lax.broadcasted_iota(jnp.int32, (tq, nc), 1) * CMP_STRIDE + (CMP_BLOCK - 1)
    s = s + jnp.where(cmp_end <= t, 0.0, MASKED)[None]
    m = jnp.max(s, axis=-1, keepdims=True)
    m = jnp.where(m > 0.5 * MASKED, m, 0.0)
    e = jnp.exp2(s - m)
    p = e * (1.0 / jnp.maximum(jnp.sum(e, axis=-1, keepdims=True), 1e-30))
    o = jnp.dot(p.reshape(GROUP * tq, nc).astype(BF16), vc_ref[0, 0, 0:nc, :], preferred_element_type=F32)
    ocmp_ref[0, :, row0:row0 + tq, :] = o.reshape(GROUP, tq, HEAD_DIM)

    psum = p[0] + p[1] + p[2] + p[3]
    hi = psum.astype(BF16)
    r1 = psum - hi.astype(F32)
    mid = r1.astype(BF16)
    lo = (r1 - mid.astype(F32)).astype(BF16)
    ct = ct_ref[0:nb, 0:nc]
    imp = (lax.dot_general(ct, hi, _NT, preferred_element_type=F32)
           + lax.dot_general(ct, mid, _NT, preferred_element_type=F32)
           + lax.dot_general(ct, lo, _NT, preferred_element_type=F32))

    blk = lax.broadcasted_iota(jnp.int32, (nb, tq), 0)
    jt = (q0 + lax.broadcasted_iota(jnp.int32, (nb, tq), 1)) // SLC_BLOCK
    forced = (blk == 0) | (blk == jt) | (blk == jt - 1)
    candidate = (blk >= 1) & (blk < jt - 1)
    val = jnp.where(candidate, imp, -1.0)
    for _ in range(N_SELECT - N_FORCED):
        best = jnp.max(val, axis=0, keepdims=True)
        first = jnp.min(jnp.where(val == best, blk, n_slc), axis=0, keepdims=True)
        val = jnp.where((blk == first) & (best >= 0.0), -1.0, val)
    unsel = jnp.where(forced | (candidate & (val < 0.0)), 0.0, 1.0)
    if nb < n_slc:
        unsel = jnp.concatenate([unsel, jnp.ones((n_slc - nb, tq), F32)], axis=0)
    sel_ref[0, 0, row0:row0 + tq, :] = unsel.T.astype(BF16)


def _cmp_select_body(q_ref, kc_ref, vc_ref, ct_ref, ocmp_ref, sel_ref, *, tq):
    nu = kc_ref.shape[2]
    n_slc = ct_ref.shape[0]
    per_slc = nu // n_slc
    visible = (pl.program_id(2) * tq + tq - CMP_BLOCK) // CMP_STRIDE + 1
    n_prefix = nu // V7X_LANES
    need = jnp.clip((visible + V7X_LANES - 1) // V7X_LANES, 1, n_prefix)
    for v in range(1, n_prefix + 1):
        nc = v * V7X_LANES
        pl.when(need == v)(functools.partial(
            _cmp_select_prefix, q_ref, kc_ref, vc_ref, ct_ref, ocmp_ref, sel_ref, tq=tq, nc=nc, nb=nc // per_slc))


def _cmp_to_slc_t(nu, n_slc):
    per_slc = SLC_BLOCK // CMP_STRIDE
    c = np.arange(nu)[None, :]
    j = np.arange(n_slc)[:, None]
    m = np.zeros((n_slc, nu), np.float32)
    for unit in range(CMP_BLOCK // CMP_STRIDE):
        m += ((c + unit) // per_slc == j)
    m[:, nu - 1] = 0.0
    return jnp.asarray(m, BF16)


def _cmp_select(q_raw, k_cmp, v_cmp):
    b, _, s, dh = q_raw.shape
    nu = k_cmp.shape[2]
    n_slc = s // SLC_BLOCK
    tq = 4 * CMP_SUB
    for q_end in range(tq, s + 1, tq):
        prefix = -(-((q_end - CMP_BLOCK) // CMP_STRIDE + 1) // V7X_LANES) * V7X_LANES
        assert prefix * n_slc // nu >= (q_end - 1) // SLC_BLOCK + 1
    kv_spec = pl.BlockSpec((1, 1, nu, dh), lambda bi, gi, qi: (bi, gi, 0, 0))
    return pl.pallas_call(
        functools.partial(_cmp_select_body, tq=tq),
        grid=(b, N_KV, s // tq),
        in_specs=[pl.BlockSpec((1, GROUP, tq, dh), lambda bi, gi, qi: (bi, gi, qi, 0)),
                  kv_spec, kv_spec,
                  pl.BlockSpec((n_slc, nu), lambda bi, gi, qi: (0, 0))],
        out_specs=[pl.BlockSpec((1, GROUP, tq, dh), lambda bi, gi, qi: (bi, gi, qi, 0)),
                   pl.BlockSpec((1, 1, tq, n_slc), lambda bi, gi, qi: (bi, gi, qi, 0))],
        out_shape=[jax.ShapeDtypeStruct((b, N_HEADS, s, dh), F32),
                   jax.ShapeDtypeStruct((b, N_KV, s, n_slc), BF16)],
        compiler_params=_params(("parallel", "parallel", "parallel"), 32),
        name="cmp_select",
    )(q_raw, k_cmp, v_cmp, _cmp_to_slc_t(nu, n_slc))


LOWER, UPPER = 0, 1


def _attend_body(q_ref, k_ref, v_ref, tri_ref, unsel_ref, ocmp_ref, g_ref, o_ref,
                 qa_scr, s_scr, m_scr, acc_scr, *, tq, tk):
    q0 = pl.program_id(2) * tq
    rows = GROUP * tq

    unsel = unsel_ref[0, 0]
    for r in range(GROUP):
        qa_scr[r * tq:(r + 1) * tq, 0:HEAD_DIM] = q_ref[0, r]
        qa_scr[r * tq:(r + 1) * tq, HEAD_DIM:2 * HEAD_DIM] = unsel
    m_scr[...] = jnp.full(m_scr.shape, MASKED, F32)
    acc_scr[...] = jnp.zeros(acc_scr.shape, F32)

    last = q0 // tk
    n_win = jnp.minimum(last + 1, WINDOW // tk + 1)
    n_tiles = last + 1 + n_win

    def tile(i):
        branch = (i > last).astype(jnp.int32)
        kt = i - branch * n_win
        return branch, kt, pl.multiple_of(kt * tk, tk)

    def scores(i):
        branch, _, k0 = tile(i)
        s_scr[...] = lax.dot_general(qa_scr[...], k_ref[0, 0, branch, pl.ds(k0, tk), :], _NT,
                                     preferred_element_type=F32)

    def softmax_pv(i, boundary):
        branch, kt, k0 = tile(i)
        if boundary:
            kind = jnp.where(kt == last, LOWER, UPPER)
            s = (s_scr[...].reshape(GROUP, tq, tk) + tri_ref[kind][None]).reshape(rows, tk)
        else:
            s = s_scr[...]
        m_old = m_scr[branch]
        m_new = jnp.maximum(m_old, jnp.max(s, axis=-1, keepdims=True))
        p = jnp.exp2(s - jnp.concatenate([m_new] * (tk // V7X_LANES), axis=1)).astype(BF16)
        pv = jnp.dot(p, v_ref[0, 0, branch, pl.ds(k0, tk), :], preferred_element_type=F32)
        alpha = jnp.exp2(m_old - m_new)
        acc_scr[branch] = jnp.concatenate([alpha] * (2 * HEAD_DIM // V7X_LANES), axis=1) * acc_scr[branch] + pv
        m_scr[branch] = m_new

    scores(0)

    def interior_step(i, carry):
        softmax_pv(i, False)
        scores(i + 1)
        return carry

    def boundary_step(i, carry):
        softmax_pv(i, True)
        scores(i + 1)
        return carry

    lax.fori_loop(0, last, interior_step, 0)
    lax.fori_loop(last, n_tiles - 1, boundary_step, 0)
    softmax_pv(n_tiles - 1, True)

    def normalized(branch):
        acc = acc_scr[branch]
        return acc[:, 0:HEAD_DIM] * (1.0 / acc[:, HEAD_DIM:2 * HEAD_DIM])

    o_slc = normalized(SLC)
    o_win = normalized(WIN)
    gate = g_ref[0, 0]
    for r in range(GROUP):
        o_ref[0, :, r * HEAD_DIM:(r + 1) * HEAD_DIM] = (
            gate[:, 3 * r:3 * r + 1] * ocmp_ref[0, r]
            + gate[:, 3 * r + 1:3 * r + 2] * o_slc[r * tq:(r + 1) * tq]
            + gate[:, 3 * r + 2:3 * r + 3] * o_win[r * tq:(r + 1) * tq])


def _attend(q_rot, k_all, v_all, unsel, o_cmp, gates):
    b, _, s, dh = q_rot.shape
    n_slc = unsel.shape[3]
    assert n_slc == dh, "the unselected one-hot fills the second half of the augmented contraction"
    tq = tk = WINDOW
    row, col = np.arange(tq)[:, None], np.arange(tk)[None, :]
    tri = jnp.asarray(np.stack([np.where(col <= row, 0.0, MASKED),
                                np.where(col > row, 0.0, MASKED)]), F32)
    q_spec = pl.BlockSpec((1, GROUP, tq, dh), lambda bi, gi, qi: (bi, gi, qi, 0))
    kv_spec = _resident((1, 1, 2, s, 2 * dh), lambda bi, gi, qi: (bi, gi, 0, 0, 0))
    row_spec = lambda w: pl.BlockSpec((1, 1, tq, w), lambda bi, gi, qi: (bi, gi, qi, 0))
    rows = GROUP * tq
    return pl.pallas_call(
        functools.partial(_attend_body, tq=tq, tk=tk),
        grid=(b, N_KV, s // tq),
        in_specs=[q_spec, kv_spec, kv_spec, _resident(tri.shape, lambda bi, gi, qi: (0, 0, 0)),
                  row_spec(n_slc), q_spec, row_spec(V7X_LANES)],
        out_specs=pl.BlockSpec((1, tq, GROUP * dh), lambda bi, gi, qi: (bi, qi, gi)),
        out_shape=jax.ShapeDtypeStruct((b, s, Q_DIM), F32),
        scratch_shapes=[pltpu.VMEM((rows, 2 * dh), BF16),
                        pltpu.VMEM((rows, tk), F32),
                        pltpu.VMEM((2, rows, V7X_LANES), F32),
                        pltpu.VMEM((2, rows, 2 * dh), F32)],
        compiler_params=_params(("parallel", "parallel", "arbitrary"), 48),
        name="attend",
    )(q_rot, k_all, v_all, tri, unsel, o_cmp, gates)


def _out_proj_body(cn_ref, a_ref, x_ref, gn_ref, wc_ref, wa_ref, o_ref):
    an = _rms(a_ref[...], gn_ref[...]).astype(BF16)
    y = (jnp.dot(cn_ref[...], wc_ref[...], preferred_element_type=F32)
         + jnp.dot(an, wa_ref[...], preferred_element_type=F32))
    o_ref[...] = x_ref[...] + y


def _out_proj(conv_n, attn, x, nsa_g, w_out):
    t, d = x.shape
    tm = min(512, t)
    wc = w_out[:CONV_CH].astype(BF16)
    wa = w_out[CONV_CH:].astype(BF16)
    return pl.pallas_call(
        _out_proj_body,
        grid=(t // tm,),
        in_specs=[pl.BlockSpec((tm, CONV_CH), lambda i: (i, 0)),
                  pl.BlockSpec((tm, Q_DIM), lambda i: (i, 0)),
                  pl.BlockSpec((tm, d), lambda i: (i, 0)),
                  pl.BlockSpec((1, Q_DIM), lambda i: (0, 0)),
                  _resident((CONV_CH, d), lambda i: (0, 0)),
                  _resident((Q_DIM, d), lambda i: (0, 0))],
        out_specs=pl.BlockSpec((tm, d), lambda i: (i, 0)),
        out_shape=jax.ShapeDtypeStruct((t, d), F32),
        compiler_params=_params(("parallel",), 40),
        name="out_proj",
    )(conv_n, attn, x, nsa_g.reshape(1, Q_DIM), wc, wa)


def kernel(x, ffn1_norm, ffn1_w_gate, ffn1_w_up, ffn1_w_down, mix_norm, w_in, cmp_pos_k, cmp_pos_v, cmp_k_w1, cmp_k_w2, cmp_v_w1, cmp_v_w2, conv_dw_w, conv_dw_b, conv_ln_g, conv_ln_b, conv_pw_w, out_norm_conv, out_norm_nsa, w_out, ffn2_norm, ffn2_w_gate, ffn2_w_up, ffn2_w_down, final_norm):
    b, s, d = x.shape
    assert s % (SLC_BLOCK * V7X_LANES) == 0, "selection blocks must fill whole 128-lane rows"
    depth = ffn1_norm.shape[0]
    y = x.reshape(b * s, d)
    for l in range(depth):
        y = _ffn(y, ffn1_norm[l], ffn1_w_gate[l], ffn1_w_up[l], ffn1_w_down[l])
        u, q_raw, q_rot, kc, vc, k_all, v_all, gates = _in_proj(y, mix_norm[l], w_in[l], b, s)
        conv_n = _conv(u.reshape(b, s, CONV_CH), conv_dw_w[l], conv_dw_b[l], conv_ln_g[l], conv_ln_b[l],
                       conv_pw_w[l], out_norm_conv[l])
        k_cmp, v_cmp = _compress(kc, vc, cmp_pos_k[l], cmp_pos_v[l],
                                 cmp_k_w1[l], cmp_k_w2[l], cmp_v_w1[l], cmp_v_w2[l])
        o_cmp, unsel = _cmp_select(q_raw, k_cmp, v_cmp)
        attn = _attend(q_rot, k_all, v_all, unsel, o_cmp, gates)
        y = _out_proj(conv_n.reshape(b * s, CONV_CH), attn.reshape(b * s, Q_DIM), y, out_norm_nsa[l], w_out[l])
        y = _ffn(y, ffn2_norm[l], ffn2_w_gate[l], ffn2_w_up[l], ffn2_w_down[l],
                 final_g=final_norm if l == depth - 1 else None)
    return y.reshape(b, s, d)
```

```python
import functools

import numpy as np
import jax
import jax.numpy as jnp
from jax import lax
from jax.experimental import pallas as pl
from jax.experimental.pallas import tpu as pltpu

F32 = jnp.float32
BF16 = jnp.bfloat16

V7X_LANES = 128
V7X_SUBLANES = 8
V7X_VMEM_BYTES = 64 * 2 ** 20

CONV_CH = 512
N_HEADS = 12
HEAD_DIM = 128
N_KV = 3
GROUP = N_HEADS // N_KV
CONV_K = 31
CMP_BLOCK = 32
CMP_STRIDE = 16
CMP_HIDDEN = 256
SLC_BLOCK = 64
N_SELECT = 16
N_FORCED = 3
WINDOW = 512
ROPE_THETA = 10000.0
EPS = 1e-6
MASKED = -1e30
LOG2_E = 1.4426950408889634
SLC, WIN = 0, 1

KV_DIM = N_KV * HEAD_DIM
Q_DIM = N_HEADS * HEAD_DIM
CMP_SUB = 128
CONV_HALO = 32

_NT = (((1,), (1,)), ((), ()))


def _rms(x, g):
    return x * lax.rsqrt(jnp.mean(x * x, axis=-1, keepdims=True) + EPS) * g


def _params(semantics, vmem_mib):
    return pltpu.CompilerParams(dimension_semantics=semantics, vmem_limit_bytes=vmem_mib * 2 ** 20)


def _resident(shape, index_map):
    return pl.BlockSpec(shape, index_map, pipeline_mode=pl.Buffered(1))


def _ffn_body(*refs, n_main, has_tail, final_norm):
    refs = list(refs)
    x_ref, g_ref, wg_ref, wu_ref, wd_ref = refs[:5]
    del refs[:5]
    if has_tail:
        tail_refs = refs[:3]
        del refs[:3]
    if final_norm:
        fg_ref = refs.pop(0)
    o_ref, h_scr = refs
    j = pl.program_id(1)

    @pl.when(j == 0)
    def _():
        x = x_ref[...]
        h_scr[...] = _rms(x, g_ref[...]).astype(BF16)
        o_ref[...] = x

    def hidden_slab(wg, wu, wd):
        h = h_scr[...]
        a = jnp.dot(h, wg[...], preferred_element_type=F32)
        b = jnp.dot(h, wu[...], preferred_element_type=F32)
        z = (a * jax.nn.sigmoid(a) * b).astype(BF16)
        o_ref[...] += 0.5 * jnp.dot(z, wd[...], preferred_element_type=F32)

    if has_tail:
        pl.when(j < n_main)(functools.partial(hidden_slab, wg_ref, wu_ref, wd_ref))
        pl.when(j == n_main)(functools.partial(hidden_slab, *tail_refs))
    else:
        hidden_slab(wg_ref, wu_ref, wd_ref)

    if final_norm:
        @pl.when(j == n_main + has_tail - 1)
        def _():
            o_ref[...] = _rms(o_ref[...], fg_ref[...])


def _ffn(x, g, w_gate, w_up, w_down, final_g=None):
    t, d = x.shape
    f = w_gate.shape[1]
    tm = min(512, t)
    tf = 512
    n_main, f_tail = divmod(f, tf)
    assert f_tail % V7X_LANES == 0 and n_main >= 1
    has_tail = int(f_tail > 0)
    f_main = n_main * tf
    wg, wu, wd = w_gate.astype(BF16), w_up.astype(BF16), w_down.astype(BF16)
    final_norm = final_g is not None
    row = pl.BlockSpec((tm, d), lambda i, j: (i, 0))
    vec = pl.BlockSpec((1, d), lambda i, j: (0, 0))
    slab = lambda i, j: (0, jnp.minimum(j, n_main - 1))
    in_specs = [row, vec,
                pl.BlockSpec((d, tf), slab),
                pl.BlockSpec((d, tf), slab),
                pl.BlockSpec((tf, d), lambda i, j: (jnp.minimum(j, n_main - 1), 0))]
    args = [x, g.reshape(1, d), wg, wu, wd]
    if has_tail:
        in_specs += [_resident((d, f_tail), lambda i, j: (0, 0)),
                     _resident((d, f_tail), lambda i, j: (0, 0)),
                     _resident((f_tail, d), lambda i, j: (0, 0))]
        args += [wg[:, f_main:], wu[:, f_main:], wd[f_main:]]
    if final_norm:
        in_specs.append(vec)
        args.append(final_g.reshape(1, d))
    return pl.pallas_call(
        functools.partial(_ffn_body, n_main=n_main, has_tail=has_tail, final_norm=final_norm),
        grid=(t // tm, n_main + has_tail),
        in_specs=in_specs,
        out_specs=row,
        out_shape=jax.ShapeDtypeStruct((t, d), F32),
        scratch_shapes=[pltpu.VMEM((tm, d), BF16)],
        compiler_params=_params(("parallel", "arbitrary"), 48),
        name="ffn_final" if final_norm else "ffn",
    )(*args)


def _rope(x, cos2, sin2):
    return x * cos2 + pltpu.roll(x, HEAD_DIM // 2, 1) * sin2


def _in_proj_body(x_ref, g_ref, w_ref, wgate_ref, cos_ref, sin_ref, blk_ref,
                  u_ref, qraw_ref, qrot_ref, kc_ref, vc_ref, k_ref, v_ref, gate_ref, kv_scr):
    h = _rms(x_ref[...], g_ref[...]).astype(BF16)
    cos2 = cos_ref[...]
    sin2 = sin_ref[...]
    scale = HEAD_DIM ** -0.5

    def proj(c0, width):
        return jnp.dot(h, w_ref[:, c0:c0 + width], preferred_element_type=F32)

    glu = proj(0, 2 * CONV_CH)
    u_ref[...] = glu[:, :CONV_CH] * jax.nn.sigmoid(glu[:, CONV_CH:])

    def head(cols, i):
        return cols[:, i * HEAD_DIM:(i + 1) * HEAD_DIM]

    c0 = 2 * CONV_CH
    for gk in range(N_KV):
        cols = proj(c0, GROUP * HEAD_DIM) * (scale * LOG2_E)
        for r in range(GROUP):
            qh = head(cols, r)
            qraw_ref[0, gk * GROUP + r] = qh.astype(BF16)
            qrot_ref[0, gk * GROUP + r] = _rope(qh, cos2, sin2).astype(BF16)
        c0 += GROUP * HEAD_DIM
    cols = proj(c0, 2 * KV_DIM)
    n_unit = kv_scr.shape[1] // CMP_STRIDE
    for ref, first in ((kc_ref, 0), (vc_ref, N_KV)):
        for gk in range(N_KV):
            kv_scr[first + gk] = head(cols, first + gk)
            for slot in range(CMP_STRIDE):
                ref[0, gk, :, slot * HEAD_DIM:(slot + 1) * HEAD_DIM] = kv_scr[
                    first + gk, pl.ds(slot, n_unit, stride=CMP_STRIDE), :]
    c0 += 2 * KV_DIM
    lo, hi = slice(0, HEAD_DIM), slice(HEAD_DIM, 2 * HEAD_DIM)
    for branch in (SLC, WIN):
        cols = proj(c0, 2 * KV_DIM)
        for gk in range(N_KV):
            k_ref[0, gk, branch, :, lo] = _rope(head(cols, gk), cos2, sin2).astype(BF16)
            v_ref[0, gk, branch, :, lo] = head(cols, N_KV + gk).astype(BF16)
        c0 += 2 * KV_DIM
    cols = jax.nn.sigmoid(jnp.dot(h, wgate_ref[...], preferred_element_type=F32))
    for gk in range(N_KV):
        k_ref[0, gk, SLC, :, hi] = blk_ref[...]
        k_ref[0, gk, WIN, :, hi] = jnp.zeros(blk_ref.shape, BF16)
        v_ref[0, gk, SLC, :, hi] = jnp.ones(blk_ref.shape, BF16)
        v_ref[0, gk, WIN, :, hi] = jnp.ones(blk_ref.shape, BF16)
        gate_ref[0, gk] = head(cols, gk)


def _in_proj(x, g, w_in, batch, seq):
    t, d = x.shape
    tm = min(256, seq)
    n_s = seq // tm
    main = 2 * CONV_CH + Q_DIM + 6 * KV_DIM
    gate_w = w_in[:, main:].reshape(d, N_KV, GROUP * 3)
    gate_w = jnp.pad(gate_w, ((0, 0), (0, 0), (0, V7X_LANES - GROUP * 3))).reshape(d, N_KV * V7X_LANES)
    gate_w = gate_w.astype(BF16)
    w = w_in[:, :main].astype(BF16)

    inv = jnp.power(ROPE_THETA, -jnp.arange(0, HEAD_DIM, 2, dtype=F32) / HEAD_DIM)
    ang = jnp.arange(seq, dtype=F32)[:, None] * inv[None, :]
    cos2 = jnp.concatenate([jnp.cos(ang), jnp.cos(ang)], axis=1)
    sin2 = jnp.concatenate([-jnp.sin(ang), jnp.sin(ang)], axis=1)

    key_blk = np.arange(seq)[:, None] // SLC_BLOCK == np.arange(seq // SLC_BLOCK)[None, :]
    blk_mask = jnp.asarray(np.where(key_blk, -(2.0 ** 100), 0.0), BF16)

    def heads(n, dtype):
        return (jax.ShapeDtypeStruct((batch, n, seq, HEAD_DIM), dtype),
                pl.BlockSpec((1, n, tm, HEAD_DIM), lambda i: (i // n_s, 0, i % n_s, 0)))

    stacked = (jax.ShapeDtypeStruct((batch, N_KV, 2, seq, 2 * HEAD_DIM), BF16),
               pl.BlockSpec((1, N_KV, 2, tm, 2 * HEAD_DIM), lambda i: (i // n_s, 0, 0, i % n_s, 0)))
    unit = CMP_STRIDE * HEAD_DIM
    units = (jax.ShapeDtypeStruct((batch, N_KV, seq // CMP_STRIDE, unit), F32),
             pl.BlockSpec((1, N_KV, tm // CMP_STRIDE, unit), lambda i: (i // n_s, 0, i % n_s, 0)))
    outs = [(jax.ShapeDtypeStruct((t, CONV_CH), F32), pl.BlockSpec((tm, CONV_CH), lambda i: (i, 0))),
            heads(N_HEADS, BF16), heads(N_HEADS, BF16),
            units, units,
            stacked, stacked,
            heads(N_KV, F32)]
    table = pl.BlockSpec((tm, HEAD_DIM), lambda i: (i % n_s, 0))
    return pl.pallas_call(
        _in_proj_body,
        grid=(t // tm,),
        in_specs=[pl.BlockSpec((tm, d), lambda i: (i, 0)),
                  pl.BlockSpec((1, d), lambda i: (0, 0)),
                  _resident((d, main), lambda i: (0, 0)),
                  _resident(gate_w.shape, lambda i: (0, 0)),
                  table, table, table],
        out_specs=[o[1] for o in outs],
        out_shape=[o[0] for o in outs],
        scratch_shapes=[pltpu.VMEM((2 * N_KV, tm, HEAD_DIM), F32)],
        compiler_params=_params(("parallel",), 48),
        name="in_proj",
    )(x, g.reshape(1, d), w, gate_w, cos2, sin2, blk_mask)


CONV_ROWS = 32


def _conv_body(u_ref, halo_ref, dw_ref, db_ref, lg_ref, lb_ref, pw_ref, og_ref, o_ref, ext_scr, y_scr, *, ts):
    i = pl.program_id(1)
    ext_scr[0, 0:CONV_HALO, :] = jnp.where(i == 0, 0.0, halo_ref[0])
    ext_scr[0, CONV_HALO:CONV_HALO + ts, :] = u_ref[0]
    moved = CONV_HALO + ts - V7X_SUBLANES
    for s in range(1, V7X_SUBLANES):
        ext_scr[s, 0:moved, :] = ext_scr[0, s:s + moved, :]
    first = CONV_HALO - (CONV_K - 1)
    for c in range(ts // CONV_ROWS):
        r0 = c * CONV_ROWS
        acc = jnp.broadcast_to(db_ref[...], (CONV_ROWS, CONV_CH))
        for k in range(CONV_K):
            s, base = (first + k) % V7X_SUBLANES, (first + k) // V7X_SUBLANES * V7X_SUBLANES
            acc = acc + dw_ref[k:k + 1, :] * ext_scr[s, r0 + base:r0 + base + CONV_ROWS, :]
        mu = jnp.mean(acc, axis=-1, keepdims=True)
        xc = acc - mu
        var = jnp.mean(xc * xc, axis=-1, keepdims=True)
        y = xc * lax.rsqrt(var + EPS) * lg_ref[...] + lb_ref[...]
        y_scr[r0:r0 + CONV_ROWS, :] = (y * jax.nn.sigmoid(y)).astype(BF16)
    z = jnp.dot(y_scr[...], pw_ref[...], preferred_element_type=F32)
    o_ref[0] = _rms(z, og_ref[...]).astype(BF16)


def _conv(u, dw_w, dw_b, ln_g, ln_b, pw_w, out_g):
    b, s, c = u.shape
    ts = min(256, s)
    per = ts // CONV_HALO
    vec = pl.BlockSpec((1, c), lambda bi, i: (0, 0))
    return pl.pallas_call(
        functools.partial(_conv_body, ts=ts),
        grid=(b, s // ts),
        in_specs=[pl.BlockSpec((1, ts, c), lambda bi, i: (bi, i, 0)),
                  pl.BlockSpec((1, CONV_HALO, c), lambda bi, i: (bi, jnp.maximum(i * per - 1, 0), 0)),
                  pl.BlockSpec((CONV_K, c), lambda bi, i: (0, 0)),
                  vec, vec, vec,
                  pl.BlockSpec((c, c), lambda bi, i: (0, 0)),
                  vec],
        out_specs=pl.BlockSpec((1, ts, c), lambda bi, i: (bi, i, 0)),
        out_shape=jax.ShapeDtypeStruct((b, s, c), BF16),
        scratch_shapes=[pltpu.VMEM((V7X_SUBLANES, CONV_HALO + ts, c), F32), pltpu.VMEM((ts, c), BF16)],
        compiler_params=_params(("parallel", "parallel"), 32),
        name="conv",
    )(u, u, dw_w, dw_b.reshape(1, c), ln_g.reshape(1, c), ln_b.reshape(1, c), pw_w.astype(BF16),
      out_g.reshape(1, c))


def _compress_one(u_ref, pos_ref, w1_ref, w2_ref, o_ref):
    u = u_ref[0, 0]
    half = u.shape[1]
    nu = u.shape[0]
    top = jnp.dot((u + pos_ref[0:1, :]).astype(BF16), w1_ref[0:half, :], preferred_element_type=F32)
    bot = jnp.dot((u + pos_ref[1:2, :]).astype(BF16), w1_ref[half:2 * half, :], preferred_element_type=F32)
    hid = top + pltpu.roll(bot, nu - 1, 0)
    hid = hid * jax.nn.sigmoid(hid)
    o_ref[0, 0] = jnp.dot(hid.astype(BF16), w2_ref[...], preferred_element_type=F32).astype(BF16)


def _compress_body(uk_ref, uv_ref, pk_ref, pv_ref, kw1_ref, kw2_ref, vw1_ref, vw2_ref, ok_ref, ov_ref):
    _compress_one(uk_ref, pk_ref, kw1_ref, kw2_ref, ok_ref)
    _compress_one(uv_ref, pv_ref, vw1_ref, vw2_ref, ov_ref)


def _compress(kc, vc, pos_k, pos_v, kw1, kw2, vw1, vw2):
    b, g, nu, unit = kc.shape
    dh = unit // CMP_STRIDE
    pos = lambda p: p.reshape(CMP_BLOCK // CMP_STRIDE, unit)
    u_spec = pl.BlockSpec((1, 1, nu, unit), lambda bi, gi: (bi, gi, 0, 0))
    full = lambda shape: pl.BlockSpec(shape, lambda bi, gi: (0,) * len(shape))
    o_spec = pl.BlockSpec((1, 1, nu, dh), lambda bi, gi: (bi, gi, 0, 0))
    o_shape = jax.ShapeDtypeStruct((b, g, nu, dh), BF16)
    return pl.pallas_call(
        _compress_body,
        grid=(b, g),
        in_specs=[u_spec, u_spec, full((2, unit)), full((2, unit)),
                  full((CMP_BLOCK * dh, CMP_HIDDEN)), full((CMP_HIDDEN, dh)),
                  full((CMP_BLOCK * dh, CMP_HIDDEN)), full((CMP_HIDDEN, dh))],
        out_specs=[o_spec, o_spec],
        out_shape=[o_shape, o_shape],
        compiler_params=_params(("parallel", "parallel"), 40),
        name="compress",
    )(kc, vc, pos(pos_k), pos(pos_v),
      kw1.astype(BF16), kw2.astype(BF16), vw1.astype(BF16), vw2.astype(BF16))


def _cmp_select_prefix(q_ref, kc_ref, vc_ref, ct_ref, ocmp_ref, sel_ref, *, tq, nc, nb):
    for sub in range(tq // CMP_SUB):
        _cmp_select_sub(q_ref, kc_ref, vc_ref, ct_ref, ocmp_ref, sel_ref,
                        row0=sub * CMP_SUB, q0=pl.program_id(2) * tq + sub * CMP_SUB, tq=CMP_SUB, nc=nc, nb=nb)


def _cmp_select_sub(q_ref, kc_ref, vc_ref, ct_ref, ocmp_ref, sel_ref, *, row0, q0, tq, nc, nb):
    n_slc = ct_ref.shape[0]
    q = q_ref[0, :, row0:row0 + tq, :].reshape(GROUP * tq, HEAD_DIM)
    s = lax.dot_general(q, kc_ref[0, 0, 0:nc, :], _NT, preferred_element_type=F32).reshape(GROUP, tq, nc)
    t = q0 + lax.broadcasted_iota(jnp.int32, (tq, nc), 0)
    cmp_end = lax.broadcasted_iota(jnp.int32, (tq, nc), 1) * CMP_STRIDE + (CMP_BLOCK - 1)
    s = s + jnp.where(cmp_end <= t, 0.0, MASKED)[None]
    m = jnp.max(s, axis=-1, keepdims=True)
    m = jnp.where(m > 0.5 * MASKED, m, 0.0)
    e = jnp.exp2(s - m)
    p = e * (1.0 / jnp.maximum(jnp.sum(e, axis=-1, keepdims=True), 1e-30))
    o = jnp.dot(p.reshape(GROUP * tq, nc).astype(BF16), vc_ref[0, 0, 0:nc, :], preferred_element_type=F32)
    ocmp_ref[0, :, row0:row0 + tq, :] = o.reshape(GROUP, tq, HEAD_DIM)

    psum = p[0] + p[1] + p[2] + p[3]
    hi = psum.astype(BF16)
    r1 = psum - hi.astype(F32)
    mid = r1.astype(BF16)
    lo = (r1 - mid.astype(F32)).astype(BF16)
    ct = ct_ref[0:nb, 0:nc]
    imp = (lax.dot_general(ct, hi, _NT, preferred_element_type=F32)
           + lax.dot_general(ct, mid, _NT, preferred_element_type=F32)
           + lax.dot_general(ct, lo, _NT, preferred_element_type=F32))

    blk = lax.broadcasted_iota(jnp.int32, (nb, tq), 0)
    jt = (q0 + lax.broadcasted_iota(jnp.int32, (nb, tq), 1)) // SLC_BLOCK
    forced = (blk == 0) | (blk == jt) | (blk == jt - 1)
    candidate = (blk >= 1) & (blk < jt - 1)
    val = jnp.where(candidate, imp, -1.0)
    blk_f = blk.astype(F32)
    for _ in range(N_SELECT - N_FORCED):
        best = jnp.max(val, axis=0, keepdims=True)
        first = jnp.min(jnp.where(val == best, blk_f, float(n_slc)), axis=0, keepdims=True)
        val = jnp.where(blk_f == first, -1.0, val)
    unsel = jnp.where(forced | (candidate & (val < 0.0)), 0.0, 1.0)
    if nb < n_slc:
        unsel = jnp.concatenate([unsel, jnp.ones((n_slc - nb, tq), F32)], axis=0)
    sel_ref[0, 0, row0:row0 + tq, :] = unsel.T.astype(BF16)


def _cmp_select_body(q_ref, kc_ref, vc_ref, ct_ref, ocmp_ref, sel_ref, *, tq):
    nu = kc_ref.shape[2]
    n_slc = ct_ref.shape[0]
    per_slc = nu // n_slc
    visible = (pl.program_id(2) * tq + tq - CMP_BLOCK) // CMP_STRIDE + 1
    n_prefix = nu // V7X_LANES
    need = jnp.clip((visible + V7X_LANES - 1) // V7X_LANES, 1, n_prefix)
    for v in range(1, n_prefix + 1):
        nc = v * V7X_LANES
        pl.when(need == v)(functools.partial(
            _cmp_select_prefix, q_ref, kc_ref, vc_ref, ct_ref, ocmp_ref, sel_ref, tq=tq, nc=nc, nb=nc // per_slc))


def _cmp_to_slc_t(nu, n_slc):
    per_slc = SLC_BLOCK // CMP_STRIDE
    c = np.arange(nu)[None, :]
    j = np.arange(n_slc)[:, None]
    m = np.zeros((n_slc, nu), np.float32)
    for unit in range(CMP_BLOCK // CMP_STRIDE):
        m += ((c + unit) // per_slc == j)
    m[:, nu - 1] = 0.0
    return jnp.asarray(m, BF16)


def _cmp_select(q_raw, k_cmp, v_cmp):
    b, _, s, dh = q_raw.shape
    nu = k_cmp.shape[2]
    n_slc = s // SLC_BLOCK
    tq = 4 * CMP_SUB
    for q_end in range(tq, s + 1, tq):
        prefix = -(-((q_end - CMP_BLOCK) // CMP_STRIDE + 1) // V7X_LANES) * V7X_LANES
        assert prefix * n_slc // nu >= (q_end - 1) // SLC_BLOCK + 1
    kv_spec = pl.BlockSpec((1, 1, nu, dh), lambda bi, gi, qi: (bi, gi, 0, 0))
    return pl.pallas_call(
        functools.partial(_cmp_select_body, tq=tq),
        grid=(b, N_KV, s // tq),
        in_specs=[pl.BlockSpec((1, GROUP, tq, dh), lambda bi, gi, qi: (bi, gi, qi, 0)),
                  kv_spec, kv_spec,
                  pl.BlockSpec((n_slc, nu), lambda bi, gi, qi: (0, 0))],
        out_specs=[pl.BlockSpec((1, GROUP, tq, dh), lambda bi, gi, qi: (bi, gi, qi, 0)),
                   pl.BlockSpec((1, 1, tq, n_slc), lambda bi, gi, qi: (bi, gi, qi, 0))],
        out_shape=[jax.ShapeDtypeStruct((b, N_HEADS, s, dh), F32),
                   jax.ShapeDtypeStruct((b, N_KV, s, n_slc), BF16)],
        compiler_params=_params(("parallel", "parallel", "parallel"), 32),
        name="cmp_select",
    )(q_raw, k_cmp, v_cmp, _cmp_to_slc_t(nu, n_slc))


LOWER, UPPER = 0, 1


def _attend_body(q_ref, k_ref, v_ref, tri_ref, unsel_ref, ocmp_ref, g_ref, o_ref,
                 qa_scr, s_scr, m_scr, acc_scr, *, tq, tk):
    q0 = pl.program_id(2) * tq
    rows = GROUP * tq

    unsel = unsel_ref[0, 0]
    for r in range(GROUP):
        qa_scr[r * tq:(r + 1) * tq, 0:HEAD_DIM] = q_ref[0, r]
        qa_scr[r * tq:(r + 1) * tq, HEAD_DIM:2 * HEAD_DIM] = unsel
    m_scr[...] = jnp.full(m_scr.shape, MASKED, F32)
    acc_scr[...] = jnp.zeros(acc_scr.shape, F32)

    last = q0 // tk
    n_win = jnp.minimum(last + 1, WINDOW // tk + 1)
    n_tiles = last + 1 + n_win

    def tile(i):
        branch = (i > last).astype(jnp.int32)
        kt = i - branch * n_win
        return branch, kt, pl.multiple_of(kt * tk, tk)

    def scores(i):
        branch, _, k0 = tile(i)
        s_scr[...] = lax.dot_general(qa_scr[...], k_ref[0, 0, branch, pl.ds(k0, tk), :], _NT,
                                     preferred_element_type=F32)

    def softmax_pv(i, boundary):
        branch, kt, k0 = tile(i)
        if boundary:
            kind = jnp.where(kt == last, LOWER, UPPER)
            s = (s_scr[...].reshape(GROUP, tq, tk) + tri_ref[kind][None]).reshape(rows, tk)
        else:
            s = s_scr[...]
        m_old = m_scr[branch]
        m_new = jnp.maximum(m_old, jnp.max(s, axis=-1, keepdims=True))
        p = jnp.exp2(s - jnp.concatenate([m_new] * (tk // V7X_LANES), axis=1)).astype(BF16)
        pv = jnp.dot(p, v_ref[0, 0, branch, pl.ds(k0, tk), :], preferred_element_type=F32)
        alpha = jnp.exp2(m_old - m_new)
        acc_scr[branch] = jnp.concatenate([alpha] * (2 * HEAD_DIM // V7X_LANES), axis=1) * acc_scr[branch] + pv
        m_scr[branch] = m_new

    scores(0)

    def interior_step(i, carry):
        softmax_pv(i, False)
        scores(i + 1)
        return carry

    def boundary_step(i, carry):
        softmax_pv(i, True)
        scores(i + 1)
        return carry

    lax.fori_loop(0, last, interior_step, 0)
    lax.fori_loop(last, n_tiles - 1, boundary_step, 0)
    softmax_pv(n_tiles - 1, True)

    def normalized(branch):
        acc = acc_scr[branch]
        return acc[:, 0:HEAD_DIM] * (1.0 / acc[:, HEAD_DIM:2 * HEAD_DIM])

    o_slc = normalized(SLC)
    o_win = normalized(WIN)
    gate = g_ref[0, 0]
    for r in range(GROUP):
        o_ref[0, :, r * HEAD_DIM:(r + 1) * HEAD_DIM] = (
            gate[:, 3 * r:3 * r + 1] * ocmp_ref[0, r]
            + gate[:, 3 * r + 1:3 * r + 2] * o_slc[r * tq:(r + 1) * tq]
            + gate[:, 3 * r + 2:3 * r + 3] * o_win[r * tq:(r + 1) * tq])


def _attend(q_rot, k_all, v_all, unsel, o_cmp, gates):
    b, _, s, dh = q_rot.shape
    n_slc = unsel.shape[3]
    assert n_slc == dh, "the unselected one-hot fills the second half of the augmented contraction"
    tq = tk = WINDOW
    row, col = np.arange(tq)[:, None], np.arange(tk)[None, :]
    tri = jnp.asarray(np.stack([np.where(col <= row, 0.0, MASKED),
                                np.where(col > row, 0.0, MASKED)]), F32)
    q_spec = pl.BlockSpec((1, GROUP, tq, dh), lambda bi, gi, qi: (bi, gi, qi, 0))
    kv_spec = _resident((1, 1, 2, s, 2 * dh), lambda bi, gi, qi: (bi, gi, 0, 0, 0))
    row_spec = lambda w: pl.BlockSpec((1, 1, tq, w), lambda bi, gi, qi: (bi, gi, qi, 0))
    rows = GROUP * tq
    return pl.pallas_call(
        functools.partial(_attend_body, tq=tq, tk=tk),
        grid=(b, N_KV, s // tq),
        in_specs=[q_spec, kv_spec, kv_spec, _resident(tri.shape, lambda bi, gi, qi: (0, 0, 0)),
                  row_spec(n_slc), q_spec, row_spec(V7X_LANES)],
        out_specs=pl.BlockSpec((1, tq, GROUP * dh), lambda bi, gi, qi: (bi, qi, gi)),
        out_shape=jax.ShapeDtypeStruct((b, s, Q_DIM), F32),
        scratch_shapes=[pltpu.VMEM((rows, 2 * dh), BF16),
                        pltpu.VMEM((rows, tk), F32),
                        pltpu.VMEM((2, rows, V7X_LANES), F32),
                        pltpu.VMEM((2, rows, 2 * dh), F32)],
        compiler_params=_params(("parallel", "parallel", "arbitrary"), 48),
        name="attend",
    )(q_rot, k_all, v_all, tri, unsel, o_cmp, gates)


def _out_proj_body(cn_ref, a_ref, x_ref, gn_ref, wc_ref, wa_ref, o_ref):
    an = _rms(a_ref[...], gn_ref[...]).astype(BF16)
    y = (jnp.dot(cn_ref[...], wc_ref[...], preferred_element_type=F32)
         + jnp.dot(an, wa_ref[...], preferred_element_type=F32))
    o_ref[...] = x_ref[...] + y


def _out_proj(conv_n, attn, x, nsa_g, w_out):
    t, d = x.shape
    tm = min(512, t)
    wc = w_out[:CONV_CH].astype(BF16)
    wa = w_out[CONV_CH:].astype(BF16)
    return pl.pallas_call(
        _out_proj_body,
        grid=(t // tm,),
        in_specs=[pl.BlockSpec((tm, CONV_CH), lambda i: (i, 0)),
                  pl.BlockSpec((tm, Q_DIM), lambda i: (i, 0)),
                  pl.BlockSpec((tm, d), lambda i: (i, 0)),
                  pl.BlockSpec((1, Q_DIM), lambda i: (0, 0)),
                  _resident((CONV_CH, d), lambda i: (0, 0)),
                  _resident((Q_DIM, d), lambda i: (0, 0))],
        out_specs=pl.BlockSpec((tm, d), lambda i: (i, 0)),
        out_shape=jax.ShapeDtypeStruct((t, d), F32),
        compiler_params=_params(("parallel",), 40),
        name="out_proj",
    )(conv_n, attn, x, nsa_g.reshape(1, Q_DIM), wc, wa)


def kernel(x, ffn1_norm, ffn1_w_gate, ffn1_w_up, ffn1_w_down, mix_norm, w_in, cmp_pos_k, cmp_pos_v, cmp_k_w1, cmp_k_w2, cmp_v_w1, cmp_v_w2, conv_dw_w, conv_dw_b, conv_ln_g, conv_ln_b, conv_pw_w, out_norm_conv, out_norm_nsa, w_out, ffn2_norm, ffn2_w_gate, ffn2_w_up, ffn2_w_down, final_norm):
    b, s, d = x.shape
    assert s % (SLC_BLOCK * V7X_LANES) == 0, "selection blocks must fill whole 128-lane rows"
    depth = ffn1_norm.shape[0]
    y = x.reshape(b * s, d)
    for l in range(depth):
        y = _ffn(y, ffn1_norm[l], ffn1_w_gate[l], ffn1_w_up[l], ffn1_w_down[l])
        u, q_raw, q_rot, kc, vc, k_all, v_all, gates = _in_proj(y, mix_norm[l], w_in[l], b, s)
        conv_n = _conv(u.reshape(b, s, CONV_CH), conv_dw_w[l], conv_dw_b[l], conv_ln_g[l], conv_ln_b[l],
                       conv_pw_w[l], out_norm_conv[l])
        k_cmp, v_cmp = _compress(kc, vc, cmp_pos_k[l], cmp_pos_v[l],
                                 cmp_k_w1[l], cmp_k_w2[l], cmp_v_w1[l], cmp_v_w2[l])
        o_cmp, unsel = _cmp_select(q_raw, k_cmp, v_cmp)
        attn = _attend(q_rot, k_all, v_all, unsel, o_cmp, gates)
        y = _out_proj(conv_n.reshape(b * s, CONV_CH), attn.reshape(b * s, Q_DIM), y, out_norm_nsa[l], w_out[l])
        y = _ffn(y, ffn2_norm[l], ffn2_w_gate[l], ffn2_w_up[l], ffn2_w_down[l],
                 final_g=final_norm if l == depth - 1 else None)
    return y.reshape(b, s, d)
```

```python
import functools

import numpy as np
import jax
import jax.numpy as jnp
from jax import lax
from jax.experimental import pallas as pl
from jax.experimental.pallas import tpu as pltpu

F32 = jnp.float32
BF16 = jnp.bfloat16

V7X_LANES = 128
V7X_SUBLANES = 8

CONV_CH = 512
N_HEADS = 12
HEAD_DIM = 128
N_KV = 3
GROUP = N_HEADS // N_KV
CONV_K = 31
CMP_BLOCK = 32
CMP_STRIDE = 16
CMP_HIDDEN = 256
SLC_BLOCK = 64
N_SELECT = 16
N_FORCED = 3
WINDOW = 512
ROPE_THETA = 10000.0
EPS = 1e-6
MASKED = -1e30
LOG2_E = 1.4426950408889634
SLC, WIN = 0, 1

KV_DIM = N_KV * HEAD_DIM
Q_DIM = N_HEADS * HEAD_DIM
CMP_SUB = 128
CONV_HALO = 32

_NT = (((1,), (1,)), ((), ()))


def _rms(x, g):
    return x * lax.rsqrt(jnp.mean(x * x, axis=-1, keepdims=True) + EPS) * g


def _params(semantics, vmem_mib):
    return pltpu.CompilerParams(dimension_semantics=semantics, vmem_limit_bytes=vmem_mib * 2 ** 20)


def _resident(shape, index_map):
    return pl.BlockSpec(shape, index_map, pipeline_mode=pl.Buffered(1))


def _ffn_body(*refs, n_main, has_tail, final_norm):
    refs = list(refs)
    x_ref, g_ref, wg_ref, wu_ref, wd_ref = refs[:5]
    del refs[:5]
    if has_tail:
        tail_refs = refs[:3]
        del refs[:3]
    if final_norm:
        fg_ref = refs.pop(0)
    o_ref, h_scr = refs
    j = pl.program_id(1)

    @pl.when(j == 0)
    def _():
        x = x_ref[...]
        h_scr[...] = _rms(x, g_ref[...]).astype(BF16)
        o_ref[...] = x

    def hidden_slab(wg, wu, wd):
        h = h_scr[...]
        a = jnp.dot(h, wg[...], preferred_element_type=F32)
        b = jnp.dot(h, wu[...], preferred_element_type=F32)
        z = (a * jax.nn.sigmoid(a) * b).astype(BF16)
        o_ref[...] += 0.5 * jnp.dot(z, wd[...], preferred_element_type=F32)

    if has_tail:
        pl.when(j < n_main)(functools.partial(hidden_slab, wg_ref, wu_ref, wd_ref))
        pl.when(j == n_main)(functools.partial(hidden_slab, *tail_refs))
    else:
        hidden_slab(wg_ref, wu_ref, wd_ref)

    if final_norm:
        @pl.when(j == n_main + has_tail - 1)
        def _():
            o_ref[...] = _rms(o_ref[...], fg_ref[...])


def _ffn(x, g, w_gate, w_up, w_down, final_g=None):
    t, d = x.shape
    f = w_gate.shape[1]
    tm = min(512, t)
    tf = 512
    n_main, f_tail = divmod(f, tf)
    assert f_tail % V7X_LANES == 0 and n_main >= 1
    has_tail = int(f_tail > 0)
    f_main = n_main * tf
    wg, wu, wd = w_gate.astype(BF16), w_up.astype(BF16), w_down.astype(BF16)
    final_norm = final_g is not None
    row = pl.BlockSpec((tm, d), lambda i, j: (i, 0))
    vec = pl.BlockSpec((1, d), lambda i, j: (0, 0))
    slab = lambda i, j: (0, jnp.minimum(j, n_main - 1))
    in_specs = [row, vec,
                pl.BlockSpec((d, tf), slab),
                pl.BlockSpec((d, tf), slab),
                pl.BlockSpec((tf, d), lambda i, j: (jnp.minimum(j, n_main - 1), 0))]
    args = [x, g.reshape(1, d), wg, wu, wd]
    if has_tail:
        in_specs += [_resident((d, f_tail), lambda i, j: (0, 0)),
                     _resident((d, f_tail), lambda i, j: (0, 0)),
                     _resident((f_tail, d), lambda i, j: (0, 0))]
        args += [wg[:, f_main:], wu[:, f_main:], wd[f_main:]]
    if final_norm:
        in_specs.append(vec)
        args.append(final_g.reshape(1, d))
    return pl.pallas_call(
        functools.partial(_ffn_body, n_main=n_main, has_tail=has_tail, final_norm=final_norm),
        grid=(t // tm, n_main + has_tail),
        in_specs=in_specs,
        out_specs=row,
        out_shape=jax.ShapeDtypeStruct((t, d), F32),
        scratch_shapes=[pltpu.VMEM((tm, d), BF16)],
        compiler_params=_params(("parallel", "arbitrary"), 48),
        name="ffn_final" if final_norm else "ffn",
    )(*args)


def _rope(x, cos2, sin2):
    return x * cos2 + pltpu.roll(x, HEAD_DIM // 2, 1) * sin2


def _in_proj_body(x_ref, g_ref, w_ref, wgate_ref, cos_ref, sin_ref, blk_ref,
                  u_ref, qraw_ref, qrot_ref, kc_ref, vc_ref, k_ref, v_ref, gate_ref, kv_scr):
    h = _rms(x_ref[...], g_ref[...]).astype(BF16)
    cos2 = cos_ref[...]
    sin2 = sin_ref[...]
    scale = HEAD_DIM ** -0.5

    def proj(c0, width):
        return jnp.dot(h, w_ref[:, c0:c0 + width], preferred_element_type=F32)

    glu = proj(0, 2 * CONV_CH)
    u_ref[...] = glu[:, :CONV_CH] * jax.nn.sigmoid(glu[:, CONV_CH:])

    def head(cols, i):
        return cols[:, i * HEAD_DIM:(i + 1) * HEAD_DIM]

    c0 = 2 * CONV_CH
    for gk in range(N_KV):
        cols = proj(c0, GROUP * HEAD_DIM) * (scale * LOG2_E)
        for r in range(GROUP):
            qh = head(cols, r)
            qraw_ref[0, gk * GROUP + r] = qh.astype(BF16)
            qrot_ref[0, gk * GROUP + r] = _rope(qh, cos2, sin2).astype(BF16)
        c0 += GROUP * HEAD_DIM
    cols = proj(c0, 2 * KV_DIM)
    n_unit = kv_scr.shape[1] // CMP_STRIDE
    for ref, first in ((kc_ref, 0), (vc_ref, N_KV)):
        for gk in range(N_KV):
            kv_scr[first + gk] = head(cols, first + gk)
            for slot in range(CMP_STRIDE):
                ref[0, gk, :, slot * HEAD_DIM:(slot + 1) * HEAD_DIM] = kv_scr[
                    first + gk, pl.ds(slot, n_unit, stride=CMP_STRIDE), :]
    c0 += 2 * KV_DIM
    lo, hi = slice(0, HEAD_DIM), slice(HEAD_DIM, 2 * HEAD_DIM)
    for branch in (SLC, WIN):
        cols = proj(c0, 2 * KV_DIM)
        for gk in range(N_KV):
            k_ref[0, gk, branch, :, lo] = _rope(head(cols, gk), cos2, sin2).astype(BF16)
            v_ref[0, gk, branch, :, lo] = head(cols, N_KV + gk).astype(BF16)
        c0 += 2 * KV_DIM
    cols = jax.nn.sigmoid(jnp.dot(h, wgate_ref[...], preferred_element_type=F32))
    for gk in range(N_KV):
        k_ref[0, gk, SLC, :, hi] = blk_ref[...]
        k_ref[0, gk, WIN, :, hi] = jnp.zeros(blk_ref.shape, BF16)
        v_ref[0, gk, SLC, :, hi] = jnp.ones(blk_ref.shape, BF16)
        v_ref[0, gk, WIN, :, hi] = jnp.ones(blk_ref.shape, BF16)
        gate_ref[0, gk] = head(cols, gk)


def _in_proj(x, g, w_in, batch, seq):
    t, d = x.shape
    tm = min(256, seq)
    n_s = seq // tm
    main = 2 * CONV_CH + Q_DIM + 6 * KV_DIM
    gate_w = w_in[:, main:].reshape(d, N_KV, GROUP * 3)
    gate_w = jnp.pad(gate_w, ((0, 0), (0, 0), (0, V7X_LANES - GROUP * 3))).reshape(d, N_KV * V7X_LANES)
    gate_w = gate_w.astype(BF16)
    w = w_in[:, :main].astype(BF16)

    inv = jnp.power(ROPE_THETA, -jnp.arange(0, HEAD_DIM, 2, dtype=F32) / HEAD_DIM)
    ang = jnp.arange(seq, dtype=F32)[:, None] * inv[None, :]
    cos2 = jnp.concatenate([jnp.cos(ang), jnp.cos(ang)], axis=1)
    sin2 = jnp.concatenate([-jnp.sin(ang), jnp.sin(ang)], axis=1)

    key_blk = np.arange(seq)[:, None] // SLC_BLOCK == np.arange(seq // SLC_BLOCK)[None, :]
    blk_mask = jnp.asarray(np.where(key_blk, -(2.0 ** 100), 0.0), BF16)

    def heads(n, dtype):
        return (jax.ShapeDtypeStruct((batch, n, seq, HEAD_DIM), dtype),
                pl.BlockSpec((1, n, tm, HEAD_DIM), lambda i: (i // n_s, 0, i % n_s, 0)))

    stacked = (jax.ShapeDtypeStruct((batch, N_KV, 2, seq, 2 * HEAD_DIM), BF16),
               pl.BlockSpec((1, N_KV, 2, tm, 2 * HEAD_DIM), lambda i: (i // n_s, 0, 0, i % n_s, 0)))
    unit = CMP_STRIDE * HEAD_DIM
    units = (jax.ShapeDtypeStruct((batch, N_KV, seq // CMP_STRIDE, unit), F32),
             pl.BlockSpec((1, N_KV, tm // CMP_STRIDE, unit), lambda i: (i // n_s, 0, i % n_s, 0)))
    outs = [(jax.ShapeDtypeStruct((t, CONV_CH), F32), pl.BlockSpec((tm, CONV_CH), lambda i: (i, 0))),
            heads(N_HEADS, BF16), heads(N_HEADS, BF16),
            units, units,
            stacked, stacked,
            heads(N_KV, F32)]
    table = pl.BlockSpec((tm, HEAD_DIM), lambda i: (i % n_s, 0))
    return pl.pallas_call(
        _in_proj_body,
        grid=(t // tm,),
        in_specs=[pl.BlockSpec((tm, d), lambda i: (i, 0)),
                  pl.BlockSpec((1, d), lambda i: (0, 0)),
                  _resident((d, main), lambda i: (0, 0)),
                  _resident(gate_w.shape, lambda i: (0, 0)),
                  table, table, table],
        out_specs=[o[1] for o in outs],
        out_shape=[o[0] for o in outs],
        scratch_shapes=[pltpu.VMEM((2 * N_KV, tm, HEAD_DIM), F32)],
        compiler_params=_params(("parallel",), 48),
        name="in_proj",
    )(x, g.reshape(1, d), w, gate_w, cos2, sin2, blk_mask)


CONV_ROWS = 32


def _conv_body(u_ref, halo_ref, dw_ref, db_ref, lg_ref, lb_ref, pw_ref, og_ref, o_ref, ext_scr, y_scr, *, ts):
    i = pl.program_id(1)
    ext_scr[0, 0:CONV_HALO, :] = jnp.where(i == 0, 0.0, halo_ref[0])
    ext_scr[0, CONV_HALO:CONV_HALO + ts, :] = u_ref[0]
    moved = CONV_HALO + ts - V7X_SUBLANES
    for s in range(1, V7X_SUBLANES):
        ext_scr[s, 0:moved, :] = ext_scr[0, s:s + moved, :]
    first = CONV_HALO - (CONV_K - 1)
    for c in range(ts // CONV_ROWS):
        r0 = c * CONV_ROWS
        acc = jnp.broadcast_to(db_ref[...], (CONV_ROWS, CONV_CH))
        for k in range(CONV_K):
            s, base = (first + k) % V7X_SUBLANES, (first + k) // V7X_SUBLANES * V7X_SUBLANES
            acc = acc + dw_ref[k:k + 1, :] * ext_scr[s, r0 + base:r0 + base + CONV_ROWS, :]
        mu = jnp.mean(acc, axis=-1, keepdims=True)
        xc = acc - mu
        var = jnp.mean(xc * xc, axis=-1, keepdims=True)
        y = xc * lax.rsqrt(var + EPS) * lg_ref[...] + lb_ref[...]
        y_scr[r0:r0 + CONV_ROWS, :] = (y * jax.nn.sigmoid(y)).astype(BF16)
    z = jnp.dot(y_scr[...], pw_ref[...], preferred_element_type=F32)
    o_ref[0] = _rms(z, og_ref[...]).astype(BF16)


def _conv(u, dw_w, dw_b, ln_g, ln_b, pw_w, out_g):
    b, s, c = u.shape
    ts = min(256, s)
    per = ts // CONV_HALO
    vec = pl.BlockSpec((1, c), lambda bi, i: (0, 0))
    return pl.pallas_call(
        functools.partial(_conv_body, ts=ts),
        grid=(b, s // ts),
        in_specs=[pl.BlockSpec((1, ts, c), lambda bi, i: (bi, i, 0)),
                  pl.BlockSpec((1, CONV_HALO, c), lambda bi, i: (bi, jnp.maximum(i * per - 1, 0), 0)),
                  pl.BlockSpec((CONV_K, c), lambda bi, i: (0, 0)),
                  vec, vec, vec,
                  pl.BlockSpec((c, c), lambda bi, i: (0, 0)),
                  vec],
        out_specs=pl.BlockSpec((1, ts, c), lambda bi, i: (bi, i, 0)),
        out_shape=jax.ShapeDtypeStruct((b, s, c), BF16),
        scratch_shapes=[pltpu.VMEM((V7X_SUBLANES, CONV_HALO + ts, c), F32), pltpu.VMEM((ts, c), BF16)],
        compiler_params=_params(("parallel", "parallel"), 32),
        name="conv",
    )(u, u, dw_w, dw_b.reshape(1, c), ln_g.reshape(1, c), ln_b.reshape(1, c), pw_w.astype(BF16),
      out_g.reshape(1, c))


def _compress_one(u_ref, pos_ref, w1_ref, w2_ref, o_ref):
    u = u_ref[0, 0]
    half = u.shape[1]
    nu = u.shape[0]
    top = jnp.dot((u + pos_ref[0:1, :]).astype(BF16), w1_ref[0:half, :], preferred_element_type=F32)
    bot = jnp.dot((u + pos_ref[1:2, :]).astype(BF16), w1_ref[half:2 * half, :], preferred_element_type=F32)
    hid = top + pltpu.roll(bot, nu - 1, 0)
    hid = hid * jax.nn.sigmoid(hid)
    o_ref[0, 0] = jnp.dot(hid.astype(BF16), w2_ref[...], preferred_element_type=F32).astype(BF16)


def _compress_body(uk_ref, uv_ref, pk_ref, pv_ref, kw1_ref, kw2_ref, vw1_ref, vw2_ref, ok_ref, ov_ref):
    _compress_one(uk_ref, pk_ref, kw1_ref, kw2_ref, ok_ref)
    _compress_one(uv_ref, pv_ref, vw1_ref, vw2_ref, ov_ref)


def _compress(kc, vc, pos_k, pos_v, kw1, kw2, vw1, vw2):
    b, g, nu, unit = kc.shape
    dh = unit // CMP_STRIDE
    pos = lambda p: p.reshape(CMP_BLOCK // CMP_STRIDE, unit)
    u_spec = pl.BlockSpec((1, 1, nu, unit), lambda bi, gi: (bi, gi, 0, 0))
    full = lambda shape: pl.BlockSpec(shape, lambda bi, gi: (0,) * len(shape))
    o_spec = pl.BlockSpec((1, 1, nu, dh), lambda bi, gi: (bi, gi, 0, 0))
    o_shape = jax.ShapeDtypeStruct((b, g, nu, dh), BF16)
    return pl.pallas_call(
        _compress_body,
        grid=(b, g),
        in_specs=[u_spec, u_spec, full((2, unit)), full((2, unit)),
                  full((CMP_BLOCK * dh, CMP_HIDDEN)), full((CMP_HIDDEN, dh)),
                  full((CMP_BLOCK * dh, CMP_HIDDEN)), full((CMP_HIDDEN, dh))],
        out_specs=[o_spec, o_spec],
        out_shape=[o_shape, o_shape],
        compiler_params=_params(("parallel", "parallel"), 40),
        name="compress",
    )(kc, vc, pos(pos_k), pos(pos_v),
      kw1.astype(BF16), kw2.astype(BF16), vw1.astype(BF16), vw2.astype(BF16))


def _cmp_select_prefix(q_ref, kc_ref, vc_ref, ct_ref, ocmp_ref, sel_ref, *, tq, nc, nb):
    for sub in range(tq // CMP_SUB):
        _cmp_select_sub(q_ref, kc_ref, vc_ref, ct_ref, ocmp_ref, sel_ref,
                        row0=sub * CMP_SUB, q0=pl.program_id(2) * tq + sub * CMP_SUB, tq=CMP_SUB, nc=nc, nb=nb)


def _cmp_select_sub(q_ref, kc_ref, vc_ref, ct_ref, ocmp_ref, sel_ref, *, row0, q0, tq, nc, nb):
    n_slc = ct_ref.shape[0]
    q = q_ref[0, :, row0:row0 + tq, :].reshape(GROUP * tq, HEAD_DIM)
    s = lax.dot_general(q, kc_ref[0, 0, 0:nc, :], _NT, preferred_element_type=F32).reshape(GROUP, tq, nc)
    t = q0 + lax.broadcasted_iota(jnp.int32, (tq, nc), 0)
    cmp_end = lax.broadcasted_iota(jnp.int32, (tq, nc), 1) * CMP_STRIDE + (CMP_BLOCK - 1)
    s = s + jnp.where(cmp_end <= t, 0.0, MASKED)[None]
    m = jnp.max(s, axis=-1, keepdims=True)
    m = jnp.where(m > 0.5 * MASKED, m, 0.0)
    e = jnp.exp2(s - m)
    p = e * (1.0 / jnp.maximum(jnp.sum(e, axis=-1, keepdims=True), 1e-30))
    o = jnp.dot(p.reshape(GROUP * tq, nc).astype(BF16), vc_ref[0, 0, 0:nc, :], preferred_element_type=F32)
    ocmp_ref[0, :, row0:row0 + tq, :] = o.reshape(GROUP, tq, HEAD_DIM)

    psum = p[0] + p[1] + p[2] + p[3]
    hi = psum.astype(BF16)
    r1 = psum - hi.astype(F32)
    mid = r1.astype(BF16)
    lo = (r1 - mid.astype(F32)).astype(BF16)
    ct = ct_ref[0:nb, 0:nc]
    imp = (lax.dot_general(ct, hi, _NT, preferred_element_type=F32)
           + lax.dot_general(ct, mid, _NT, preferred_element_type=F32)
           + lax.dot_general(ct, lo, _NT, preferred_element_type=F32))

    blk = lax.broadcasted_iota(jnp.int32, (nb, tq), 0)
    jt = (q0 + lax.broadcasted_iota(jnp.int32, (nb, tq), 1)) // SLC_BLOCK
    forced = (blk == 0) | (blk == jt) | (blk == jt - 1)
    candidate = (blk >= 1) & (blk < jt - 1)
    val = jnp.where(candidate, imp, -1.0)
    blk_f = blk.astype(F32)
    for _ in range(N_SELECT - N_FORCED):
        best = jnp.max(val, axis=0, keepdims=True)
        first = jnp.min(jnp.where(val == best, blk_f, float(n_slc)), axis=0, keepdims=True)
        val = jnp.where(blk_f == first, -1.0, val)
    unsel = jnp.where(forced | (candidate & (val < 0.0)), 0.0, 1.0)
    if nb < n_slc:
        unsel = jnp.concatenate([unsel, jnp.ones((n_slc - nb, tq), F32)], axis=0)
    sel_ref[0, 0, row0:row0 + tq, :] = unsel.T.astype(BF16)


def _cmp_select_body(q_ref, kc_ref, vc_ref, ct_ref, ocmp_ref, sel_ref, *, tq):
    nu = kc_ref.shape[2]
    n_slc = ct_ref.shape[0]
    per_slc = nu // n_slc
    visible = (pl.program_id(2) * tq + tq - CMP_BLOCK) // CMP_STRIDE + 1
    n_prefix = nu // V7X_LANES
    need = jnp.clip((visible + V7X_LANES - 1) // V7X_LANES, 1, n_prefix)
    for v in range(1, n_prefix + 1):
        nc = v * V7X_LANES
        pl.when(need == v)(functools.partial(
            _cmp_select_prefix, q_ref, kc_ref, vc_ref, ct_ref, ocmp_ref, sel_ref, tq=tq, nc=nc, nb=nc // per_slc))


def _cmp_to_slc_t(nu, n_slc):
    per_slc = SLC_BLOCK // CMP_STRIDE
    c = np.arange(nu)[None, :]
    j = np.arange(n_slc)[:, None]
    m = np.zeros((n_slc, nu), np.float32)
    for unit in range(CMP_BLOCK // CMP_STRIDE):
        m += ((c + unit) // per_slc == j)
    m[:, nu - 1] = 0.0
    return jnp.asarray(m, BF16)


def _cmp_select(q_raw, k_cmp, v_cmp):
    b, _, s, dh = q_raw.shape
    nu = k_cmp.shape[2]
    n_slc = s // SLC_BLOCK
    tq = 4 * CMP_SUB
    for q_end in range(tq, s + 1, tq):
        prefix = -(-((q_end - CMP_BLOCK) // CMP_STRIDE + 1) // V7X_LANES) * V7X_LANES
        assert prefix * n_slc // nu >= (q_end - 1) // SLC_BLOCK + 1
    kv_spec = pl.BlockSpec((1, 1, nu, dh), lambda bi, gi, qi: (bi, gi, 0, 0))
    return pl.pallas_call(
        functools.partial(_cmp_select_body, tq=tq),
        grid=(b, N_KV, s // tq),
        in_specs=[pl.BlockSpec((1, GROUP, tq, dh), lambda bi, gi, qi: (bi, gi, qi, 0)),
                  kv_spec, kv_spec,
                  pl.BlockSpec((n_slc, nu), lambda bi, gi, qi: (0, 0))],
        out_specs=[pl.BlockSpec((1, GROUP, tq, dh), lambda bi, gi, qi: (bi, gi, qi, 0)),
                   pl.BlockSpec((1, 1, tq, n_slc), lambda bi, gi, qi: (bi, gi, qi, 0))],
        out_shape=[jax.ShapeDtypeStruct((b, N_HEADS, s, dh), F32),
                   jax.ShapeDtypeStruct((b, N_KV, s, n_slc), BF16)],
        compiler_params=_params(("parallel", "parallel", "parallel"), 32),
        name="cmp_select",
    )(q_raw, k_cmp, v_cmp, _cmp_to_slc_t(nu, n_slc))


LOWER, UPPER = 0, 1


def _attend_body(q_ref, k_ref, v_ref, tri_ref, unsel_ref, ocmp_ref, g_ref, o_ref,
                 qa_scr, s_scr, m_scr, acc_scr, *, tq, tk):
    q0 = pl.program_id(2) * tq
    rows = GROUP * tq

    unsel = unsel_ref[0, 0]
    for r in range(GROUP):
        qa_scr[r * tq:(r + 1) * tq, 0:HEAD_DIM] = q_ref[0, r]
        qa_scr[r * tq:(r + 1) * tq, HEAD_DIM:2 * HEAD_DIM] = unsel
    m_scr[...] = jnp.full(m_scr.shape, MASKED, F32)
    acc_scr[...] = jnp.zeros(acc_scr.shape, F32)

    last = q0 // tk
    n_win = jnp.minimum(last + 1, WINDOW // tk + 1)
    n_tiles = last + 1 + n_win

    def tile(i):
        branch = (i > last).astype(jnp.int32)
        kt = i - branch * n_win
        return branch, kt, pl.multiple_of(kt * tk, tk)

    def scores(i):
        branch, _, k0 = tile(i)
        s_scr[...] = lax.dot_general(qa_scr[...], k_ref[0, 0, branch, pl.ds(k0, tk), :], _NT,
                                     preferred_element_type=F32)

    def softmax_pv(i, boundary):
        branch, kt, k0 = tile(i)
        if boundary:
            kind = jnp.where(kt == last, LOWER, UPPER)
            s = (s_scr[...].reshape(GROUP, tq, tk) + tri_ref[kind][None]).reshape(rows, tk)
        else:
            s = s_scr[...]
        m_old = m_scr[branch]
        m_new = jnp.maximum(m_old, jnp.max(s, axis=-1, keepdims=True))
        p = jnp.exp2(s - jnp.concatenate([m_new] * (tk // V7X_LANES), axis=1)).astype(BF16)
        pv = jnp.dot(p, v_ref[0, 0, branch, pl.ds(k0, tk), :], preferred_element_type=F32)
        alpha = jnp.exp2(m_old - m_new)
        acc_scr[branch] = jnp.concatenate([alpha] * (2 * HEAD_DIM // V7X_LANES), axis=1) * acc_scr[branch] + pv
        m_scr[branch] = m_new

    scores(0)

    def interior_step(i, carry):
        softmax_pv(i, False)
        scores(i + 1)
        return carry

    def boundary_step(i, carry):
        softmax_pv(i, True)
        scores(i + 1)
        return carry

    lax.fori_loop(0, last, interior_step, 0)
    lax.fori_loop(last, n_tiles - 1, boundary_step, 0)
    softmax_pv(n_tiles - 1, True)

    def normalized(branch):
        acc = acc_scr[branch]
        return acc[:, 0:HEAD_DIM] * (1.0 / acc[:, HEAD_DIM:2 * HEAD_DIM])

    o_slc = normalized(SLC)
    o_win = normalized(WIN)
    gate = g_ref[0, 0]
    for r in range(GROUP):
        o_ref[0, :, r * HEAD_DIM:(r + 1) * HEAD_DIM] = (
            gate[:, 3 * r:3 * r + 1] * ocmp_ref[0, r]
            + gate[:, 3 * r + 1:3 * r + 2] * o_slc[r * tq:(r + 1) * tq]
            + gate[:, 3 * r + 2:3 * r + 3] * o_win[r * tq:(r + 1) * tq])


def _attend(q_rot, k_all, v_all, unsel, o_cmp, gates):
    b, _, s, dh = q_rot.shape
    n_slc = unsel.shape[3]
    assert n_slc == dh, "the unselected one-hot fills the second half of the augmented contraction"
    tq = tk = WINDOW
    row, col = np.arange(tq)[:, None], np.arange(tk)[None, :]
    tri = jnp.asarray(np.stack([np.where(col <= row, 0.0, MASKED),
                                np.where(col > row, 0.0, MASKED)]), F32)
    q_spec = pl.BlockSpec((1, GROUP, tq, dh), lambda bi, gi, qi: (bi, gi, qi, 0))
    kv_spec = _resident((1, 1, 2, s, 2 * dh), lambda bi, gi, qi: (bi, gi, 0, 0, 0))
    row_spec = lambda w: pl.BlockSpec((1, 1, tq, w), lambda bi, gi, qi: (bi, gi, qi, 0))
    rows = GROUP * tq
    return pl.pallas_call(
        functools.partial(_attend_body, tq=tq, tk=tk),
        grid=(b, N_KV, s // tq),
        in_specs=[q_spec, kv_spec, kv_spec, _resident(tri.shape, lambda bi, gi, qi: (0, 0, 0)),
                  row_spec(n_slc), q_spec, row_spec(V7X_LANES)],
        out_specs=pl.BlockSpec((1, tq, GROUP * dh), lambda bi, gi, qi: (bi, qi, gi)),
        out_shape=jax.ShapeDtypeStruct((b, s, Q_DIM), F32),
        scratch_shapes=[pltpu.VMEM((rows, 2 * dh), BF16),
                        pltpu.VMEM((rows, tk), F32),
                        pltpu.VMEM((2, rows, V7X_LANES), F32),
                        pltpu.VMEM((2, rows, 2 * dh), F32)],
        compiler_params=_params(("parallel", "parallel", "arbitrary"), 48),
        name="attend",
    )(q_rot, k_all, v_all, tri, unsel, o_cmp, gates)


def _out_proj_body(cn_ref, a_ref, x_ref, gn_ref, wc_ref, wa_ref, o_ref):
    an = _rms(a_ref[...], gn_ref[...]).astype(BF16)
    y = (jnp.dot(cn_ref[...], wc_ref[...], preferred_element_type=F32)
         + jnp.dot(an, wa_ref[...], preferred_element_type=F32))
    o_ref[...] = x_ref[...] + y


def _out_proj(conv_n, attn, x, nsa_g, w_out):
    t, d = x.shape
    tm = min(512, t)
    wc = w_out[:CONV_CH].astype(BF16)
    wa = w_out[CONV_CH:].astype(BF16)
    return pl.pallas_call(
        _out_proj_body,
        grid=(t // tm,),
        in_specs=[pl.BlockSpec((tm, CONV_CH), lambda i: (i, 0)),
                  pl.BlockSpec((tm, Q_DIM), lambda i: (i, 0)),
                  pl.BlockSpec((tm, d), lambda i: (i, 0)),
                  pl.BlockSpec((1, Q_DIM), lambda i: (0, 0)),
                  _resident((CONV_CH, d), lambda i: (0, 0)),
                  _resident((Q_DIM, d), lambda i: (0, 0))],
        out_specs=pl.BlockSpec((tm, d), lambda i: (i, 0)),
        out_shape=jax.ShapeDtypeStruct((t, d), F32),
        compiler_params=_params(("parallel",), 40),
        name="out_proj",
    )(conv_n, attn, x, nsa_g.reshape(1, Q_DIM), wc, wa)


def kernel(x, ffn1_norm, ffn1_w_gate, ffn1_w_up, ffn1_w_down, mix_norm, w_in, cmp_pos_k, cmp_pos_v, cmp_k_w1, cmp_k_w2, cmp_v_w1, cmp_v_w2, conv_dw_w, conv_dw_b, conv_ln_g, conv_ln_b, conv_pw_w, out_norm_conv, out_norm_nsa, w_out, ffn2_norm, ffn2_w_gate, ffn2_w_up, ffn2_w_down, final_norm):
    b, s, d = x.shape
    assert s % (SLC_BLOCK * V7X_LANES) == 0, "selection blocks must fill whole 128-lane rows"
    depth = ffn1_norm.shape[0]
    y = x.reshape(b * s, d)
    for l in range(depth):
        y = _ffn(y, ffn1_norm[l], ffn1_w_gate[l], ffn1_w_up[l], ffn1_w_down[l])
        u, q_raw, q_rot, kc, vc, k_all, v_all, gates = _in_proj(y, mix_norm[l], w_in[l], b, s)
        conv_n = _conv(u.reshape(b, s, CONV_CH), conv_dw_w[l], conv_dw_b[l], conv_ln_g[l], conv_ln_b[l],
                       conv_pw_w[l], out_norm_conv[l])
        k_cmp, v_cmp = _compress(kc, vc, cmp_pos_k[l], cmp_pos_v[l],
                                 cmp_k_w1[l], cmp_k_w2[l], cmp_v_w1[l], cmp_v_w2[l])
        o_cmp, unsel = _cmp_select(q_raw, k_cmp, v_cmp)
        attn = _attend(q_rot, k_all, v_all, unsel, o_cmp, gates)
        y = _out_proj(conv_n.reshape(b * s, CONV_CH), attn.reshape(b * s, Q_DIM), y, out_norm_nsa[l], w_out[l])
        y = _ffn(y, ffn2_norm[l], ffn2_w_gate[l], ffn2_w_up[l], ffn2_w_down[l],
                 final_g=final_norm if l == depth - 1 else None)
    return y.reshape(b, s, d)
```

```python
import functools

import numpy as np
import jax
import jax.numpy as jnp
from jax import lax
from jax.experimental import pallas as pl
from jax.experimental.pallas import tpu as pltpu

F32 = jnp.float32
BF16 = jnp.bfloat16

V7X_LANES = 128
V7X_SUBLANES = 8

CONV_CH = 512
N_HEADS = 12
HEAD_DIM = 128
N_KV = 3
GROUP = N_HEADS // N_KV
CONV_K = 31
CMP_BLOCK = 32
CMP_STRIDE = 16
CMP_HIDDEN = 256
SLC_BLOCK = 64
N_SELECT = 16
N_FORCED = 3
WINDOW = 512
ROPE_THETA = 10000.0
EPS = 1e-6
MASKED = -1e30
LOG2_E = 1.4426950408889634
SLC, WIN = 0, 1

KV_DIM = N_KV * HEAD_DIM
Q_DIM = N_HEADS * HEAD_DIM
CMP_SUB = 128
CONV_HALO = 32

_NT = (((1,), (1,)), ((), ()))


def _rms(x, g):
    return x * lax.rsqrt(jnp.mean(x * x, axis=-1, keepdims=True) + EPS) * g


def _params(semantics, vmem_mib):
    return pltpu.CompilerParams(dimension_semantics=semantics, vmem_limit_bytes=vmem_mib * 2 ** 20)


def _resident(shape, index_map):
    return pl.BlockSpec(shape, index_map, pipeline_mode=pl.Buffered(1))


def _ffn_body(*refs, n_main, has_tail, final_norm):
    refs = list(refs)
    x_ref, g_ref, wg_ref, wu_ref, wd_ref = refs[:5]
    del refs[:5]
    if has_tail:
        tail_refs = refs[:3]
        del refs[:3]
    if final_norm:
        fg_ref = refs.pop(0)
    o_ref, h_scr = refs
    j = pl.program_id(1)

    @pl.when(j == 0)
    def _():
        x = x_ref[...]
        h_scr[...] = _rms(x, g_ref[...]).astype(BF16)
        o_ref[...] = x

    def hidden_slab(wg, wu, wd):
        h = h_scr[...]
        a = jnp.dot(h, wg[...], preferred_element_type=F32)
        b = jnp.dot(h, wu[...], preferred_element_type=F32)
        z = (a * jax.nn.sigmoid(a) * b).astype(BF16)
        o_ref[...] += 0.5 * jnp.dot(z, wd[...], preferred_element_type=F32)

    if has_tail:
        pl.when(j < n_main)(functools.partial(hidden_slab, wg_ref, wu_ref, wd_ref))
        pl.when(j == n_main)(functools.partial(hidden_slab, *tail_refs))
    else:
        hidden_slab(wg_ref, wu_ref, wd_ref)

    if final_norm:
        @pl.when(j == n_main + has_tail - 1)
        def _():
            o_ref[...] = _rms(o_ref[...], fg_ref[...])


def _ffn(x, g, w_gate, w_up, w_down, final_g=None):
    t, d = x.shape
    f = w_gate.shape[1]
    tm = min(512, t)
    tf = 512
    n_main, f_tail = divmod(f, tf)
    assert f_tail % V7X_LANES == 0 and n_main >= 1
    has_tail = int(f_tail > 0)
    f_main = n_main * tf
    wg, wu, wd = w_gate.astype(BF16), w_up.astype(BF16), w_down.astype(BF16)
    final_norm = final_g is not None
    row = pl.BlockSpec((tm, d), lambda i, j: (i, 0))
    vec = pl.BlockSpec((1, d), lambda i, j: (0, 0))
    slab = lambda i, j: (0, jnp.minimum(j, n_main - 1))
    in_specs = [row, vec,
                pl.BlockSpec((d, tf), slab),
                pl.BlockSpec((d, tf), slab),
                pl.BlockSpec((tf, d), lambda i, j: (jnp.minimum(j, n_main - 1), 0))]
    args = [x, g.reshape(1, d), wg, wu, wd]
    if has_tail:
        in_specs += [_resident((d, f_tail), lambda i, j: (0, 0)),
                     _resident((d, f_tail), lambda i, j: (0, 0)),
                     _resident((f_tail, d), lambda i, j: (0, 0))]
        args += [wg[:, f_main:], wu[:, f_main:], wd[f_main:]]
    if final_norm:
        in_specs.append(vec)
        args.append(final_g.reshape(1, d))
    return pl.pallas_call(
        functools.partial(_ffn_body, n_main=n_main, has_tail=has_tail, final_norm=final_norm),
        grid=(t // tm, n_main + has_tail),
        in_specs=in_specs,
        out_specs=row,
        out_shape=jax.ShapeDtypeStruct((t, d), F32),
        scratch_shapes=[pltpu.VMEM((tm, d), BF16)],
        compiler_params=_params(("parallel", "arbitrary"), 48),
        name="ffn_final" if final_norm else "ffn",
    )(*args)


def _rope(x, cos2, sin2):
    return x * cos2 + pltpu.roll(x, HEAD_DIM // 2, 1) * sin2


def _in_proj_body(x_ref, g_ref, w_ref, wgate_ref, cos_ref, sin_ref, blk_ref,
                  u_ref, qraw_ref, qrot_ref, kc_ref, vc_ref, k_ref, v_ref, gate_ref, kv_scr):
    h = _rms(x_ref[...], g_ref[...]).astype(BF16)
    cos2 = cos_ref[...]
    sin2 = sin_ref[...]
    scale = HEAD_DIM ** -0.5

    def proj(c0, width):
        return jnp.dot(h, w_ref[:, c0:c0 + width], preferred_element_type=F32)

    glu = proj(0, 2 * CONV_CH)
    u_ref[...] = glu[:, :CONV_CH] * jax.nn.sigmoid(glu[:, CONV_CH:])

    def head(cols, i):
        return cols[:, i * HEAD_DIM:(i + 1) * HEAD_DIM]

    c0 = 2 * CONV_CH
    for gk in range(N_KV):
        cols = proj(c0, GROUP * HEAD_DIM) * (scale * LOG2_E)
        for r in range(GROUP):
            qh = head(cols, r)
            qraw_ref[0, gk * GROUP + r] = qh.astype(BF16)
            qrot_ref[0, gk * GROUP + r] = _rope(qh, cos2, sin2).astype(BF16)
        c0 += GROUP * HEAD_DIM
    cols = proj(c0, 2 * KV_DIM)
    n_unit = kv_scr.shape[1] // CMP_STRIDE
    for ref, first in ((kc_ref, 0), (vc_ref, N_KV)):
        for gk in range(N_KV):
            kv_scr[first + gk] = head(cols, first + gk)
            for slot in range(CMP_STRIDE):
                ref[0, gk, :, slot * HEAD_DIM:(slot + 1) * HEAD_DIM] = kv_scr[
                    first + gk, pl.ds(slot, n_unit, stride=CMP_STRIDE), :]
    c0 += 2 * KV_DIM
    lo, hi = slice(0, HEAD_DIM), slice(HEAD_DIM, 2 * HEAD_DIM)
    for branch in (SLC, WIN):
        cols = proj(c0, 2 * KV_DIM)
        for gk in range(N_KV):
            k_ref[0, gk, branch, :, lo] = _rope(head(cols, gk), cos2, sin2).astype(BF16)
            v_ref[0, gk, branch, :, lo] = head(cols, N_KV + gk).astype(BF16)
        c0 += 2 * KV_DIM
    cols = jax.nn.sigmoid(jnp.dot(h, wgate_ref[...], preferred_element_type=F32))
    for gk in range(N_KV):
        k_ref[0, gk, SLC, :, hi] = blk_ref[...]
        k_ref[0, gk, WIN, :, hi] = jnp.zeros(blk_ref.shape, BF16)
        v_ref[0, gk, SLC, :, hi] = jnp.ones(blk_ref.shape, BF16)
        v_ref[0, gk, WIN, :, hi] = jnp.ones(blk_ref.shape, BF16)
        gate_ref[0, gk] = head(cols, gk)


def _in_proj(x, g, w_in, batch, seq):
    t, d = x.shape
    tm = min(256, seq)
    n_s = seq // tm
    main = 2 * CONV_CH + Q_DIM + 6 * KV_DIM
    gate_w = w_in[:, main:].reshape(d, N_KV, GROUP * 3)
    gate_w = jnp.pad(gate_w, ((0, 0), (0, 0), (0, V7X_LANES - GROUP * 3))).reshape(d, N_KV * V7X_LANES)
    gate_w = gate_w.astype(BF16)
    w = w_in[:, :main].astype(BF16)

    inv = jnp.power(ROPE_THETA, -jnp.arange(0, HEAD_DIM, 2, dtype=F32) / HEAD_DIM)
    ang = jnp.arange(seq, dtype=F32)[:, None] * inv[None, :]
    cos2 = jnp.concatenate([jnp.cos(ang), jnp.cos(ang)], axis=1)
    sin2 = jnp.concatenate([-jnp.sin(ang), jnp.sin(ang)], axis=1)

    key_blk = np.arange(seq)[:, None] // SLC_BLOCK == np.arange(seq // SLC_BLOCK)[None, :]
    blk_mask = jnp.asarray(np.where(key_blk, -(2.0 ** 100), 0.0), BF16)

    def heads(n, dtype):
        return (jax.ShapeDtypeStruct((batch, n, seq, HEAD_DIM), dtype),
                pl.BlockSpec((1, n, tm, HEAD_DIM), lambda i: (i // n_s, 0, i % n_s, 0)))

    stacked = (jax.ShapeDtypeStruct((batch, N_KV, 2, seq, 2 * HEAD_DIM), BF16),
               pl.BlockSpec((1, N_KV, 2, tm, 2 * HEAD_DIM), lambda i: (i // n_s, 0, 0, i % n_s, 0)))
    unit = CMP_STRIDE * HEAD_DIM
    units = (jax.ShapeDtypeStruct((batch, N_KV, seq // CMP_STRIDE, unit), F32),
             pl.BlockSpec((1, N_KV, tm // CMP_STRIDE, unit), lambda i: (i // n_s, 0, i % n_s, 0)))
    outs = [(jax.ShapeDtypeStruct((t, CONV_CH), F32), pl.BlockSpec((tm, CONV_CH), lambda i: (i, 0))),
            heads(N_HEADS, BF16), heads(N_HEADS, BF16),
            units, units,
            stacked, stacked,
            heads(N_KV, F32)]
    table = pl.BlockSpec((tm, HEAD_DIM), lambda i: (i % n_s, 0))
    return pl.pallas_call(
        _in_proj_body,
        grid=(t // tm,),
        in_specs=[pl.BlockSpec((tm, d), lambda i: (i, 0)),
                  pl.BlockSpec((1, d), lambda i: (0, 0)),
                  _resident((d, main), lambda i: (0, 0)),
                  _resident(gate_w.shape, lambda i: (0, 0)),
                  table, table, table],
        out_specs=[o[1] for o in outs],
        out_shape=[o[0] for o in outs],
        scratch_shapes=[pltpu.VMEM((2 * N_KV, tm, HEAD_DIM), F32)],
        compiler_params=_params(("parallel",), 48),
        name="in_proj",
    )(x, g.reshape(1, d), w, gate_w, cos2, sin2, blk_mask)


CONV_ROWS = 512


def _conv_body(u_ref, halo_ref, dw_ref, db_ref, lg_ref, lb_ref, pw_ref, og_ref, o_ref, ext_scr, y_scr, *, ts):
    i = pl.program_id(1)
    ext_scr[0, 0:CONV_HALO, :] = jnp.where(i == 0, 0.0, halo_ref[0])
    ext_scr[0, CONV_HALO:CONV_HALO + ts, :] = u_ref[0]
    moved = CONV_HALO + ts - V7X_SUBLANES
    for s in range(1, V7X_SUBLANES):
        ext_scr[s, 0:moved, :] = ext_scr[0, s:s + moved, :]
    first = CONV_HALO - (CONV_K - 1)
    for c in range(ts // CONV_ROWS):
        r0 = c * CONV_ROWS
        acc = jnp.broadcast_to(db_ref[...], (CONV_ROWS, CONV_CH))
        for k in range(CONV_K):
            s, base = (first + k) % V7X_SUBLANES, (first + k) // V7X_SUBLANES * V7X_SUBLANES
            acc = acc + dw_ref[k:k + 1, :] * ext_scr[s, r0 + base:r0 + base + CONV_ROWS, :]
        mu = jnp.mean(acc, axis=-1, keepdims=True)
        xc = acc - mu
        var = jnp.mean(xc * xc, axis=-1, keepdims=True)
        y = xc * lax.rsqrt(var + EPS) * lg_ref[...] + lb_ref[...]
        y_scr[r0:r0 + CONV_ROWS, :] = (y * jax.nn.sigmoid(y)).astype(BF16)
    z = jnp.dot(y_scr[...], pw_ref[...], preferred_element_type=F32)
    o_ref[0] = _rms(z, og_ref[...]).astype(BF16)


def _conv(u, dw_w, dw_b, ln_g, ln_b, pw_w, out_g):
    b, s, c = u.shape
    ts = min(512, s)
    per = ts // CONV_HALO
    vec = pl.BlockSpec((1, c), lambda bi, i: (0, 0))
    return pl.pallas_call(
        functools.partial(_conv_body, ts=ts),
        grid=(b, s // ts),
        in_specs=[pl.BlockSpec((1, ts, c), lambda bi, i: (bi, i, 0)),
                  pl.BlockSpec((1, CONV_HALO, c), lambda bi, i: (bi, jnp.maximum(i * per - 1, 0), 0)),
                  pl.BlockSpec((CONV_K, c), lambda bi, i: (0, 0)),
                  vec, vec, vec,
                  pl.BlockSpec((c, c), lambda bi, i: (0, 0)),
                  vec],
        out_specs=pl.BlockSpec((1, ts, c), lambda bi, i: (bi, i, 0)),
        out_shape=jax.ShapeDtypeStruct((b, s, c), BF16),
        scratch_shapes=[pltpu.VMEM((V7X_SUBLANES, CONV_HALO + ts, c), F32), pltpu.VMEM((ts, c), BF16)],
        compiler_params=_params(("parallel", "parallel"), 32),
        name="conv",
    )(u, u, dw_w, dw_b.reshape(1, c), ln_g.reshape(1, c), ln_b.reshape(1, c), pw_w.astype(BF16),
      out_g.reshape(1, c))


def _compress_one(u_ref, pos_ref, w1_ref, w2_ref, o_ref):
    u = u_ref[0, 0]
    half = u.shape[1]
    nu = u.shape[0]
    top = jnp.dot((u + pos_ref[0:1, :]).astype(BF16), w1_ref[0:half, :], preferred_element_type=F32)
    bot = jnp.dot((u + pos_ref[1:2, :]).astype(BF16), w1_ref[half:2 * half, :], preferred_element_type=F32)
    hid = top + pltpu.roll(bot, nu - 1, 0)
    hid = hid * jax.nn.sigmoid(hid)
    o_ref[0, 0] = jnp.dot(hid.astype(BF16), w2_ref[...], preferred_element_type=F32).astype(BF16)


def _compress_body(uk_ref, uv_ref, pk_ref, pv_ref, kw1_ref, kw2_ref, vw1_ref, vw2_ref, ok_ref, ov_ref):
    _compress_one(uk_ref, pk_ref, kw1_ref, kw2_ref, ok_ref)
    _compress_one(uv_ref, pv_ref, vw1_ref, vw2_ref, ov_ref)


def _compress(kc, vc, pos_k, pos_v, kw1, kw2, vw1, vw2):
    b, g, nu, unit = kc.shape
    dh = unit // CMP_STRIDE
    pos = lambda p: p.reshape(CMP_BLOCK // CMP_STRIDE, unit)
    u_spec = pl.BlockSpec((1, 1, nu, unit), lambda bi, gi: (bi, gi, 0, 0))
    full = lambda shape: pl.BlockSpec(shape, lambda bi, gi: (0,) * len(shape))
    o_spec = pl.BlockSpec((1, 1, nu, dh), lambda bi, gi: (bi, gi, 0, 0))
    o_shape = jax.ShapeDtypeStruct((b, g, nu, dh), BF16)
    return pl.pallas_call(
        _compress_body,
        grid=(b, g),
        in_specs=[u_spec, u_spec, full((2, unit)), full((2, unit)),
                  full((CMP_BLOCK * dh, CMP_HIDDEN)), full((CMP_HIDDEN, dh)),
                  full((CMP_BLOCK * dh, CMP_HIDDEN)), full((CMP_HIDDEN, dh))],
        out_specs=[o_spec, o_spec],
        out_shape=[o_shape, o_shape],
        compiler_params=_params(("parallel", "parallel"), 40),
        name="compress",
    )(kc, vc, pos(pos_k), pos(pos_v),
      kw1.astype(BF16), kw2.astype(BF16), vw1.astype(BF16), vw2.astype(BF16))


def _cmp_select_prefix(q_ref, kc_ref, vc_ref, ct_ref, ocmp_ref, sel_ref, *, tq, nc, nb):
    for sub in range(tq // CMP_SUB):
        _cmp_select_sub(q_ref, kc_ref, vc_ref, ct_ref, ocmp_ref, sel_ref,
                        row0=sub * CMP_SUB, q0=pl.program_id(2) * tq + sub * CMP_SUB, tq=CMP_SUB, nc=nc, nb=nb)


def _cmp_select_sub(q_ref, kc_ref, vc_ref, ct_ref, ocmp_ref, sel_ref, *, row0, q0, tq, nc, nb):
    n_slc = ct_ref.shape[0]
    q = q_ref[0, :, row0:row0 + tq, :].reshape(GROUP * tq, HEAD_DIM)
    s = lax.dot_general(q, kc_ref[0, 0, 0:nc, :], _NT, preferred_element_type=F32).reshape(GROUP, tq, nc)
    t = q0 + lax.broadcasted_iota(jnp.int32, (tq, nc), 0)
    cmp_end = lax.broadcasted_iota(jnp.int32, (tq, nc), 1) * CMP_STRIDE + (CMP_BLOCK - 1)
    s = s + jnp.where(cmp_end <= t, 0.0, MASKED)[None]
    m = jnp.max(s, axis=-1, keepdims=True)
    m = jnp.where(m > 0.5 * MASKED, m, 0.0)
    e = jnp.exp2(s - m)
    p = e * (1.0 / jnp.maximum(jnp.sum(e, axis=-1, keepdims=True), 1e-30))
    o = jnp.dot(p.reshape(GROUP * tq, nc).astype(BF16), vc_ref[0, 0, 0:nc, :], preferred_element_type=F32)
    ocmp_ref[0, :, row0:row0 + tq, :] = o.reshape(GROUP, tq, HEAD_DIM)

    psum = p[0] + p[1] + p[2] + p[3]
    hi = psum.astype(BF16)
    r1 = psum - hi.astype(F32)
    mid = r1.astype(BF16)
    lo = (r1 - mid.astype(F32)).astype(BF16)
    ct = ct_ref[0:nb, 0:nc]
    imp = (lax.dot_general(ct, hi, _NT, preferred_element_type=F32)
           + lax.dot_general(ct, mid, _NT, preferred_element_type=F32)
           + lax.dot_general(ct, lo, _NT, preferred_element_type=F32))

    blk = lax.broadcasted_iota(jnp.int32, (nb, tq), 0)
    jt = (q0 + lax.broadcasted_iota(jnp.int32, (nb, tq), 1)) // SLC_BLOCK
    forced = (blk == 0) | (blk == jt) | (blk == jt - 1)
    candidate = (blk >= 1) & (blk < jt - 1)
    val = jnp.where(candidate, imp, -1.0)
    blk_f = blk.astype(F32)
    for _ in range(N_SELECT - N_FORCED):
        best = jnp.max(val, axis=0, keepdims=True)
        first = jnp.min(jnp.where(val == best, blk_f, float(n_slc)), axis=0, keepdims=True)
        val = jnp.where(blk_f == first, -1.0, val)
    unsel = jnp.where(forced | (candidate & (val < 0.0)), 0.0, 1.0)
    if nb < n_slc:
        unsel = jnp.concatenate([unsel, jnp.ones((n_slc - nb, tq), F32)], axis=0)
    sel_ref[0, 0, row0:row0 + tq, :] = unsel.T.astype(BF16)


def _cmp_select_body(q_ref, kc_ref, vc_ref, ct_ref, ocmp_ref, sel_ref, *, tq):
    nu = kc_ref.shape[2]
    n_slc = ct_ref.shape[0]
    per_slc = nu // n_slc
    visible = (pl.program_id(2) * tq + tq - CMP_BLOCK) // CMP_STRIDE + 1
    n_prefix = nu // V7X_LANES
    need = jnp.clip((visible + V7X_LANES - 1) // V7X_LANES, 1, n_prefix)
    for v in range(1, n_prefix + 1):
        nc = v * V7X_LANES
        pl.when(need == v)(functools.partial(
            _cmp_select_prefix, q_ref, kc_ref, vc_ref, ct_ref, ocmp_ref, sel_ref, tq=tq, nc=nc, nb=nc // per_slc))


def _cmp_to_slc_t(nu, n_slc):
    per_slc = SLC_BLOCK // CMP_STRIDE
    c = np.arange(nu)[None, :]
    j = np.arange(n_slc)[:, None]
    m = np.zeros((n_slc, nu), np.float32)
    for unit in range(CMP_BLOCK // CMP_STRIDE):
        m += ((c + unit) // per_slc == j)
    m[:, nu - 1] = 0.0
    return jnp.asarray(m, BF16)


def _cmp_select(q_raw, k_cmp, v_cmp):
    b, _, s, dh = q_raw.shape
    nu = k_cmp.shape[2]
    n_slc = s // SLC_BLOCK
    tq = 4 * CMP_SUB
    for q_end in range(tq, s + 1, tq):
        prefix = -(-((q_end - CMP_BLOCK) // CMP_STRIDE + 1) // V7X_LANES) * V7X_LANES
        assert prefix * n_slc // nu >= (q_end - 1) // SLC_BLOCK + 1
    kv_spec = pl.BlockSpec((1, 1, nu, dh), lambda bi, gi, qi: (bi, gi, 0, 0))
    return pl.pallas_call(
        functools.partial(_cmp_select_body, tq=tq),
        grid=(b, N_KV, s // tq),
        in_specs=[pl.BlockSpec((1, GROUP, tq, dh), lambda bi, gi, qi: (bi, gi, qi, 0)),
                  kv_spec, kv_spec,
                  pl.BlockSpec((n_slc, nu), lambda bi, gi, qi: (0, 0))],
        out_specs=[pl.BlockSpec((1, GROUP, tq, dh), lambda bi, gi, qi: (bi, gi, qi, 0)),
                   pl.BlockSpec((1, 1, tq, n_slc), lambda bi, gi, qi: (bi, gi, qi, 0))],
        out_shape=[jax.ShapeDtypeStruct((b, N_HEADS, s, dh), F32),
                   jax.ShapeDtypeStruct((b, N_KV, s, n_slc), BF16)],
        compiler_params=_params(("parallel", "parallel", "parallel"), 32),
        name="cmp_select",
    )(q_raw, k_cmp, v_cmp, _cmp_to_slc_t(nu, n_slc))


LOWER, UPPER = 0, 1


def _attend_body(q_ref, k_ref, v_ref, tri_ref, unsel_ref, ocmp_ref, g_ref, o_ref,
                 qa_scr, s_scr, m_scr, acc_scr, *, tq, tk):
    q0 = pl.program_id(2) * tq
    rows = GROUP * tq

    unsel = unsel_ref[0, 0]
    for r in range(GROUP):
        qa_scr[r * tq:(r + 1) * tq, 0:HEAD_DIM] = q_ref[0, r]
        qa_scr[r * tq:(r + 1) * tq, HEAD_DIM:2 * HEAD_DIM] = unsel
    m_scr[...] = jnp.full(m_scr.shape, MASKED, F32)
    acc_scr[...] = jnp.zeros(acc_scr.shape, F32)

    last = q0 // tk
    n_win = jnp.minimum(last + 1, WINDOW // tk + 1)
    n_tiles = last + 1 + n_win

    def tile(i):
        branch = (i > last).astype(jnp.int32)
        kt = i - branch * n_win
        return branch, kt, pl.multiple_of(kt * tk, tk)

    def scores(i):
        branch, _, k0 = tile(i)
        s_scr[...] = lax.dot_general(qa_scr[...], k_ref[0, 0, branch, pl.ds(k0, tk), :], _NT,
                                     preferred_element_type=F32)

    def softmax_pv(i, boundary):
        branch, kt, k0 = tile(i)
        if boundary:
            kind = jnp.where(kt == last, LOWER, UPPER)
            s = (s_scr[...].reshape(GROUP, tq, tk) + tri_ref[kind][None]).reshape(rows, tk)
        else:
            s = s_scr[...]
        m_old = m_scr[branch]
        m_new = jnp.maximum(m_old, jnp.max(s, axis=-1, keepdims=True))
        p = jnp.exp2(s - jnp.concatenate([m_new] * (tk // V7X_LANES), axis=1)).astype(BF16)
        pv = jnp.dot(p, v_ref[0, 0, branch, pl.ds(k0, tk), :], preferred_element_type=F32)
        alpha = jnp.exp2(m_old - m_new)
        acc_scr[branch] = jnp.concatenate([alpha] * (2 * HEAD_DIM // V7X_LANES), axis=1) * acc_scr[branch] + pv
        m_scr[branch] = m_new

    scores(0)

    def interior_step(i, carry):
        softmax_pv(i, False)
        scores(i + 1)
        return carry

    def boundary_step(i, carry):
        softmax_pv(i, True)
        scores(i + 1)
        return carry

    lax.fori_loop(0, last, interior_step, 0)
    lax.fori_loop(last, n_tiles - 1, boundary_step, 0)
    softmax_pv(n_tiles - 1, True)

    def normalized(branch):
        acc = acc_scr[branch]
        return acc[:, 0:HEAD_DIM] * (1.0 / acc[:, HEAD_DIM:2 * HEAD_DIM])

    o_slc = normalized(SLC)
    o_win = normalized(WIN)
    gate = g_ref[0, 0]
    for r in range(GROUP):
        o_ref[0, :, r * HEAD_DIM:(r + 1) * HEAD_DIM] = (
            gate[:, 3 * r:3 * r + 1] * ocmp_ref[0, r]
            + gate[:, 3 * r + 1:3 * r + 2] * o_slc[r * tq:(r + 1) * tq]
            + gate[:, 3 * r + 2:3 * r + 3] * o_win[r * tq:(r + 1) * tq])


def _attend(q_rot, k_all, v_all, unsel, o_cmp, gates):
    b, _, s, dh = q_rot.shape
    n_slc = unsel.shape[3]
    assert n_slc == dh, "the unselected one-hot fills the second half of the augmented contraction"
    tq = tk = WINDOW
    row, col = np.arange(tq)[:, None], np.arange(tk)[None, :]
    tri = jnp.asarray(np.stack([np.where(col <= row, 0.0, MASKED),
                                np.where(col > row, 0.0, MASKED)]), F32)
    q_spec = pl.BlockSpec((1, GROUP, tq, dh), lambda bi, gi, qi: (bi, gi, qi, 0))
    kv_spec = _resident((1, 1, 2, s, 2 * dh), lambda bi, gi, qi: (bi, gi, 0, 0, 0))
    row_spec = lambda w: pl.BlockSpec((1, 1, tq, w), lambda bi, gi, qi: (bi, gi, qi, 0))
    rows = GROUP * tq
    return pl.pallas_call(
        functools.partial(_attend_body, tq=tq, tk=tk),
        grid=(b, N_KV, s // tq),
        in_specs=[q_spec, kv_spec, kv_spec, _resident(tri.shape, lambda bi, gi, qi: (0, 0, 0)),
                  row_spec(n_slc), q_spec, row_spec(V7X_LANES)],
        out_specs=pl.BlockSpec((1, tq, GROUP * dh), lambda bi, gi, qi: (bi, qi, gi)),
        out_shape=jax.ShapeDtypeStruct((b, s, Q_DIM), F32),
        scratch_shapes=[pltpu.VMEM((rows, 2 * dh), BF16),
                        pltpu.VMEM((rows, tk), F32),
                        pltpu.VMEM((2, rows, V7X_LANES), F32),
                        pltpu.VMEM((2, rows, 2 * dh), F32)],
        compiler_params=_params(("parallel", "parallel", "arbitrary"), 48),
        name="attend",
    )(q_rot, k_all, v_all, tri, unsel, o_cmp, gates)


def _out_proj_body(cn_ref, a_ref, x_ref, gn_ref, wc_ref, wa_ref, o_ref):
    an = _rms(a_ref[...], gn_ref[...]).astype(BF16)
    y = (jnp.dot(cn_ref[...], wc_ref[...], preferred_element_type=F32)
         + jnp.dot(an, wa_ref[...], preferred_element_type=F32))
    o_ref[...] = x_ref[...] + y


def _out_proj(conv_n, attn, x, nsa_g, w_out):
    t, d = x.shape
    tm = min(512, t)
    wc = w_out[:CONV_CH].astype(BF16)
    wa = w_out[CONV_CH:].astype(BF16)
    return pl.pallas_call(
        _out_proj_body,
        grid=(t // tm,),
        in_specs=[pl.BlockSpec((tm, CONV_CH), lambda i: (i, 0)),
                  pl.BlockSpec((tm, Q_DIM), lambda i: (i, 0)),
                  pl.BlockSpec((tm, d), lambda i: (i, 0)),
                  pl.BlockSpec((1, Q_DIM), lambda i: (0, 0)),
                  _resident((CONV_CH, d), lambda i: (0, 0)),
                  _resident((Q_DIM, d), lambda i: (0, 0))],
        out_specs=pl.BlockSpec((tm, d), lambda i: (i, 0)),
        out_shape=jax.ShapeDtypeStruct((t, d), F32),
        compiler_params=_params(("parallel",), 40),
        name="out_proj",
    )(conv_n, attn, x, nsa_g.reshape(1, Q_DIM), wc, wa)


def kernel(x, ffn1_norm, ffn1_w_gate, ffn1_w_up, ffn1_w_down, mix_norm, w_in, cmp_pos_k, cmp_pos_v, cmp_k_w1, cmp_k_w2, cmp_v_w1, cmp_v_w2, conv_dw_w, conv_dw_b, conv_ln_g, conv_ln_b, conv_pw_w, out_norm_conv, out_norm_nsa, w_out, ffn2_norm, ffn2_w_gate, ffn2_w_up, ffn2_w_down, final_norm):
    b, s, d = x.shape
    assert s % (SLC_BLOCK * V7X_LANES) == 0, "selection blocks must fill whole 128-lane rows"
    depth = ffn1_norm.shape[0]
    y = x.reshape(b * s, d)
    for l in range(depth):
        y = _ffn(y, ffn1_norm[l], ffn1_w_gate[l], ffn1_w_up[l], ffn1_w_down[l])
        u, q_raw, q_rot, kc, vc, k_all, v_all, gates = _in_proj(y, mix_norm[l], w_in[l], b, s)
        conv_n = _conv(u.reshape(b, s, CONV_CH), conv_dw_w[l], conv_dw_b[l], conv_ln_g[l], conv_ln_b[l],
                       conv_pw_w[l], out_norm_conv[l])
        k_cmp, v_cmp = _compress(kc, vc, cmp_pos_k[l], cmp_pos_v[l],
                                 cmp_k_w1[l], cmp_k_w2[l], cmp_v_w1[l], cmp_v_w2[l])
        o_cmp, unsel = _cmp_select(q_raw, k_cmp, v_cmp)
        attn = _attend(q_rot, k_all, v_all, unsel, o_cmp, gates)
        y = _out_proj(conv_n.reshape(b * s, CONV_CH), attn.reshape(b * s, Q_DIM), y, out_norm_nsa[l], w_out[l])
        y = _ffn(y, ffn2_norm[l], ffn2_w_gate[l], ffn2_w_up[l], ffn2_w_down[l],
                 final_g=final_norm if l == depth - 1 else None)
    return y.reshape(b, s, d)
```

```python
import functools

import numpy as np
import jax
import jax.numpy as jnp
from jax import lax
from jax.experimental import pallas as pl
from jax.experimental.pallas import tpu as pltpu

F32 = jnp.float32
BF16 = jnp.bfloat16

V7X_LANES = 128
V7X_SUBLANES = 8

CONV_CH = 512
N_HEADS = 12
HEAD_DIM = 128
N_KV = 3
GROUP = N_HEADS // N_KV
CONV_K = 31
CMP_BLOCK = 32
CMP_STRIDE = 16
CMP_HIDDEN = 256
SLC_BLOCK = 64
N_SELECT = 16
N_FORCED = 3
WINDOW = 512
ROPE_THETA = 10000.0
EPS = 1e-6
MASKED = -1e30
LOG2_E = 1.4426950408889634
SLC, WIN = 0, 1

KV_DIM = N_KV * HEAD_DIM
Q_DIM = N_HEADS * HEAD_DIM
CMP_SUB = 1024
CONV_HALO = 32

_NT = (((1,), (1,)), ((), ()))


def _rms(x, g):
    return x * lax.rsqrt(jnp.mean(x * x, axis=-1, keepdims=True) + EPS) * g


def _params(semantics, vmem_mib):
    return pltpu.CompilerParams(dimension_semantics=semantics, vmem_limit_bytes=vmem_mib * 2 ** 20)


def _resident(shape, index_map):
    return pl.BlockSpec(shape, index_map, pipeline_mode=pl.Buffered(1))


def _ffn_body(*refs, n_main, has_tail, final_norm):
    refs = list(refs)
    x_ref, g_ref, wg_ref, wu_ref, wd_ref = refs[:5]
    del refs[:5]
    if has_tail:
        tail_refs = refs[:3]
        del refs[:3]
    if final_norm:
        fg_ref = refs.pop(0)
    o_ref, h_scr = refs
    j = pl.program_id(1)

    @pl.when(j == 0)
    def _():
        x = x_ref[...]
        h_scr[...] = _rms(x, g_ref[...]).astype(BF16)
        o_ref[...] = x

    def hidden_slab(wg, wu, wd):
        h = h_scr[...]
        a = jnp.dot(h, wg[...], preferred_element_type=F32)
        b = jnp.dot(h, wu[...], preferred_element_type=F32)
        z = (a * jax.nn.sigmoid(a) * b).astype(BF16)
        o_ref[...] += 0.5 * jnp.dot(z, wd[...], preferred_element_type=F32)

    if has_tail:
        pl.when(j < n_main)(functools.partial(hidden_slab, wg_ref, wu_ref, wd_ref))
        pl.when(j == n_main)(functools.partial(hidden_slab, *tail_refs))
    else:
        hidden_slab(wg_ref, wu_ref, wd_ref)

    if final_norm:
        @pl.when(j == n_main + has_tail - 1)
        def _():
            o_ref[...] = _rms(o_ref[...], fg_ref[...])


def _ffn(x, g, w_gate, w_up, w_down, final_g=None):
    t, d = x.shape
    f = w_gate.shape[1]
    tm = min(512, t)
    tf = 512
    n_main, f_tail = divmod(f, tf)
    assert f_tail % V7X_LANES == 0 and n_main >= 1
    has_tail = int(f_tail > 0)
    f_main = n_main * tf
    wg, wu, wd = w_gate.astype(BF16), w_up.astype(BF16), w_down.astype(BF16)
    final_norm = final_g is not None
    row = pl.BlockSpec((tm, d), lambda i, j: (i, 0))
    vec = pl.BlockSpec((1, d), lambda i, j: (0, 0))
    slab = lambda i, j: (0, jnp.minimum(j, n_main - 1))
    in_specs = [row, vec,
                pl.BlockSpec((d, tf), slab),
                pl.BlockSpec((d, tf), slab),
                pl.BlockSpec((tf, d), lambda i, j: (jnp.minimum(j, n_main - 1), 0))]
    args = [x, g.reshape(1, d), wg, wu, wd]
    if has_tail:
        in_specs += [_resident((d, f_tail), lambda i, j: (0, 0)),
                     _resident((d, f_tail), lambda i, j: (0, 0)),
                     _resident((f_tail, d), lambda i, j: (0, 0))]
        args += [wg[:, f_main:], wu[:, f_main:], wd[f_main:]]
    if final_norm:
        in_specs.append(vec)
        args.append(final_g.reshape(1, d))
    return pl.pallas_call(
        functools.partial(_ffn_body, n_main=n_main, has_tail=has_tail, final_norm=final_norm),
        grid=(t // tm, n_main + has_tail),
        in_specs=in_specs,
        out_specs=row,
        out_shape=jax.ShapeDtypeStruct((t, d), F32),
        scratch_shapes=[pltpu.VMEM((tm, d), BF16)],
        compiler_params=_params(("parallel", "arbitrary"), 48),
        name="ffn_final" if final_norm else "ffn",
    )(*args)


def _rope(x, cos2, sin2):
    return x * cos2 + pltpu.roll(x, HEAD_DIM // 2, 1) * sin2


def _in_proj_body(x_ref, g_ref, w_ref, wgate_ref, cos_ref, sin_ref, blk_ref,
                  u_ref, qraw_ref, qrot_ref, kc_ref, vc_ref, k_ref, v_ref, gate_ref, kv_scr):
    h = _rms(x_ref[...], g_ref[...]).astype(BF16)
    cos2 = cos_ref[...]
    sin2 = sin_ref[...]
    scale = HEAD_DIM ** -0.5

    def proj(c0, width):
        return jnp.dot(h, w_ref[:, c0:c0 + width], preferred_element_type=F32)

    glu = proj(0, 2 * CONV_CH)
    u_ref[...] = glu[:, :CONV_CH] * jax.nn.sigmoid(glu[:, CONV_CH:])

    def head(cols, i):
        return cols[:, i * HEAD_DIM:(i + 1) * HEAD_DIM]

    c0 = 2 * CONV_CH
    for gk in range(N_KV):
        cols = proj(c0, GROUP * HEAD_DIM) * (scale * LOG2_E)
        for r in range(GROUP):
            qh = head(cols, r)
            qraw_ref[0, gk * GROUP + r] = qh.astype(BF16)
            qrot_ref[0, gk * GROUP + r] = _rope(qh, cos2, sin2).astype(BF16)
        c0 += GROUP * HEAD_DIM
    cols = proj(c0, 2 * KV_DIM)
    n_unit = kv_scr.shape[1] // CMP_STRIDE
    for ref, first in ((kc_ref, 0), (vc_ref, N_KV)):
        for gk in range(N_KV):
            kv_scr[first + gk] = head(cols, first + gk)
            for slot in range(CMP_STRIDE):
                ref[0, gk, :, slot * HEAD_DIM:(slot + 1) * HEAD_DIM] = kv_scr[
                    first + gk, pl.ds(slot, n_unit, stride=CMP_STRIDE), :]
    c0 += 2 * KV_DIM
    lo, hi = slice(0, HEAD_DIM), slice(HEAD_DIM, 2 * HEAD_DIM)
    for branch in (SLC, WIN):
        cols = proj(c0, 2 * KV_DIM)
        for gk in range(N_KV):
            k_ref[0, gk, branch, :, lo] = _rope(head(cols, gk), cos2, sin2).astype(BF16)
            v_ref[0, gk, branch, :, lo] = head(cols, N_KV + gk).astype(BF16)
        c0 += 2 * KV_DIM
    cols = jax.nn.sigmoid(jnp.dot(h, wgate_ref[...], preferred_element_type=F32))
    for gk in range(N_KV):
        k_ref[0, gk, SLC, :, hi] = blk_ref[...]
        k_ref[0, gk, WIN, :, hi] = jnp.zeros(blk_ref.shape, BF16)
        v_ref[0, gk, SLC, :, hi] = jnp.ones(blk_ref.shape, BF16)
        v_ref[0, gk, WIN, :, hi] = jnp.ones(blk_ref.shape, BF16)
        gate_ref[0, gk] = head(cols, gk)


def _in_proj(x, g, w_in, batch, seq):
    t, d = x.shape
    tm = min(256, seq)
    n_s = seq // tm
    main = 2 * CONV_CH + Q_DIM + 6 * KV_DIM
    gate_w = w_in[:, main:].reshape(d, N_KV, GROUP * 3)
    gate_w = jnp.pad(gate_w, ((0, 0), (0, 0), (0, V7X_LANES - GROUP * 3))).reshape(d, N_KV * V7X_LANES)
    gate_w = gate_w.astype(BF16)
    w = w_in[:, :main].astype(BF16)

    inv = jnp.power(ROPE_THETA, -jnp.arange(0, HEAD_DIM, 2, dtype=F32) / HEAD_DIM)
    ang = jnp.arange(seq, dtype=F32)[:, None] * inv[None, :]
    cos2 = jnp.concatenate([jnp.cos(ang), jnp.cos(ang)], axis=1)
    sin2 = jnp.concatenate([-jnp.sin(ang), jnp.sin(ang)], axis=1)

    key_blk = np.arange(seq)[:, None] // SLC_BLOCK == np.arange(seq // SLC_BLOCK)[None, :]
    blk_mask = jnp.asarray(np.where(key_blk, -(2.0 ** 100), 0.0), BF16)

    def heads(n, dtype):
        return (jax.ShapeDtypeStruct((batch, n, seq, HEAD_DIM), dtype),
                pl.BlockSpec((1, n, tm, HEAD_DIM), lambda i: (i // n_s, 0, i % n_s, 0)))

    stacked = (jax.ShapeDtypeStruct((batch, N_KV, 2, seq, 2 * HEAD_DIM), BF16),
               pl.BlockSpec((1, N_KV, 2, tm, 2 * HEAD_DIM), lambda i: (i // n_s, 0, 0, i % n_s, 0)))
    unit = CMP_STRIDE * HEAD_DIM
    units = (jax.ShapeDtypeStruct((batch, N_KV, seq // CMP_STRIDE, unit), F32),
             pl.BlockSpec((1, N_KV, tm // CMP_STRIDE, unit), lambda i: (i // n_s, 0, i % n_s, 0)))
    outs = [(jax.ShapeDtypeStruct((t, CONV_CH), F32), pl.BlockSpec((tm, CONV_CH), lambda i: (i, 0))),
            heads(N_HEADS, BF16), heads(N_HEADS, BF16),
            units, units,
            stacked, stacked,
            heads(N_KV, F32)]
    table = pl.BlockSpec((tm, HEAD_DIM), lambda i: (i % n_s, 0))
    return pl.pallas_call(
        _in_proj_body,
        grid=(t // tm,),
        in_specs=[pl.BlockSpec((tm, d), lambda i: (i, 0)),
                  pl.BlockSpec((1, d), lambda i: (0, 0)),
                  _resident((d, main), lambda i: (0, 0)),
                  _resident(gate_w.shape, lambda i: (0, 0)),
                  table, table, table],
        out_specs=[o[1] for o in outs],
        out_shape=[o[0] for o in outs],
        scratch_shapes=[pltpu.VMEM((2 * N_KV, tm, HEAD_DIM), F32)],
        compiler_params=_params(("parallel",), 48),
        name="in_proj",
    )(x, g.reshape(1, d), w, gate_w, cos2, sin2, blk_mask)


CONV_ROWS = 512


def _conv_body(u_ref, halo_ref, dw_ref, db_ref, lg_ref, lb_ref, pw_ref, og_ref, o_ref, ext_scr, y_scr, *, ts):
    i = pl.program_id(1)
    ext_scr[0, 0:CONV_HALO, :] = jnp.where(i == 0, 0.0, halo_ref[0])
    ext_scr[0, CONV_HALO:CONV_HALO + ts, :] = u_ref[0]
    moved = CONV_HALO + ts - V7X_SUBLANES
    for s in range(1, V7X_SUBLANES):
        ext_scr[s, 0:moved, :] = ext_scr[0, s:s + moved, :]
    first = CONV_HALO - (CONV_K - 1)
    for c in range(ts // CONV_ROWS):
        r0 = c * CONV_ROWS
        acc = jnp.broadcast_to(db_ref[...], (CONV_ROWS, CONV_CH))
        for k in range(CONV_K):
            s, base = (first + k) % V7X_SUBLANES, (first + k) // V7X_SUBLANES * V7X_SUBLANES
            acc = acc + dw_ref[k:k + 1, :] * ext_scr[s, r0 + base:r0 + base + CONV_ROWS, :]
        mu = jnp.mean(acc, axis=-1, keepdims=True)
        xc = acc - mu
        var = jnp.mean(xc * xc, axis=-1, keepdims=True)
        y = xc * lax.rsqrt(var + EPS) * lg_ref[...] + lb_ref[...]
        y_scr[r0:r0 + CONV_ROWS, :] = (y * jax.nn.sigmoid(y)).astype(BF16)
    z = jnp.dot(y_scr[...], pw_ref[...], preferred_element_type=F32)
    o_ref[0] = _rms(z, og_ref[...]).astype(BF16)


def _conv(u, dw_w, dw_b, ln_g, ln_b, pw_w, out_g):
    b, s, c = u.shape
    ts = min(512, s)
    per = ts // CONV_HALO
    vec = pl.BlockSpec((1, c), lambda bi, i: (0, 0))
    return pl.pallas_call(
        functools.partial(_conv_body, ts=ts),
        grid=(b, s // ts),
        in_specs=[pl.BlockSpec((1, ts, c), lambda bi, i: (bi, i, 0)),
                  pl.BlockSpec((1, CONV_HALO, c), lambda bi, i: (bi, jnp.maximum(i * per - 1, 0), 0)),
                  pl.BlockSpec((CONV_K, c), lambda bi, i: (0, 0)),
                  vec, vec, vec,
                  pl.BlockSpec((c, c), lambda bi, i: (0, 0)),
                  vec],
        out_specs=pl.BlockSpec((1, ts, c), lambda bi, i: (bi, i, 0)),
        out_shape=jax.ShapeDtypeStruct((b, s, c), BF16),
        scratch_shapes=[pltpu.VMEM((V7X_SUBLANES, CONV_HALO + ts, c), F32), pltpu.VMEM((ts, c), BF16)],
        compiler_params=_params(("parallel", "parallel"), 32),
        name="conv",
    )(u, u, dw_w, dw_b.reshape(1, c), ln_g.reshape(1, c), ln_b.reshape(1, c), pw_w.astype(BF16),
      out_g.reshape(1, c))


def _compress_one(u_ref, pos_ref, w1_ref, w2_ref, o_ref):
    u = u_ref[0, 0]
    half = u.shape[1]
    nu = u.shape[0]
    top = jnp.dot((u + pos_ref[0:1, :]).astype(BF16), w1_ref[0:half, :], preferred_element_type=F32)
    bot = jnp.dot((u + pos_ref[1:2, :]).astype(BF16), w1_ref[half:2 * half, :], preferred_element_type=F32)
    hid = top + pltpu.roll(bot, nu - 1, 0)
    hid = hid * jax.nn.sigmoid(hid)
    o_ref[0, 0] = jnp.dot(hid.astype(BF16), w2_ref[...], preferred_element_type=F32).astype(BF16)


def _compress_body(uk_ref, uv_ref, pk_ref, pv_ref, kw1_ref, kw2_ref, vw1_ref, vw2_ref, ok_ref, ov_ref):
    _compress_one(uk_ref, pk_ref, kw1_ref, kw2_ref, ok_ref)
    _compress_one(uv_ref, pv_ref, vw1_ref, vw2_ref, ov_ref)


def _compress(kc, vc, pos_k, pos_v, kw1, kw2, vw1, vw2):
    b, g, nu, unit = kc.shape
    dh = unit // CMP_STRIDE
    pos = lambda p: p.reshape(CMP_BLOCK // CMP_STRIDE, unit)
    u_spec = pl.BlockSpec((1, 1, nu, unit), lambda bi, gi: (bi, gi, 0, 0))
    full = lambda shape: pl.BlockSpec(shape, lambda bi, gi: (0,) * len(shape))
    o_spec = pl.BlockSpec((1, 1, nu, dh), lambda bi, gi: (bi, gi, 0, 0))
    o_shape = jax.ShapeDtypeStruct((b, g, nu, dh), BF16)
    return pl.pallas_call(
        _compress_body,
        grid=(b, g),
        in_specs=[u_spec, u_spec, full((2, unit)), full((2, unit)),
                  full((CMP_BLOCK * dh, CMP_HIDDEN)), full((CMP_HIDDEN, dh)),
                  full((CMP_BLOCK * dh, CMP_HIDDEN)), full((CMP_HIDDEN, dh))],
        out_specs=[o_spec, o_spec],
        out_shape=[o_shape, o_shape],
        compiler_params=_params(("parallel", "parallel"), 40),
        name="compress",
    )(kc, vc, pos(pos_k), pos(pos_v),
      kw1.astype(BF16), kw2.astype(BF16), vw1.astype(BF16), vw2.astype(BF16))


def _cmp_select_prefix(q_ref, kc_ref, vc_ref, ct_ref, ocmp_ref, sel_ref, *, tq, nc, nb):
    for sub in range(tq // CMP_SUB):
        _cmp_select_sub(q_ref, kc_ref, vc_ref, ct_ref, ocmp_ref, sel_ref,
                        row0=sub * CMP_SUB, q0=pl.program_id(2) * tq + sub * CMP_SUB, tq=CMP_SUB, nc=nc, nb=nb)


def _cmp_select_sub(q_ref, kc_ref, vc_ref, ct_ref, ocmp_ref, sel_ref, *, row0, q0, tq, nc, nb):
    n_slc = ct_ref.shape[0]
    q = q_ref[0, :, row0:row0 + tq, :].reshape(GROUP * tq, HEAD_DIM)
    s = lax.dot_general(q, kc_ref[0, 0, 0:nc, :], _NT, preferred_element_type=F32).reshape(GROUP, tq, nc)
    t = q0 + lax.broadcasted_iota(jnp.int32, (tq, nc), 0)
    cmp_end = lax.broadcasted_iota(jnp.int32, (tq, nc), 1) * CMP_STRIDE + (CMP_BLOCK - 1)
    s = s + jnp.where(cmp_end <= t, 0.0, MASKED)[None]
    m = jnp.max(s, axis=-1, keepdims=True)
    m = jnp.where(m > 0.5 * MASKED, m, 0.0)
    e = jnp.exp2(s - m)
    p = e * (1.0 / jnp.maximum(jnp.sum(e, axis=-1, keepdims=True), 1e-30))
    o = jnp.dot(p.reshape(GROUP * tq, nc).astype(BF16), vc_ref[0, 0, 0:nc, :], preferred_element_type=F32)
    ocmp_ref[0, :, row0:row0 + tq, :] = o.reshape(GROUP, tq, HEAD_DIM)

    psum = p[0] + p[1] + p[2] + p[3]
    hi = psum.astype(BF16)
    r1 = psum - hi.astype(F32)
    mid = r1.astype(BF16)
    lo = (r1 - mid.astype(F32)).astype(BF16)
    ct = ct_ref[0:nb, 0:nc]
    imp = (lax.dot_general(ct, hi, _NT, preferred_element_type=F32)
           + lax.dot_general(ct, mid, _NT, preferred_element_type=F32)
           + lax.dot_general(ct, lo, _NT, preferred_element_type=F32))

    blk = lax.broadcasted_iota(jnp.int32, (nb, tq), 0)
    jt = (q0 + lax.broadcasted_iota(jnp.int32, (nb, tq), 1)) // SLC_BLOCK
    forced = (blk == 0) | (blk == jt) | (blk == jt - 1)
    candidate = (blk >= 1) & (blk < jt - 1)
    val = jnp.where(candidate, imp, -1.0)
    blk_f = blk.astype(F32)
    for _ in range(N_SELECT - N_FORCED):
        best = jnp.max(val, axis=0, keepdims=True)
        first = jnp.min(jnp.where(val == best, blk_f, float(n_slc)), axis=0, keepdims=True)
        val = jnp.where(blk_f == first, -1.0, val)
    unsel = jnp.where(forced | (candidate & (val < 0.0)), 0.0, 1.0)
    if nb < n_slc:
        unsel = jnp.concatenate([unsel, jnp.ones((n_slc - nb, tq), F32)], axis=0)
    sel_ref[0, 0, row0:row0 + tq, :] = unsel.T.astype(BF16)


def _cmp_select_body(q_ref, kc_ref, vc_ref, ct_ref, ocmp_ref, sel_ref, *, tq):
    nu = kc_ref.shape[2]
    n_slc = ct_ref.shape[0]
    per_slc = nu // n_slc
    visible = (pl.program_id(2) * tq + tq - CMP_BLOCK) // CMP_STRIDE + 1
    n_prefix = nu // V7X_LANES
    need = jnp.clip((visible + V7X_LANES - 1) // V7X_LANES, 1, n_prefix)
    for v in range(1, n_prefix + 1):
        nc = v * V7X_LANES
        pl.when(need == v)(functools.partial(
            _cmp_select_prefix, q_ref, kc_ref, vc_ref, ct_ref, ocmp_ref, sel_ref, tq=tq, nc=nc, nb=nc // per_slc))


def _cmp_to_slc_t(nu, n_slc):
    per_slc = SLC_BLOCK // CMP_STRIDE
    c = np.arange(nu)[None, :]
    j = np.arange(n_slc)[:, None]
    m = np.zeros((n_slc, nu), np.float32)
    for unit in range(CMP_BLOCK // CMP_STRIDE):
        m += ((c + unit) // per_slc == j)
    m[:, nu - 1] = 0.0
    return jnp.asarray(m, BF16)


def _cmp_select(q_raw, k_cmp, v_cmp):
    b, _, s, dh = q_raw.shape
    nu = k_cmp.shape[2]
    n_slc = s // SLC_BLOCK
    tq = 1024
    for q_end in range(tq, s + 1, tq):
        prefix = -(-((q_end - CMP_BLOCK) // CMP_STRIDE + 1) // V7X_LANES) * V7X_LANES
        assert prefix * n_slc // nu >= (q_end - 1) // SLC_BLOCK + 1
    kv_spec = pl.BlockSpec((1, 1, nu, dh), lambda bi, gi, qi: (bi, gi, 0, 0))
    return pl.pallas_call(
        functools.partial(_cmp_select_body, tq=tq),
        grid=(b, N_KV, s // tq),
        in_specs=[pl.BlockSpec((1, GROUP, tq, dh), lambda bi, gi, qi: (bi, gi, qi, 0)),
                  kv_spec, kv_spec,
                  pl.BlockSpec((n_slc, nu), lambda bi, gi, qi: (0, 0))],
        out_specs=[pl.BlockSpec((1, GROUP, tq, dh), lambda bi, gi, qi: (bi, gi, qi, 0)),
                   pl.BlockSpec((1, 1, tq, n_slc), lambda bi, gi, qi: (bi, gi, qi, 0))],
        out_shape=[jax.ShapeDtypeStruct((b, N_HEADS, s, dh), F32),
                   jax.ShapeDtypeStruct((b, N_KV, s, n_slc), BF16)],
        compiler_params=_params(("parallel", "parallel", "parallel"), 32),
        name="cmp_select",
    )(q_raw, k_cmp, v_cmp, _cmp_to_slc_t(nu, n_slc))


LOWER, UPPER = 0, 1


def _attend_body(q_ref, k_ref, v_ref, tri_ref, unsel_ref, ocmp_ref, g_ref, o_ref,
                 qa_scr, s_scr, m_scr, acc_scr, *, tq, tk):
    q0 = pl.program_id(2) * tq
    rows = GROUP * tq

    unsel = unsel_ref[0, 0]
    for r in range(GROUP):
        qa_scr[r * tq:(r + 1) * tq, 0:HEAD_DIM] = q_ref[0, r]
        qa_scr[r * tq:(r + 1) * tq, HEAD_DIM:2 * HEAD_DIM] = unsel
    m_scr[...] = jnp.full(m_scr.shape, MASKED, F32)
    acc_scr[...] = jnp.zeros(acc_scr.shape, F32)

    last = q0 // tk
    n_win = jnp.minimum(last + 1, WINDOW // tk + 1)
    n_tiles = last + 1 + n_win

    def tile(i):
        branch = (i > last).astype(jnp.int32)
        kt = i - branch * n_win
        return branch, kt, pl.multiple_of(kt * tk, tk)

    def scores(i):
        branch, _, k0 = tile(i)
        s_scr[...] = lax.dot_general(qa_scr[...], k_ref[0, 0, branch, pl.ds(k0, tk), :], _NT,
                                     preferred_element_type=F32)

    def softmax_pv(i, boundary):
        branch, kt, k0 = tile(i)
        if boundary:
            kind = jnp.where(kt == last, LOWER, UPPER)
            s = (s_scr[...].reshape(GROUP, tq, tk) + tri_ref[kind][None]).reshape(rows, tk)
        else:
            s = s_scr[...]
        m_old = m_scr[branch]
        m_new = jnp.maximum(m_old, jnp.max(s, axis=-1, keepdims=True))
        p = jnp.exp2(s - jnp.concatenate([m_new] * (tk // V7X_LANES), axis=1)).astype(BF16)
        pv = jnp.dot(p, v_ref[0, 0, branch, pl.ds(k0, tk), :], preferred_element_type=F32)
        alpha = jnp.exp2(m_old - m_new)
        acc_scr[branch] = jnp.concatenate([alpha] * (2 * HEAD_DIM // V7X_LANES), axis=1) * acc_scr[branch] + pv
        m_scr[branch] = m_new

    scores(0)

    def interior_step(i, carry):
        softmax_pv(i, False)
        scores(i + 1)
        return carry

    def boundary_step(i, carry):
        softmax_pv(i, True)
        scores(i + 1)
        return carry

    lax.fori_loop(0, last, interior_step, 0)
    lax.fori_loop(last, n_tiles - 1, boundary_step, 0)
    softmax_pv(n_tiles - 1, True)

    def normalized(branch):
        acc = acc_scr[branch]
        return acc[:, 0:HEAD_DIM] * (1.0 / acc[:, HEAD_DIM:2 * HEAD_DIM])

    o_slc = normalized(SLC)
    o_win = normalized(WIN)
    gate = g_ref[0, 0]
    for r in range(GROUP):
        o_ref[0, :, r * HEAD_DIM:(r + 1) * HEAD_DIM] = (
            gate[:, 3 * r:3 * r + 1] * ocmp_ref[0, r]
            + gate[:, 3 * r + 1:3 * r + 2] * o_slc[r * tq:(r + 1) * tq]
            + gate[:, 3 * r + 2:3 * r + 3] * o_win[r * tq:(r + 1) * tq])


def _attend(q_rot, k_all, v_all, unsel, o_cmp, gates):
    b, _, s, dh = q_rot.shape
    n_slc = unsel.shape[3]
    assert n_slc == dh, "the unselected one-hot fills the second half of the augmented contraction"
    tq = tk = WINDOW
    row, col = np.arange(tq)[:, None], np.arange(tk)[None, :]
    tri = jnp.asarray(np.stack([np.where(col <= row, 0.0, MASKED),
                                np.where(col > row, 0.0, MASKED)]), F32)
    q_spec = pl.BlockSpec((1, GROUP, tq, dh), lambda bi, gi, qi: (bi, gi, qi, 0))
    kv_spec = _resident((1, 1, 2, s, 2 * dh), lambda bi, gi, qi: (bi, gi, 0, 0, 0))
    row_spec = lambda w: pl.BlockSpec((1, 1, tq, w), lambda bi, gi, qi: (bi, gi, qi, 0))
    rows = GROUP * tq
    return pl.pallas_call(
        functools.partial(_attend_body, tq=tq, tk=tk),
        grid=(b, N_KV, s // tq),
        in_specs=[q_spec, kv_spec, kv_spec, _resident(tri.shape, lambda bi, gi, qi: (0, 0, 0)),
                  row_spec(n_slc), q_spec, row_spec(V7X_LANES)],
        out_specs=pl.BlockSpec((1, tq, GROUP * dh), lambda bi, gi, qi: (bi, qi, gi)),
        out_shape=jax.ShapeDtypeStruct((b, s, Q_DIM), F32),
        scratch_shapes=[pltpu.VMEM((rows, 2 * dh), BF16),
                        pltpu.VMEM((rows, tk), F32),
                        pltpu.VMEM((2, rows, V7X_LANES), F32),
                        pltpu.VMEM((2, rows, 2 * dh), F32)],
        compiler_params=_params(("parallel", "parallel", "arbitrary"), 48),
        name="attend",
    )(q_rot, k_all, v_all, tri, unsel, o_cmp, gates)


def _out_proj_body(cn_ref, a_ref, x_ref, gn_ref, wc_ref, wa_ref, o_ref):
    an = _rms(a_ref[...], gn_ref[...]).astype(BF16)
    y = (jnp.dot(cn_ref[...], wc_ref[...], preferred_element_type=F32)
         + jnp.dot(an, wa_ref[...], preferred_element_type=F32))
    o_ref[...] = x_ref[...] + y


def _out_proj(conv_n, attn, x, nsa_g, w_out):
    t, d = x.shape
    tm = min(512, t)
    wc = w_out[:CONV_CH].astype(BF16)
    wa = w_out[CONV_CH:].astype(BF16)
    return pl.pallas_call(
        _out_proj_body,
        grid=(t // tm,),
        in_specs=[pl.BlockSpec((tm, CONV_CH), lambda i: (i, 0)),
                  pl.BlockSpec((tm, Q_DIM), lambda i: (i, 0)),
                  pl.BlockSpec((tm, d), lambda i: (i, 0)),
                  pl.BlockSpec((1, Q_DIM), lambda i: (0, 0)),
                  _resident((CONV_CH, d), lambda i: (0, 0)),
                  _resident((Q_DIM, d), lambda i: (0, 0))],
        out_specs=pl.BlockSpec((tm, d), lambda i: (i, 0)),
        out_shape=jax.ShapeDtypeStruct((t, d), F32),
        compiler_params=_params(("parallel",), 40),
        name="out_proj",
    )(conv_n, attn, x, nsa_g.reshape(1, Q_DIM), wc, wa)


def kernel(x, ffn1_norm, ffn1_w_gate, ffn1_w_up, ffn1_w_down, mix_norm, w_in, cmp_pos_k, cmp_pos_v, cmp_k_w1, cmp_k_w2, cmp_v_w1, cmp_v_w2, conv_dw_w, conv_dw_b, conv_ln_g, conv_ln_b, conv_pw_w, out_norm_conv, out_norm_nsa, w_out, ffn2_norm, ffn2_w_gate, ffn2_w_up, ffn2_w_down, final_norm):
    b, s, d = x.shape
    assert s % (SLC_BLOCK * V7X_LANES) == 0, "selection blocks must fill whole 128-lane rows"
    depth = ffn1_norm.shape[0]
    y = x.reshape(b * s, d)
    for l in range(depth):
        y = _ffn(y, ffn1_norm[l], ffn1_w_gate[l], ffn1_w_up[l], ffn1_w_down[l])
        u, q_raw, q_rot, kc, vc, k_all, v_all, gates = _in_proj(y, mix_norm[l], w_in[l], b, s)
        conv_n = _conv(u.reshape(b, s, CONV_CH), conv_dw_w[l], conv_dw_b[l], conv_ln_g[l], conv_ln_b[l],
                       conv_pw_w[l], out_norm_conv[l])
        k_cmp, v_cmp = _compress(kc, vc, cmp_pos_k[l], cmp_pos_v[l],
                                 cmp_k_w1[l], cmp_k_w2[l], cmp_v_w1[l], cmp_v_w2[l])
        o_cmp, unsel = _cmp_select(q_raw, k_cmp, v_cmp)
        attn = _attend(q_rot, k_all, v_all, unsel, o_cmp, gates)
        y = _out_proj(conv_n.reshape(b * s, CONV_CH), attn.reshape(b * s, Q_DIM), y, out_norm_nsa[l], w_out[l])
        y = _ffn(y, ffn2_norm[l], ffn2_w_gate[l], ffn2_w_up[l], ffn2_w_down[l],
                 final_g=final_norm if l == depth - 1 else None)
    return y.reshape(b, s, d)
```

```python
import functools

import numpy as np
import jax
import jax.numpy as jnp
from jax import lax
from jax.experimental import pallas as pl
from jax.experimental.pallas import tpu as pltpu

F32 = jnp.float32
BF16 = jnp.bfloat16

V7X_LANES = 128
V7X_SUBLANES = 8

CONV_CH = 512
N_HEADS = 12
HEAD_DIM = 128
N_KV = 3
GROUP = N_HEADS // N_KV
CONV_K = 31
CMP_BLOCK = 32
CMP_STRIDE = 16
CMP_HIDDEN = 256
SLC_BLOCK = 64
N_SELECT = 16
N_FORCED = 3
WINDOW = 512
ROPE_THETA = 10000.0
EPS = 1e-6
MASKED = -1e30
LOG2_E = 1.4426950408889634
SLC, WIN = 0, 1

KV_DIM = N_KV * HEAD_DIM
Q_DIM = N_HEADS * HEAD_DIM
CMP_SUB = 1024
CONV_HALO = 32

_NT = (((1,), (1,)), ((), ()))


def _rms(x, g):
    return x * lax.rsqrt(jnp.mean(x * x, axis=-1, keepdims=True) + EPS) * g


def _params(semantics, vmem_mib):
    return pltpu.CompilerParams(dimension_semantics=semantics, vmem_limit_bytes=vmem_mib * 2 ** 20)


def _resident(shape, index_map):
    return pl.BlockSpec(shape, index_map, pipeline_mode=pl.Buffered(1))


def _ffn_body(*refs, n_main, has_tail, final_norm):
    refs = list(refs)
    x_ref, g_ref, wg_ref, wu_ref, wd_ref = refs[:5]
    del refs[:5]
    if has_tail:
        tail_refs = refs[:3]
        del refs[:3]
    if final_norm:
        fg_ref = refs.pop(0)
    o_ref, h_scr = refs
    j = pl.program_id(1)

    @pl.when(j == 0)
    def _():
        x = x_ref[...]
        h_scr[...] = _rms(x, g_ref[...]).astype(BF16)
        o_ref[...] = x

    def hidden_slab(wg, wu, wd):
        h = h_scr[...]
        a = jnp.dot(h, wg[...], preferred_element_type=F32)
        b = jnp.dot(h, wu[...], preferred_element_type=F32)
        z = (a * jax.nn.sigmoid(a) * b).astype(BF16)
        o_ref[...] += 0.5 * jnp.dot(z, wd[...], preferred_element_type=F32)

    if has_tail:
        pl.when(j < n_main)(functools.partial(hidden_slab, wg_ref, wu_ref, wd_ref))
        pl.when(j == n_main)(functools.partial(hidden_slab, *tail_refs))
    else:
        hidden_slab(wg_ref, wu_ref, wd_ref)

    if final_norm:
        @pl.when(j == n_main + has_tail - 1)
        def _():
            o_ref[...] = _rms(o_ref[...], fg_ref[...])


def _ffn(x, g, w_gate, w_up, w_down, final_g=None):
    t, d = x.shape
    f = w_gate.shape[1]
    tm = min(512, t)
    tf = 512
    n_main, f_tail = divmod(f, tf)
    assert f_tail % V7X_LANES == 0 and n_main >= 1
    has_tail = int(f_tail > 0)
    f_main = n_main * tf
    wg, wu, wd = w_gate.astype(BF16), w_up.astype(BF16), w_down.astype(BF16)
    final_norm = final_g is not None
    row = pl.BlockSpec((tm, d), lambda i, j: (i, 0))
    vec = pl.BlockSpec((1, d), lambda i, j: (0, 0))
    slab = lambda i, j: (0, jnp.minimum(j, n_main - 1))
    in_specs = [row, vec,
                pl.BlockSpec((d, tf), slab),
                pl.BlockSpec((d, tf), slab),
                pl.BlockSpec((tf, d), lambda i, j: (jnp.minimum(j, n_main - 1), 0))]
    args = [x, g.reshape(1, d), wg, wu, wd]
    if has_tail:
        in_specs += [_resident((d, f_tail), lambda i, j: (0, 0)),
                     _resident((d, f_tail), lambda i, j: (0, 0)),
                     _resident((f_tail, d), lambda i, j: (0, 0))]
        args += [wg[:, f_main:], wu[:, f_main:], wd[f_main:]]
    if final_norm:
        in_specs.append(vec)
        args.append(final_g.reshape(1, d))
    return pl.pallas_call(
        functools.partial(_ffn_body, n_main=n_main, has_tail=has_tail, final_norm=final_norm),
        grid=(t // tm, n_main + has_tail),
        in_specs=in_specs,
        out_specs=row,
        out_shape=jax.ShapeDtypeStruct((t, d), F32),
        scratch_shapes=[pltpu.VMEM((tm, d), BF16)],
        compiler_params=_params(("parallel", "arbitrary"), 48),
        name="ffn_final" if final_norm else "ffn",
    )(*args)


def _rope(x, cos2, sin2):
    return x * cos2 + pltpu.roll(x, HEAD_DIM // 2, 1) * sin2


def _in_proj_body(x_ref, g_ref, w_ref, wgate_ref, cos_ref, sin_ref, blk_ref,
                  u_ref, qraw_ref, qrot_ref, kc_ref, vc_ref, k_ref, v_ref, gate_ref, kv_scr):
    h = _rms(x_ref[...], g_ref[...]).astype(BF16)
    cos2 = cos_ref[...]
    sin2 = sin_ref[...]
    scale = HEAD_DIM ** -0.5

    def proj(c0, width):
        return jnp.dot(h, w_ref[:, c0:c0 + width], preferred_element_type=F32)

    glu = proj(0, 2 * CONV_CH)
    u_ref[...] = glu[:, :CONV_CH] * jax.nn.sigmoid(glu[:, CONV_CH:])

    def head(cols, i):
        return cols[:, i * HEAD_DIM:(i + 1) * HEAD_DIM]

    c0 = 2 * CONV_CH
    for gk in range(N_KV):
        cols = proj(c0, GROUP * HEAD_DIM) * (scale * LOG2_E)
        for r in range(GROUP):
            qh = head(cols, r)
            qraw_ref[0, gk * GROUP + r] = qh.astype(BF16)
            qrot_ref[0, gk * GROUP + r] = _rope(qh, cos2, sin2).astype(BF16)
        c0 += GROUP * HEAD_DIM
    cols = proj(c0, 2 * KV_DIM)
    n_unit = kv_scr.shape[1] // CMP_STRIDE
    for ref, first in ((kc_ref, 0), (vc_ref, N_KV)):
        for gk in range(N_KV):
            kv_scr[first + gk] = head(cols, first + gk)
            for slot in range(CMP_STRIDE):
                ref[0, gk, :, slot * HEAD_DIM:(slot + 1) * HEAD_DIM] = kv_scr[
                    first + gk, pl.ds(slot, n_unit, stride=CMP_STRIDE), :]
    c0 += 2 * KV_DIM
    lo, hi = slice(0, HEAD_DIM), slice(HEAD_DIM, 2 * HEAD_DIM)
    for branch in (SLC, WIN):
        cols = proj(c0, 2 * KV_DIM)
        for gk in range(N_KV):
            k_ref[0, gk, branch, :, lo] = _rope(head(cols, gk), cos2, sin2).astype(BF16)
            v_ref[0, gk, branch, :, lo] = head(cols, N_KV + gk).astype(BF16)
        c0 += 2 * KV_DIM
    cols = jax.nn.sigmoid(jnp.dot(h, wgate_ref[...], preferred_element_type=F32))
    for gk in range(N_KV):
        k_ref[0, gk, SLC, :, hi] = blk_ref[...]
        k_ref[0, gk, WIN, :, hi] = jnp.zeros(blk_ref.shape, BF16)
        v_ref[0, gk, SLC, :, hi] = jnp.ones(blk_ref.shape, BF16)
        v_ref[0, gk, WIN, :, hi] = jnp.ones(blk_ref.shape, BF16)
        gate_ref[0, gk] = head(cols, gk)


def _in_proj(x, g, w_in, batch, seq):
    t, d = x.shape
    tm = min(256, seq)
    n_s = seq // tm
    main = 2 * CONV_CH + Q_DIM + 6 * KV_DIM
    gate_w = w_in[:, main:].reshape(d, N_KV, GROUP * 3)
    gate_w = jnp.pad(gate_w, ((0, 0), (0, 0), (0, V7X_LANES - GROUP * 3))).reshape(d, N_KV * V7X_LANES)
    gate_w = gate_w.astype(BF16)
    w = w_in[:, :main].astype(BF16)

    inv = jnp.power(ROPE_THETA, -jnp.arange(0, HEAD_DIM, 2, dtype=F32) / HEAD_DIM)
    ang = jnp.arange(seq, dtype=F32)[:, None] * inv[None, :]
    cos2 = jnp.concatenate([jnp.cos(ang), jnp.cos(ang)], axis=1)
    sin2 = jnp.concatenate([-jnp.sin(ang), jnp.sin(ang)], axis=1)

    key_blk = np.arange(seq)[:, None] // SLC_BLOCK == np.arange(seq // SLC_BLOCK)[None, :]
    blk_mask = jnp.asarray(np.where(key_blk, -(2.0 ** 100), 0.0), BF16)

    def heads(n, dtype):
        return (jax.ShapeDtypeStruct((batch, n, seq, HEAD_DIM), dtype),
                pl.BlockSpec((1, n, tm, HEAD_DIM), lambda i: (i // n_s, 0, i % n_s, 0)))

    stacked = (jax.ShapeDtypeStruct((batch, N_KV, 2, seq, 2 * HEAD_DIM), BF16),
               pl.BlockSpec((1, N_KV, 2, tm, 2 * HEAD_DIM), lambda i: (i // n_s, 0, 0, i % n_s, 0)))
    unit = CMP_STRIDE * HEAD_DIM
    units = (jax.ShapeDtypeStruct((batch, N_KV, seq // CMP_STRIDE, unit), F32),
             pl.BlockSpec((1, N_KV, tm // CMP_STRIDE, unit), lambda i: (i // n_s, 0, i % n_s, 0)))
    outs = [(jax.ShapeDtypeStruct((t, CONV_CH), F32), pl.BlockSpec((tm, CONV_CH), lambda i: (i, 0))),
            heads(N_HEADS, BF16), heads(N_HEADS, BF16),
            units, units,
            stacked, stacked,
            heads(N_KV, F32)]
    table = pl.BlockSpec((tm, HEAD_DIM), lambda i: (i % n_s, 0))
    return pl.pallas_call(
        _in_proj_body,
        grid=(t // tm,),
        in_specs=[pl.BlockSpec((tm, d), lambda i: (i, 0)),
                  pl.BlockSpec((1, d), lambda i: (0, 0)),
                  _resident((d, main), lambda i: (0, 0)),
                  _resident(gate_w.shape, lambda i: (0, 0)),
                  table, table, table],
        out_specs=[o[1] for o in outs],
        out_shape=[o[0] for o in outs],
        scratch_shapes=[pltpu.VMEM((2 * N_KV, tm, HEAD_DIM), F32)],
        compiler_params=_params(("parallel",), 48),
        name="in_proj",
    )(x, g.reshape(1, d), w, gate_w, cos2, sin2, blk_mask)


CONV_ROWS = 512


def _conv_body(u_ref, halo_ref, dw_ref, db_ref, lg_ref, lb_ref, pw_ref, og_ref, o_ref, ext_scr, y_scr, *, ts):
    i = pl.program_id(1)
    ext_scr[0, 0:CONV_HALO, :] = jnp.where(i == 0, 0.0, halo_ref[0])
    ext_scr[0, CONV_HALO:CONV_HALO + ts, :] = u_ref[0]
    moved = CONV_HALO + ts - V7X_SUBLANES
    for s in range(1, V7X_SUBLANES):
        ext_scr[s, 0:moved, :] = ext_scr[0, s:s + moved, :]
    first = CONV_HALO - (CONV_K - 1)
    for c in range(ts // CONV_ROWS):
        r0 = c * CONV_ROWS
        acc = jnp.broadcast_to(db_ref[...], (CONV_ROWS, CONV_CH))
        for k in range(CONV_K):
            s, base = (first + k) % V7X_SUBLANES, (first + k) // V7X_SUBLANES * V7X_SUBLANES
            acc = acc + dw_ref[k:k + 1, :] * ext_scr[s, r0 + base:r0 + base + CONV_ROWS, :]
        mu = jnp.mean(acc, axis=-1, keepdims=True)
        xc = acc - mu
        var = jnp.mean(xc * xc, axis=-1, keepdims=True)
        y = xc * lax.rsqrt(var + EPS) * lg_ref[...] + lb_ref[...]
        y_scr[r0:r0 + CONV_ROWS, :] = (y * jax.nn.sigmoid(y)).astype(BF16)
    z = jnp.dot(y_scr[...], pw_ref[...], preferred_element_type=F32)
    o_ref[0] = _rms(z, og_ref[...]).astype(BF16)


def _conv(u, dw_w, dw_b, ln_g, ln_b, pw_w, out_g):
    b, s, c = u.shape
    ts = min(512, s)
    per = ts // CONV_HALO
    vec = pl.BlockSpec((1, c), lambda bi, i: (0, 0))
    return pl.pallas_call(
        functools.partial(_conv_body, ts=ts),
        grid=(b, s // ts),
        in_specs=[pl.BlockSpec((1, ts, c), lambda bi, i: (bi, i, 0)),
                  pl.BlockSpec((1, CONV_HALO, c), lambda bi, i: (bi, jnp.maximum(i * per - 1, 0), 0)),
                  pl.BlockSpec((CONV_K, c), lambda bi, i: (0, 0)),
                  vec, vec, vec,
                  pl.BlockSpec((c, c), lambda bi, i: (0, 0)),
                  vec],
        out_specs=pl.BlockSpec((1, ts, c), lambda bi, i: (bi, i, 0)),
        out_shape=jax.ShapeDtypeStruct((b, s, c), BF16),
        scratch_shapes=[pltpu.VMEM((V7X_SUBLANES, CONV_HALO + ts, c), F32), pltpu.VMEM((ts, c), BF16)],
        compiler_params=_params(("parallel", "parallel"), 32),
        name="conv",
    )(u, u, dw_w, dw_b.reshape(1, c), ln_g.reshape(1, c), ln_b.reshape(1, c), pw_w.astype(BF16),
      out_g.reshape(1, c))


def _compress_one(u_ref, pos_ref, w1_ref, w2_ref, o_ref):
    u = u_ref[0, 0]
    half = u.shape[1]
    nu = u.shape[0]
    top = jnp.dot((u + pos_ref[0:1, :]).astype(BF16), w1_ref[0:half, :], preferred_element_type=F32)
    bot = jnp.dot((u + pos_ref[1:2, :]).astype(BF16), w1_ref[half:2 * half, :], preferred_element_type=F32)
    hid = top + pltpu.roll(bot, nu - 1, 0)
    hid = hid * jax.nn.sigmoid(hid)
    o_ref[0, 0] = jnp.dot(hid.astype(BF16), w2_ref[...], preferred_element_type=F32).astype(BF16)


def _compress_body(uk_ref, uv_ref, pk_ref, pv_ref, kw1_ref, kw2_ref, vw1_ref, vw2_ref, ok_ref, ov_ref):
    _compress_one(uk_ref, pk_ref, kw1_ref, kw2_ref, ok_ref)
    _compress_one(uv_ref, pv_ref, vw1_ref, vw2_ref, ov_ref)


def _compress(kc, vc, pos_k, pos_v, kw1, kw2, vw1, vw2):
    b, g, nu, unit = kc.shape
    dh = unit // CMP_STRIDE
    pos = lambda p: p.reshape(CMP_BLOCK // CMP_STRIDE, unit)
    u_spec = pl.BlockSpec((1, 1, nu, unit), lambda bi, gi: (bi, gi, 0, 0))
    full = lambda shape: pl.BlockSpec(shape, lambda bi, gi: (0,) * len(shape))
    o_spec = pl.BlockSpec((1, 1, nu, dh), lambda bi, gi: (bi, gi, 0, 0))
    o_shape = jax.ShapeDtypeStruct((b, g, nu, dh), BF16)
    return pl.pallas_call(
        _compress_body,
        grid=(b, g),
        in_specs=[u_spec, u_spec, full((2, unit)), full((2, unit)),
                  full((CMP_BLOCK * dh, CMP_HIDDEN)), full((CMP_HIDDEN, dh)),
                  full((CMP_BLOCK * dh, CMP_HIDDEN)), full((CMP_HIDDEN, dh))],
        out_specs=[o_spec, o_spec],
        out_shape=[o_shape, o_shape],
        compiler_params=_params(("parallel", "parallel"), 40),
        name="compress",
    )(kc, vc, pos(pos_k), pos(pos_v),
      kw1.astype(BF16), kw2.astype(BF16), vw1.astype(BF16), vw2.astype(BF16))


def _cmp_select_prefix(q_ref, kc_ref, vc_ref, ct_ref, ocmp_ref, sel_ref, *, tq, nc, nb):
    for sub in range(tq // CMP_SUB):
        _cmp_select_sub(q_ref, kc_ref, vc_ref, ct_ref, ocmp_ref, sel_ref,
                        row0=sub * CMP_SUB, q0=pl.program_id(2) * tq + sub * CMP_SUB, tq=CMP_SUB, nc=nc, nb=nb)


def _cmp_select_sub(q_ref, kc_ref, vc_ref, ct_ref, ocmp_ref, sel_ref, *, row0, q0, tq, nc, nb):
    n_slc = ct_ref.shape[0]
    q = q_ref[0, :, row0:row0 + tq, :].reshape(GROUP * tq, HEAD_DIM)
    s = lax.dot_general(q, kc_ref[0, 0, 0:nc, :], _NT, preferred_element_type=F32).reshape(GROUP, tq, nc)
    t = q0 + lax.broadcasted_iota(jnp.int32, (tq, nc), 0)
    cmp_end = lax.broadcasted_iota(jnp.int32, (tq, nc), 1) * CMP_STRIDE + (CMP_BLOCK - 1)
    s = s + jnp.where(cmp_end <= t, 0.0, MASKED)[None]
    m = jnp.max(s, axis=-1, keepdims=True)
    m = jnp.where(m > 0.5 * MASKED, m, 0.0)
    e = jnp.exp2(s - m)
    p = e * (1.0 / jnp.maximum(jnp.sum(e, axis=-1, keepdims=True), 1e-30))
    o = jnp.dot(p.reshape(GROUP * tq, nc).astype(BF16), vc_ref[0, 0, 0:nc, :], preferred_element_type=F32)
    ocmp_ref[0, :, row0:row0 + tq, :] = o.reshape(GROUP, tq, HEAD_DIM)

    psum = p[0] + p[1] + p[2] + p[3]
    hi = psum.astype(BF16)
    r1 = psum - hi.astype(F32)
    mid = r1.astype(BF16)
    lo = (r1 - mid.astype(F32)).astype(BF16)
    ct = ct_ref[0:nb, 0:nc]
    imp = (lax.dot_general(ct, hi, _NT, preferred_element_type=F32)
           + lax.dot_general(ct, mid, _NT, preferred_element_type=F32)
           + lax.dot_general(ct, lo, _NT, preferred_element_type=F32))

    blk = lax.broadcasted_iota(jnp.int32, (nb, tq), 0)
    jt = (q0 + lax.broadcasted_iota(jnp.int32, (nb, tq), 1)) // SLC_BLOCK
    forced = (blk == 0) | (blk == jt) | (blk == jt - 1)
    candidate = (blk >= 1) & (blk < jt - 1)
    val = jnp.where(candidate, imp, -1.0)
    blk_f = blk.astype(F32)
    for _ in range(N_SELECT - N_FORCED):
        best = jnp.max(val, axis=0, keepdims=True)
        first = jnp.min(jnp.where(val == best, blk_f, float(n_slc)), axis=0, keepdims=True)
        val = jnp.where(blk_f == first, -1.0, val)
    unsel = jnp.where(forced | (candidate & (val < 0.0)), 0.0, 1.0)
    if nb < n_slc:
        unsel = jnp.concatenate([unsel, jnp.ones((n_slc - nb, tq), F32)], axis=0)
    sel_ref[0, 0, row0:row0 + tq, :] = unsel.T.astype(BF16)


def _cmp_select_body(q_ref, kc_ref, vc_ref, ct_ref, ocmp_ref, sel_ref, *, tq):
    nu = kc_ref.shape[2]
    n_slc = ct_ref.shape[0]
    per_slc = nu // n_slc
    visible = (pl.program_id(2) * tq + tq - CMP_BLOCK) // CMP_STRIDE + 1
    n_prefix = nu // V7X_LANES
    need = jnp.clip((visible + V7X_LANES - 1) // V7X_LANES, 1, n_prefix)
    for v in range(1, n_prefix + 1):
        nc = v * V7X_LANES
        pl.when(need == v)(functools.partial(
            _cmp_select_prefix, q_ref, kc_ref, vc_ref, ct_ref, ocmp_ref, sel_ref, tq=tq, nc=nc, nb=nc // per_slc))


def _cmp_to_slc_t(nu, n_slc):
    per_slc = SLC_BLOCK // CMP_STRIDE
    c = np.arange(nu)[None, :]
    j = np.arange(n_slc)[:, None]
    m = np.zeros((n_slc, nu), np.float32)
    for unit in range(CMP_BLOCK // CMP_STRIDE):
        m += ((c + unit) // per_slc == j)
    m[:, nu - 1] = 0.0
    return jnp.asarray(m, BF16)


def _cmp_select(q_raw, k_cmp, v_cmp):
    b, _, s, dh = q_raw.shape
    nu = k_cmp.shape[2]
    n_slc = s // SLC_BLOCK
    tq = 1024
    for q_end in range(tq, s + 1, tq):
        prefix = -(-((q_end - CMP_BLOCK) // CMP_STRIDE + 1) // V7X_LANES) * V7X_LANES
        assert prefix * n_slc // nu >= (q_end - 1) // SLC_BLOCK + 1
    kv_spec = pl.BlockSpec((1, 1, nu, dh), lambda bi, gi, qi: (bi, gi, 0, 0))
    return pl.pallas_call(
        functools.partial(_cmp_select_body, tq=tq),
        grid=(b, N_KV, s // tq),
        in_specs=[pl.BlockSpec((1, GROUP, tq, dh), lambda bi, gi, qi: (bi, gi, qi, 0)),
                  kv_spec, kv_spec,
                  pl.BlockSpec((n_slc, nu), lambda bi, gi, qi: (0, 0))],
        out_specs=[pl.BlockSpec((1, GROUP, tq, dh), lambda bi, gi, qi: (bi, gi, qi, 0)),
                   pl.BlockSpec((1, 1, tq, n_slc), lambda bi, gi, qi: (bi, gi, qi, 0))],
        out_shape=[jax.ShapeDtypeStruct((b, N_HEADS, s, dh), F32),
                   jax.ShapeDtypeStruct((b, N_KV, s, n_slc), BF16)],
        compiler_params=_params(("parallel", "parallel", "parallel"), 32),
        name="cmp_select",
    )(q_raw, k_cmp, v_cmp, _cmp_to_slc_t(nu, n_slc))


LOWER, UPPER = 0, 1


def _attend_body(q_ref, k_ref, v_ref, tri_ref, unsel_ref, ocmp_ref, g_ref, o_ref,
                 qa_scr, s_scr, m_scr, acc_scr, *, tq, tk):
    q0 = pl.program_id(2) * tq
    rows = GROUP * tq

    qa_scr[:, 0:HEAD_DIM] = q_ref[0].reshape(rows, HEAD_DIM)
    qa_scr[:, HEAD_DIM:2 * HEAD_DIM] = jnp.concatenate([unsel_ref[0, 0]] * GROUP, axis=0)
    m_scr[...] = jnp.full(m_scr.shape, MASKED, F32)
    acc_scr[...] = jnp.zeros(acc_scr.shape, F32)

    last = q0 // tk
    n_win = jnp.minimum(last + 1, WINDOW // tk + 1)
    n_tiles = last + 1 + n_win

    def tile(i):
        branch = (i > last).astype(jnp.int32)
        kt = i - branch * n_win
        return branch, kt, pl.multiple_of(kt * tk, tk)

    def scores(i):
        branch, _, k0 = tile(i)
        s_scr[...] = lax.dot_general(qa_scr[...], k_ref[0, 0, branch, pl.ds(k0, tk), :], _NT,
                                     preferred_element_type=F32)

    def softmax_pv(i, boundary):
        branch, kt, k0 = tile(i)
        if boundary:
            kind = jnp.where(kt == last, LOWER, UPPER)
            s = (s_scr[...].reshape(GROUP, tq, tk) + tri_ref[kind][None]).reshape(rows, tk)
        else:
            s = s_scr[...]
        m_old = m_scr[branch]
        m_new = jnp.maximum(m_old, jnp.max(s, axis=-1, keepdims=True))
        p = jnp.exp2(s - jnp.concatenate([m_new] * (tk // V7X_LANES), axis=1)).astype(BF16)
        pv = jnp.dot(p, v_ref[0, 0, branch, pl.ds(k0, tk), :], preferred_element_type=F32)
        alpha = jnp.exp2(m_old - m_new)
        acc_scr[branch] = jnp.concatenate([alpha] * (2 * HEAD_DIM // V7X_LANES), axis=1) * acc_scr[branch] + pv
        m_scr[branch] = m_new

    scores(0)

    def interior_step(i, carry):
        softmax_pv(i, False)
        scores(i + 1)
        return carry

    def boundary_step(i, carry):
        softmax_pv(i, True)
        scores(i + 1)
        return carry

    lax.fori_loop(0, last, interior_step, 0)
    lax.fori_loop(last, n_tiles - 1, boundary_step, 0)
    softmax_pv(n_tiles - 1, True)

    def normalized(branch):
        acc = acc_scr[branch]
        return acc[:, 0:HEAD_DIM] * (1.0 / acc[:, HEAD_DIM:2 * HEAD_DIM])

    gate = g_ref[0, 0]
    stacked = lambda j: jnp.concatenate([gate[:, 3 * r + j:3 * r + j + 1] for r in range(GROUP)], axis=0)
    o = (stacked(0) * ocmp_ref[0].reshape(rows, HEAD_DIM) + stacked(1) * normalized(SLC)
         + stacked(2) * normalized(WIN))
    for r in range(GROUP):
        o_ref[0, :, r * HEAD_DIM:(r + 1) * HEAD_DIM] = o[r * tq:(r + 1) * tq]


def _attend(q_rot, k_all, v_all, unsel, o_cmp, gates):
    b, _, s, dh = q_rot.shape
    n_slc = unsel.shape[3]
    assert n_slc == dh, "the unselected one-hot fills the second half of the augmented contraction"
    tq = tk = WINDOW
    row, col = np.arange(tq)[:, None], np.arange(tk)[None, :]
    tri = jnp.asarray(np.stack([np.where(col <= row, 0.0, MASKED),
                                np.where(col > row, 0.0, MASKED)]), F32)
    q_spec = pl.BlockSpec((1, GROUP, tq, dh), lambda bi, gi, qi: (bi, gi, qi, 0))
    kv_spec = _resident((1, 1, 2, s, 2 * dh), lambda bi, gi, qi: (bi, gi, 0, 0, 0))
    row_spec = lambda w: pl.BlockSpec((1, 1, tq, w), lambda bi, gi, qi: (bi, gi, qi, 0))
    rows = GROUP * tq
    return pl.pallas_call(
        functools.partial(_attend_body, tq=tq, tk=tk),
        grid=(b, N_KV, s // tq),
        in_specs=[q_spec, kv_spec, kv_spec, _resident(tri.shape, lambda bi, gi, qi: (0, 0, 0)),
                  row_spec(n_slc), q_spec, row_spec(V7X_LANES)],
        out_specs=pl.BlockSpec((1, tq, GROUP * dh), lambda bi, gi, qi: (bi, qi, gi)),
        out_shape=jax.ShapeDtypeStruct((b, s, Q_DIM), F32),
        scratch_shapes=[pltpu.VMEM((rows, 2 * dh), BF16),
                        pltpu.VMEM((rows, tk), F32),
                        pltpu.VMEM((2, rows, V7X_LANES), F32),
                        pltpu.VMEM((2, rows, 2 * dh), F32)],
        compiler_params=_params(("parallel", "parallel", "arbitrary"), 48),
        name="attend",
    )(q_rot, k_all, v_all, tri, unsel, o_cmp, gates)


def _out_proj_body(cn_ref, a_ref, x_ref, gn_ref, wc_ref, wa_ref, o_ref):
    an = _rms(a_ref[...], gn_ref[...]).astype(BF16)
    y = (jnp.dot(cn_ref[...], wc_ref[...], preferred_element_type=F32)
         + jnp.dot(an, wa_ref[...], preferred_element_type=F32))
    o_ref[...] = x_ref[...] + y


def _out_proj(conv_n, attn, x, nsa_g, w_out):
    t, d = x.shape
    tm = min(512, t)
    wc = w_out[:CONV_CH].astype(BF16)
    wa = w_out[CONV_CH:].astype(BF16)
    return pl.pallas_call(
        _out_proj_body,
        grid=(t // tm,),
        in_specs=[pl.BlockSpec((tm, CONV_CH), lambda i: (i, 0)),
                  pl.BlockSpec((tm, Q_DIM), lambda i: (i, 0)),
                  pl.BlockSpec((tm, d), lambda i: (i, 0)),
                  pl.BlockSpec((1, Q_DIM), lambda i: (0, 0)),
                  _resident((CONV_CH, d), lambda i: (0, 0)),
                  _resident((Q_DIM, d), lambda i: (0, 0))],
        out_specs=pl.BlockSpec((tm, d), lambda i: (i, 0)),
        out_shape=jax.ShapeDtypeStruct((t, d), F32),
        compiler_params=_params(("parallel",), 40),
        name="out_proj",
    )(conv_n, attn, x, nsa_g.reshape(1, Q_DIM), wc, wa)


def kernel(x, ffn1_norm, ffn1_w_gate, ffn1_w_up, ffn1_w_down, mix_norm, w_in, cmp_pos_k, cmp_pos_v, cmp_k_w1, cmp_k_w2, cmp_v_w1, cmp_v_w2, conv_dw_w, conv_dw_b, conv_ln_g, conv_ln_b, conv_pw_w, out_norm_conv, out_norm_nsa, w_out, ffn2_norm, ffn2_w_gate, ffn2_w_up, ffn2_w_down, final_norm):
    b, s, d = x.shape
    assert s % (SLC_BLOCK * V7X_LANES) == 0, "selection blocks must fill whole 128-lane rows"
    depth = ffn1_norm.shape[0]
    y = x.reshape(b * s, d)
    for l in range(depth):
        y = _ffn(y, ffn1_norm[l], ffn1_w_gate[l], ffn1_w_up[l], ffn1_w_down[l])
        u, q_raw, q_rot, kc, vc, k_all, v_all, gates = _in_proj(y, mix_norm[l], w_in[l], b, s)
        conv_n = _conv(u.reshape(b, s, CONV_CH), conv_dw_w[l], conv_dw_b[l], conv_ln_g[l], conv_ln_b[l],
                       conv_pw_w[l], out_norm_conv[l])
        k_cmp, v_cmp = _compress(kc, vc, cmp_pos_k[l], cmp_pos_v[l],
                                 cmp_k_w1[l], cmp_k_w2[l], cmp_v_w1[l], cmp_v_w2[l])
        o_cmp, unsel = _cmp_select(q_raw, k_cmp, v_cmp)
        attn = _attend(q_rot, k_all, v_all, unsel, o_cmp, gates)
        y = _out_proj(conv_n.reshape(b * s, CONV_CH), attn.reshape(b * s, Q_DIM), y, out_norm_nsa[l], w_out[l])
        y = _ffn(y, ffn2_norm[l], ffn2_w_gate[l], ffn2_w_up[l], ffn2_w_down[l],
                 final_g=final_norm if l == depth - 1 else None)
    return y.reshape(b, s, d)
```

```python
import functools

import numpy as np
import jax
import jax.numpy as jnp
from jax import lax
from jax.experimental import pallas as pl
from jax.experimental.pallas import tpu as pltpu

F32 = jnp.float32
BF16 = jnp.bfloat16

V7X_LANES = 128
V7X_SUBLANES = 8

CONV_CH = 512
N_HEADS = 12
HEAD_DIM = 128
N_KV = 3
GROUP = N_HEADS // N_KV
CONV_K = 31
CMP_BLOCK = 32
CMP_STRIDE = 16
CMP_HIDDEN = 256
SLC_BLOCK = 64
N_SELECT = 16
N_FORCED = 3
WINDOW = 512
ROPE_THETA = 10000.0
EPS = 1e-6
MASKED = -1e30
LOG2_E = 1.4426950408889634
SLC, WIN = 0, 1

KV_DIM = N_KV * HEAD_DIM
Q_DIM = N_HEADS * HEAD_DIM
CMP_SUB = 1024
CONV_HALO = 32

_NT = (((1,), (1,)), ((), ()))


def _rms(x, g):
    return x * lax.rsqrt(jnp.mean(x * x, axis=-1, keepdims=True) + EPS) * g


def _params(semantics, vmem_mib):
    return pltpu.CompilerParams(dimension_semantics=semantics, vmem_limit_bytes=vmem_mib * 2 ** 20)


def _resident(shape, index_map):
    return pl.BlockSpec(shape, index_map, pipeline_mode=pl.Buffered(1))


def _ffn_body(*refs, n_main, has_tail, final_norm):
    refs = list(refs)
    x_ref, g_ref, wg_ref, wu_ref, wd_ref = refs[:5]
    del refs[:5]
    if has_tail:
        tail_refs = refs[:3]
        del refs[:3]
    if final_norm:
        fg_ref = refs.pop(0)
    o_ref, h_scr = refs
    j = pl.program_id(1)

    @pl.when(j == 0)
    def _():
        h_scr[...] = _rms(x_ref[...], g_ref[...]).astype(BF16)

    def hidden_slab(wg, wu, wd, first=False):
        h = h_scr[...]
        a = jnp.dot(h, wg[...], preferred_element_type=F32)
        b = jnp.dot(h, wu[...], preferred_element_type=F32)
        z = (a * jax.nn.sigmoid(a) * b).astype(BF16)
        base = x_ref[...] if first else o_ref[...]
        o_ref[...] = base + 0.5 * jnp.dot(z, wd[...], preferred_element_type=F32)

    main = functools.partial(hidden_slab, wg_ref, wu_ref, wd_ref)
    pl.when(j == 0)(functools.partial(main, first=True))
    if has_tail:
        pl.when((j > 0) & (j < n_main))(main)
        pl.when(j == n_main)(functools.partial(hidden_slab, *tail_refs))
    else:
        pl.when(j > 0)(main)

    if final_norm:
        @pl.when(j == n_main + has_tail - 1)
        def _():
            o_ref[...] = _rms(o_ref[...], fg_ref[...])


def _ffn(x, g, w_gate, w_up, w_down, final_g=None):
    t, d = x.shape
    f = w_gate.shape[1]
    tm = min(512, t)
    tf = 1024 if f >= 1024 else 512
    n_main, f_tail = divmod(f, tf)
    assert f_tail % V7X_LANES == 0 and n_main >= 1
    has_tail = int(f_tail > 0)
    f_main = n_main * tf
    wg, wu, wd = w_gate.astype(BF16), w_up.astype(BF16), w_down.astype(BF16)
    final_norm = final_g is not None
    row = pl.BlockSpec((tm, d), lambda i, j: (i, 0))
    vec = pl.BlockSpec((1, d), lambda i, j: (0, 0))
    slab = lambda i, j: (0, jnp.minimum(j, n_main - 1))
    in_specs = [row, vec,
                pl.BlockSpec((d, tf), slab),
                pl.BlockSpec((d, tf), slab),
                pl.BlockSpec((tf, d), lambda i, j: (jnp.minimum(j, n_main - 1), 0))]
    args = [x, g.reshape(1, d), wg, wu, wd]
    if has_tail:
        in_specs += [_resident((d, f_tail), lambda i, j: (0, 0)),
                     _resident((d, f_tail), lambda i, j: (0, 0)),
                     _resident((f_tail, d), lambda i, j: (0, 0))]
        args += [wg[:, f_main:], wu[:, f_main:], wd[f_main:]]
    if final_norm:
        in_specs.append(vec)
        args.append(final_g.reshape(1, d))
    return pl.pallas_call(
        functools.partial(_ffn_body, n_main=n_main, has_tail=has_tail, final_norm=final_norm),
        grid=(t // tm, n_main + has_tail),
        in_specs=in_specs,
        out_specs=row,
        out_shape=jax.ShapeDtypeStruct((t, d), F32),
        scratch_shapes=[pltpu.VMEM((tm, d), BF16)],
        compiler_params=_params(("parallel", "arbitrary"), 56),
        name="ffn_final" if final_norm else "ffn",
    )(*args)


def _rope(x, cos2, sin2):
    return x * cos2 + pltpu.roll(x, HEAD_DIM // 2, 1) * sin2


def _in_proj_body(x_ref, g_ref, w_ref, wgate_ref, cos_ref, sin_ref, blk_ref,
                  u_ref, qraw_ref, qrot_ref, kc_ref, vc_ref, k_ref, v_ref, gate_ref, kv_scr):
    h = _rms(x_ref[...], g_ref[...]).astype(BF16)
    cos2 = cos_ref[...]
    sin2 = sin_ref[...]
    scale = HEAD_DIM ** -0.5

    def proj(c0, width):
        return jnp.dot(h, w_ref[:, c0:c0 + width], preferred_element_type=F32)

    glu = proj(0, 2 * CONV_CH)
    u_ref[...] = glu[:, :CONV_CH] * jax.nn.sigmoid(glu[:, CONV_CH:])

    def head(cols, i):
        return cols[:, i * HEAD_DIM:(i + 1) * HEAD_DIM]

    c0 = 2 * CONV_CH
    for gk in range(N_KV):
        cols = proj(c0, GROUP * HEAD_DIM) * (scale * LOG2_E)
        for r in range(GROUP):
            qh = head(cols, r)
            qraw_ref[0, gk * GROUP + r] = qh.astype(BF16)
            qrot_ref[0, gk * GROUP + r] = _rope(qh, cos2, sin2).astype(BF16)
        c0 += GROUP * HEAD_DIM
    cols = proj(c0, 2 * KV_DIM)
    n_unit = kv_scr.shape[1] // CMP_STRIDE
    for ref, first in ((kc_ref, 0), (vc_ref, N_KV)):
        for gk in range(N_KV):
            kv_scr[first + gk] = head(cols, first + gk)
            for slot in range(CMP_STRIDE):
                ref[0, gk, :, slot * HEAD_DIM:(slot + 1) * HEAD_DIM] = kv_scr[
                    first + gk, pl.ds(slot, n_unit, stride=CMP_STRIDE), :]
    c0 += 2 * KV_DIM
    lo, hi = slice(0, HEAD_DIM), slice(HEAD_DIM, 2 * HEAD_DIM)
    for branch in (SLC, WIN):
        cols = proj(c0, 2 * KV_DIM)
        for gk in range(N_KV):
            k_ref[0, gk, branch, :, lo] = _rope(head(cols, gk), cos2, sin2).astype(BF16)
            v_ref[0, gk, branch, :, lo] = head(cols, N_KV + gk).astype(BF16)
        c0 += 2 * KV_DIM
    cols = jax.nn.sigmoid(jnp.dot(h, wgate_ref[...], preferred_element_type=F32))
    for gk in range(N_KV):
        k_ref[0, gk, SLC, :, hi] = blk_ref[...]
        k_ref[0, gk, WIN, :, hi] = jnp.zeros(blk_ref.shape, BF16)
        v_ref[0, gk, SLC, :, hi] = jnp.ones(blk_ref.shape, BF16)
        v_ref[0, gk, WIN, :, hi] = jnp.ones(blk_ref.shape, BF16)
        gate_ref[0, gk] = head(cols, gk)


def _in_proj(x, g, w_in, batch, seq):
    t, d = x.shape
    tm = min(256, seq)
    n_s = seq // tm
    main = 2 * CONV_CH + Q_DIM + 6 * KV_DIM
    gate_w = w_in[:, main:].reshape(d, N_KV, GROUP * 3)
    gate_w = jnp.pad(gate_w, ((0, 0), (0, 0), (0, V7X_LANES - GROUP * 3))).reshape(d, N_KV * V7X_LANES)
    gate_w = gate_w.astype(BF16)
    w = w_in[:, :main].astype(BF16)

    inv = jnp.power(ROPE_THETA, -jnp.arange(0, HEAD_DIM, 2, dtype=F32) / HEAD_DIM)
    ang = jnp.arange(seq, dtype=F32)[:, None] * inv[None, :]
    cos2 = jnp.concatenate([jnp.cos(ang), jnp.cos(ang)], axis=1)
    sin2 = jnp.concatenate([-jnp.sin(ang), jnp.sin(ang)], axis=1)

    key_blk = np.arange(seq)[:, None] // SLC_BLOCK == np.arange(seq // SLC_BLOCK)[None, :]
    blk_mask = jnp.asarray(np.where(key_blk, -(2.0 ** 100), 0.0), BF16)

    def heads(n, dtype):
        return (jax.ShapeDtypeStruct((batch, n, seq, HEAD_DIM), dtype),
                pl.BlockSpec((1, n, tm, HEAD_DIM), lambda i: (i // n_s, 0, i % n_s, 0)))

    stacked = (jax.ShapeDtypeStruct((batch, N_KV, 2, seq, 2 * HEAD_DIM), BF16),
               pl.BlockSpec((1, N_KV, 2, tm, 2 * HEAD_DIM), lambda i: (i // n_s, 0, 0, i % n_s, 0)))
    unit = CMP_STRIDE * HEAD_DIM
    units = (jax.ShapeDtypeStruct((batch, N_KV, seq // CMP_STRIDE, unit), F32),
             pl.BlockSpec((1, N_KV, tm // CMP_STRIDE, unit), lambda i: (i // n_s, 0, i % n_s, 0)))
    outs = [(jax.ShapeDtypeStruct((t, CONV_CH), F32), pl.BlockSpec((tm, CONV_CH), lambda i: (i, 0))),
            heads(N_HEADS, BF16), heads(N_HEADS, BF16),
            units, units,
            stacked, stacked,
            heads(N_KV, F32)]
    table = pl.BlockSpec((tm, HEAD_DIM), lambda i: (i % n_s, 0))
    return pl.pallas_call(
        _in_proj_body,
        grid=(t // tm,),
        in_specs=[pl.BlockSpec((tm, d), lambda i: (i, 0)),
                  pl.BlockSpec((1, d), lambda i: (0, 0)),
                  _resident((d, main), lambda i: (0, 0)),
                  _resident(gate_w.shape, lambda i: (0, 0)),
                  table, table, table],
        out_specs=[o[1] for o in outs],
        out_shape=[o[0] for o in outs],
        scratch_shapes=[pltpu.VMEM((2 * N_KV, tm, HEAD_DIM), F32)],
        compiler_params=_params(("parallel",), 48),
        name="in_proj",
    )(x, g.reshape(1, d), w, gate_w, cos2, sin2, blk_mask)


CONV_ROWS = 512


def _conv_body(u_ref, halo_ref, dw_ref, db_ref, lg_ref, lb_ref, pw_ref, og_ref, o_ref, ext_scr, y_scr, *, ts):
    i = pl.program_id(1)
    ext_scr[0, 0:CONV_HALO, :] = jnp.where(i == 0, 0.0, halo_ref[0])
    ext_scr[0, CONV_HALO:CONV_HALO + ts, :] = u_ref[0]
    moved = CONV_HALO + ts - V7X_SUBLANES
    for s in range(1, V7X_SUBLANES):
        ext_scr[s, 0:moved, :] = ext_scr[0, s:s + moved, :]
    first = CONV_HALO - (CONV_K - 1)
    for c in range(ts // CONV_ROWS):
        r0 = c * CONV_ROWS
        acc = jnp.broadcast_to(db_ref[...], (CONV_ROWS, CONV_CH))
        for k in range(CONV_K):
            s, base = (first + k) % V7X_SUBLANES, (first + k) // V7X_SUBLANES * V7X_SUBLANES
            acc = acc + dw_ref[k:k + 1, :] * ext_scr[s, r0 + base:r0 + base + CONV_ROWS, :]
        mu = jnp.mean(acc, axis=-1, keepdims=True)
        xc = acc - mu
        var = jnp.mean(xc * xc, axis=-1, keepdims=True)
        y = xc * lax.rsqrt(var + EPS) * lg_ref[...] + lb_ref[...]
        y_scr[r0:r0 + CONV_ROWS, :] = (y * jax.nn.sigmoid(y)).astype(BF16)
    z = jnp.dot(y_scr[...], pw_ref[...], preferred_element_type=F32)
    o_ref[0] = _rms(z, og_ref[...]).astype(BF16)


def _conv(u, dw_w, dw_b, ln_g, ln_b, pw_w, out_g):
    b, s, c = u.shape
    ts = min(512, s)
    per = ts // CONV_HALO
    vec = pl.BlockSpec((1, c), lambda bi, i: (0, 0))
    return pl.pallas_call(
        functools.partial(_conv_body, ts=ts),
        grid=(b, s // ts),
        in_specs=[pl.BlockSpec((1, ts, c), lambda bi, i: (bi, i, 0)),
                  pl.BlockSpec((1, CONV_HALO, c), lambda bi, i: (bi, jnp.maximum(i * per - 1, 0), 0)),
                  pl.BlockSpec((CONV_K, c), lambda bi, i: (0, 0)),
                  vec, vec, vec,
                  pl.BlockSpec((c, c), lambda bi, i: (0, 0)),
                  vec],
        out_specs=pl.BlockSpec((1, ts, c), lambda bi, i: (bi, i, 0)),
        out_shape=jax.ShapeDtypeStruct((b, s, c), BF16),
        scratch_shapes=[pltpu.VMEM((V7X_SUBLANES, CONV_HALO + ts, c), F32), pltpu.VMEM((ts, c), BF16)],
        compiler_params=_params(("parallel", "parallel"), 32),
        name="conv",
    )(u, u, dw_w, dw_b.reshape(1, c), ln_g.reshape(1, c), ln_b.reshape(1, c), pw_w.astype(BF16),
      out_g.reshape(1, c))


def _compress_one(u_ref, pos_ref, w1_ref, w2_ref, o_ref):
    u = u_ref[0, 0]
    half = u.shape[1]
    nu = u.shape[0]
    top = jnp.dot((u + pos_ref[0:1, :]).astype(BF16), w1_ref[0:half, :], preferred_element_type=F32)
    bot = jnp.dot((u + pos_ref[1:2, :]).astype(BF16), w1_ref[half:2 * half, :], preferred_element_type=F32)
    hid = top + pltpu.roll(bot, nu - 1, 0)
    hid = hid * jax.nn.sigmoid(hid)
    o_ref[0, 0] = jnp.dot(hid.astype(BF16), w2_ref[...], preferred_element_type=F32).astype(BF16)


def _compress_body(uk_ref, uv_ref, pk_ref, pv_ref, kw1_ref, kw2_ref, vw1_ref, vw2_ref, ok_ref, ov_ref):
    _compress_one(uk_ref, pk_ref, kw1_ref, kw2_ref, ok_ref)
    _compress_one(uv_ref, pv_ref, vw1_ref, vw2_ref, ov_ref)


def _compress(kc, vc, pos_k, pos_v, kw1, kw2, vw1, vw2):
    b, g, nu, unit = kc.shape
    dh = unit // CMP_STRIDE
    pos = lambda p: p.reshape(CMP_BLOCK // CMP_STRIDE, unit)
    u_spec = pl.BlockSpec((1, 1, nu, unit), lambda bi, gi: (bi, gi, 0, 0))
    full = lambda shape: pl.BlockSpec(shape, lambda bi, gi: (0,) * len(shape))
    o_spec = pl.BlockSpec((1, 1, nu, dh), lambda bi, gi: (bi, gi, 0, 0))
    o_shape = jax.ShapeDtypeStruct((b, g, nu, dh), BF16)
    return pl.pallas_call(
        _compress_body,
        grid=(b, g),
        in_specs=[u_spec, u_spec, full((2, unit)), full((2, unit)),
                  full((CMP_BLOCK * dh, CMP_HIDDEN)), full((CMP_HIDDEN, dh)),
                  full((CMP_BLOCK * dh, CMP_HIDDEN)), full((CMP_HIDDEN, dh))],
        out_specs=[o_spec, o_spec],
        out_shape=[o_shape, o_shape],
        compiler_params=_params(("parallel", "parallel"), 40),
        name="compress",
    )(kc, vc, pos(pos_k), pos(pos_v),
      kw1.astype(BF16), kw2.astype(BF16), vw1.astype(BF16), vw2.astype(BF16))


def _cmp_select_prefix(q_ref, kc_ref, vc_ref, ct_ref, ocmp_ref, sel_ref, *, tq, nc, nb):
    for sub in range(tq // CMP_SUB):
        _cmp_select_sub(q_ref, kc_ref, vc_ref, ct_ref, ocmp_ref, sel_ref,
                        row0=sub * CMP_SUB, q0=pl.program_id(2) * tq + sub * CMP_SUB, tq=CMP_SUB, nc=nc, nb=nb)


def _cmp_select_sub(q_ref, kc_ref, vc_ref, ct_ref, ocmp_ref, sel_ref, *, row0, q0, tq, nc, nb):
    n_slc = ct_ref.shape[0]
    q = q_ref[0, :, row0:row0 + tq, :].reshape(GROUP * tq, HEAD_DIM)
    s = lax.dot_general(q, kc_ref[0, 0, 0:nc, :], _NT, preferred_element_type=F32).reshape(GROUP, tq, nc)
    t = q0 + lax.broadcasted_iota(jnp.int32, (tq, nc), 0)
    cmp_end = lax.broadcasted_iota(jnp.int32, (tq, nc), 1) * CMP_STRIDE + (CMP_BLOCK - 1)
    s = s + jnp.where(cmp_end <= t, 0.0, MASKED)[None]
    m = jnp.max(s, axis=-1, keepdims=True)
    m = jnp.where(m > 0.5 * MASKED, m, 0.0)
    e = jnp.exp2(s - m)
    p = e * (1.0 / jnp.maximum(jnp.sum(e, axis=-1, keepdims=True), 1e-30))
    o = jnp.dot(p.reshape(GROUP * tq, nc).astype(BF16), vc_ref[0, 0, 0:nc, :], preferred_element_type=F32)
    ocmp_ref[0, :, row0:row0 + tq, :] = o.reshape(GROUP, tq, HEAD_DIM)

    psum = p[0] + p[1] + p[2] + p[3]
    hi = psum.astype(BF16)
    r1 = psum - hi.astype(F32)
    mid = r1.astype(BF16)
    lo = (r1 - mid.astype(F32)).astype(BF16)
    ct = ct_ref[0:nb, 0:nc]
    imp = (lax.dot_general(ct, hi, _NT, preferred_element_type=F32)
           + lax.dot_general(ct, mid, _NT, preferred_element_type=F32)
           + lax.dot_general(ct, lo, _NT, preferred_element_type=F32))

    blk = lax.broadcasted_iota(jnp.int32, (nb, tq), 0)
    jt = (q0 + lax.broadcasted_iota(jnp.int32, (nb, tq), 1)) // SLC_BLOCK
    forced = (blk == 0) | (blk == jt) | (blk == jt - 1)
    candidate = (blk >= 1) & (blk < jt - 1)
    val = jnp.where(candidate, imp, -1.0)
    blk_f = blk.astype(F32)
    for _ in range(N_SELECT - N_FORCED):
        best = jnp.max(val, axis=0, keepdims=True)
        first = jnp.min(jnp.where(val == best, blk_f, float(n_slc)), axis=0, keepdims=True)
        val = jnp.where(blk_f == first, -1.0, val)
    unsel = jnp.where(forced | (candidate & (val < 0.0)), 0.0, 1.0)
    if nb < n_slc:
        unsel = jnp.concatenate([unsel, jnp.ones((n_slc - nb, tq), F32)], axis=0)
    sel_ref[0, 0, row0:row0 + tq, :] = unsel.T.astype(BF16)


def _cmp_select_body(q_ref, kc_ref, vc_ref, ct_ref, ocmp_ref, sel_ref, *, tq):
    nu = kc_ref.shape[2]
    n_slc = ct_ref.shape[0]
    per_slc = nu // n_slc
    visible = (pl.program_id(2) * tq + tq - CMP_BLOCK) // CMP_STRIDE + 1
    n_prefix = nu // V7X_LANES
    need = jnp.clip((visible + V7X_LANES - 1) // V7X_LANES, 1, n_prefix)
    for v in range(1, n_prefix + 1):
        nc = v * V7X_LANES
        pl.when(need == v)(functools.partial(
            _cmp_select_prefix, q_ref, kc_ref, vc_ref, ct_ref, ocmp_ref, sel_ref, tq=tq, nc=nc, nb=nc // per_slc))


def _cmp_to_slc_t(nu, n_slc):
    per_slc = SLC_BLOCK // CMP_STRIDE
    c = np.arange(nu)[None, :]
    j = np.arange(n_slc)[:, None]
    m = np.zeros((n_slc, nu), np.float32)
    for unit in range(CMP_BLOCK // CMP_STRIDE):
        m += ((c + unit) // per_slc == j)
    m[:, nu - 1] = 0.0
    return jnp.asarray(m, BF16)


def _cmp_select(q_raw, k_cmp, v_cmp):
    b, _, s, dh = q_raw.shape
    nu = k_cmp.shape[2]
    n_slc = s // SLC_BLOCK
    tq = 1024
    for q_end in range(tq, s + 1, tq):
        prefix = -(-((q_end - CMP_BLOCK) // CMP_STRIDE + 1) // V7X_LANES) * V7X_LANES
        assert prefix * n_slc // nu >= (q_end - 1) // SLC_BLOCK + 1
    kv_spec = pl.BlockSpec((1, 1, nu, dh), lambda bi, gi, qi: (bi, gi, 0, 0))
    return pl.pallas_call(
        functools.partial(_cmp_select_body, tq=tq),
        grid=(b, N_KV, s // tq),
        in_specs=[pl.BlockSpec((1, GROUP, tq, dh), lambda bi, gi, qi: (bi, gi, qi, 0)),
                  kv_spec, kv_spec,
                  pl.BlockSpec((n_slc, nu), lambda bi, gi, qi: (0, 0))],
        out_specs=[pl.BlockSpec((1, GROUP, tq, dh), lambda bi, gi, qi: (bi, gi, qi, 0)),
                   pl.BlockSpec((1, 1, tq, n_slc), lambda bi, gi, qi: (bi, gi, qi, 0))],
        out_shape=[jax.ShapeDtypeStruct((b, N_HEADS, s, dh), F32),
                   jax.ShapeDtypeStruct((b, N_KV, s, n_slc), BF16)],
        compiler_params=_params(("parallel", "parallel", "parallel"), 32),
        name="cmp_select",
    )(q_raw, k_cmp, v_cmp, _cmp_to_slc_t(nu, n_slc))


LOWER, UPPER = 0, 1


def _attend_body(q_ref, k_ref, v_ref, tri_ref, unsel_ref, ocmp_ref, g_ref, o_ref,
                 qa_scr, s_scr, m_scr, acc_scr, *, tq, tk):
    q0 = pl.program_id(2) * tq
    rows = GROUP * tq

    qa_scr[:, 0:HEAD_DIM] = q_ref[0].reshape(rows, HEAD_DIM)
    qa_scr[:, HEAD_DIM:2 * HEAD_DIM] = jnp.concatenate([unsel_ref[0, 0]] * GROUP, axis=0)
    m_scr[...] = jnp.full(m_scr.shape, MASKED, F32)
    acc_scr[...] = jnp.zeros(acc_scr.shape, F32)

    last = q0 // tk
    n_win = jnp.minimum(last + 1, WINDOW // tk + 1)
    n_tiles = last + 1 + n_win

    def tile(i):
        branch = (i > last).astype(jnp.int32)
        kt = i - branch * n_win
        return branch, kt, pl.multiple_of(kt * tk, tk)

    def scores(i):
        branch, _, k0 = tile(i)
        s_scr[...] = lax.dot_general(qa_scr[...], k_ref[0, 0, branch, pl.ds(k0, tk), :], _NT,
                                     preferred_element_type=F32)

    def softmax_pv(i, boundary):
        branch, kt, k0 = tile(i)
        if boundary:
            kind = jnp.where(kt == last, LOWER, UPPER)
            s = (s_scr[...].reshape(GROUP, tq, tk) + tri_ref[kind][None]).reshape(rows, tk)
        else:
            s = s_scr[...]
        m_old = m_scr[branch]
        m_new = jnp.maximum(m_old, jnp.max(s, axis=-1, keepdims=True))
        p = jnp.exp2(s - jnp.concatenate([m_new] * (tk // V7X_LANES), axis=1)).astype(BF16)
        pv = jnp.dot(p, v_ref[0, 0, branch, pl.ds(k0, tk), :], preferred_element_type=F32)
        alpha = jnp.exp2(m_old - m_new)
        acc_scr[branch] = jnp.concatenate([alpha] * (2 * HEAD_DIM // V7X_LANES), axis=1) * acc_scr[branch] + pv
        m_scr[branch] = m_new

    scores(0)

    def interior_step(i, carry):
        softmax_pv(i, False)
        scores(i + 1)
        return carry

    def boundary_step(i, carry):
        softmax_pv(i, True)
        scores(i + 1)
        return carry

    lax.fori_loop(0, last, interior_step, 0)
    lax.fori_loop(last, n_tiles - 1, boundary_step, 0)
    softmax_pv(n_tiles - 1, True)

    def normalized(branch):
        acc = acc_scr[branch]
        return acc[:, 0:HEAD_DIM] * (1.0 / acc[:, HEAD_DIM:2 * HEAD_DIM])

    gate = g_ref[0, 0]
    stacked = lambda j: jnp.concatenate([gate[:, 3 * r + j:3 * r + j + 1] for r in range(GROUP)], axis=0)
    o = (stacked(0) * ocmp_ref[0].reshape(rows, HEAD_DIM) + stacked(1) * normalized(SLC)
         + stacked(2) * normalized(WIN))
    for r in range(GROUP):
        o_ref[0, :, r * HEAD_DIM:(r + 1) * HEAD_DIM] = o[r * tq:(r + 1) * tq]


def _attend(q_rot, k_all, v_all, unsel, o_cmp, gates):
    b, _, s, dh = q_rot.shape
    n_slc = unsel.shape[3]
    assert n_slc == dh, "the unselected one-hot fills the second half of the augmented contraction"
    tq = tk = WINDOW
    row, col = np.arange(tq)[:, None], np.arange(tk)[None, :]
    tri = jnp.asarray(np.stack([np.where(col <= row, 0.0, MASKED),
                                np.where(col > row, 0.0, MASKED)]), F32)
    q_spec = pl.BlockSpec((1, GROUP, tq, dh), lambda bi, gi, qi: (bi, gi, qi, 0))
    kv_spec = _resident((1, 1, 2, s, 2 * dh), lambda bi, gi, qi: (bi, gi, 0, 0, 0))
    row_spec = lambda w: pl.BlockSpec((1, 1, tq, w), lambda bi, gi, qi: (bi, gi, qi, 0))
    rows = GROUP * tq
    return pl.pallas_call(
        functools.partial(_attend_body, tq=tq, tk=tk),
        grid=(b, N_KV, s // tq),
        in_specs=[q_spec, kv_spec, kv_spec, _resident(tri.shape, lambda bi, gi, qi: (0, 0, 0)),
                  row_spec(n_slc), q_spec, row_spec(V7X_LANES)],
        out_specs=pl.BlockSpec((1, tq, GROUP * dh), lambda bi, gi, qi: (bi, qi, gi)),
        out_shape=jax.ShapeDtypeStruct((b, s, Q_DIM), F32),
        scratch_shapes=[pltpu.VMEM((rows, 2 * dh), BF16),
                        pltpu.VMEM((rows, tk), F32),
                        pltpu.VMEM((2, rows, V7X_LANES), F32),
                        pltpu.VMEM((2, rows, 2 * dh), F32)],
        compiler_params=_params(("parallel", "parallel", "arbitrary"), 48),
        name="attend",
    )(q_rot, k_all, v_all, tri, unsel, o_cmp, gates)


def _out_proj_body(cn_ref, a_ref, x_ref, gn_ref, wc_ref, wa_ref, o_ref):
    an = _rms(a_ref[...], gn_ref[...]).astype(BF16)
    y = (jnp.dot(cn_ref[...], wc_ref[...], preferred_element_type=F32)
         + jnp.dot(an, wa_ref[...], preferred_element_type=F32))
    o_ref[...] = x_ref[...] + y


def _out_proj(conv_n, attn, x, nsa_g, w_out):
    t, d = x.shape
    tm = min(512, t)
    wc = w_out[:CONV_CH].astype(BF16)
    wa = w_out[CONV_CH:].astype(BF16)
    return pl.pallas_call(
        _out_proj_body,
        grid=(t // tm,),
        in_specs=[pl.BlockSpec((tm, CONV_CH), lambda i: (i, 0)),
                  pl.BlockSpec((tm, Q_DIM), lambda i: (i, 0)),
                  pl.BlockSpec((tm, d), lambda i: (i, 0)),
                  pl.BlockSpec((1, Q_DIM), lambda i: (0, 0)),
                  _resident((CONV_CH, d), lambda i: (0, 0)),
                  _resident((Q_DIM, d), lambda i: (0, 0))],
        out_specs=pl.BlockSpec((tm, d), lambda i: (i, 0)),
        out_shape=jax.ShapeDtypeStruct((t, d), F32),
        compiler_params=_params(("parallel",), 40),
        name="out_proj",
    )(conv_n, attn, x, nsa_g.reshape(1, Q_DIM), wc, wa)


def kernel(x, ffn1_norm, ffn1_w_gate, ffn1_w_up, ffn1_w_down, mix_norm, w_in, cmp_pos_k, cmp_pos_v, cmp_k_w1, cmp_k_w2, cmp_v_w1, cmp_v_w2, conv_dw_w, conv_dw_b, conv_ln_g, conv_ln_b, conv_pw_w, out_norm_conv, out_norm_nsa, w_out, ffn2_norm, ffn2_w_gate, ffn2_w_up, ffn2_w_down, final_norm):
    b, s, d = x.shape
    assert s % (SLC_BLOCK * V7X_LANES) == 0, "selection blocks must fill whole 128-lane rows"
    depth = ffn1_norm.shape[0]
    y = x.reshape(b * s, d)
    for l in range(depth):
        y = _ffn(y, ffn1_norm[l], ffn1_w_gate[l], ffn1_w_up[l], ffn1_w_down[l])
        u, q_raw, q_rot, kc, vc, k_all, v_all, gates = _in_proj(y, mix_norm[l], w_in[l], b, s)
        conv_n = _conv(u.reshape(b, s, CONV_CH), conv_dw_w[l], conv_dw_b[l], conv_ln_g[l], conv_ln_b[l],
                       conv_pw_w[l], out_norm_conv[l])
        k_cmp, v_cmp = _compress(kc, vc, cmp_pos_k[l], cmp_pos_v[l],
                                 cmp_k_w1[l], cmp_k_w2[l], cmp_v_w1[l], cmp_v_w2[l])
        o_cmp, unsel = _cmp_select(q_raw, k_cmp, v_cmp)
        attn = _attend(q_rot, k_all, v_all, unsel, o_cmp, gates)
        y = _out_proj(conv_n.reshape(b * s, CONV_CH), attn.reshape(b * s, Q_DIM), y, out_norm_nsa[l], w_out[l])
        y = _ffn(y, ffn2_norm[l], ffn2_w_gate[l], ffn2_w_up[l], ffn2_w_down[l],
                 final_g=final_norm if l == depth - 1 else None)
    return y.reshape(b, s, d)
```

```python
import functools

import numpy as np
import jax
import jax.numpy as jnp
from jax import lax
from jax.experimental import pallas as pl
from jax.experimental.pallas import tpu as pltpu

F32 = jnp.float32
BF16 = jnp.bfloat16

V7X_LANES = 128
V7X_SUBLANES = 8

CONV_CH = 512
N_HEADS = 12
HEAD_DIM = 128
N_KV = 3
GROUP = N_HEADS // N_KV
CONV_K = 31
CMP_BLOCK = 32
CMP_STRIDE = 16
CMP_HIDDEN = 256
SLC_BLOCK = 64
N_SELECT = 16
N_FORCED = 3
WINDOW = 512
ROPE_THETA = 10000.0
EPS = 1e-6
MASKED = -1e30
LOG2_E = 1.4426950408889634
SLC, WIN = 0, 1

KV_DIM = N_KV * HEAD_DIM
Q_DIM = N_HEADS * HEAD_DIM
CONV_HALO = 32

_NT = (((1,), (1,)), ((), ()))


def _rms(x, g):
    return x * lax.rsqrt(jnp.mean(x * x, axis=-1, keepdims=True) + EPS) * g


def _params(semantics, vmem_mib):
    return pltpu.CompilerParams(dimension_semantics=semantics, vmem_limit_bytes=vmem_mib * 2 ** 20)


def _resident(shape, index_map):
    return pl.BlockSpec(shape, index_map, pipeline_mode=pl.Buffered(1))


def _ffn_body(*refs, n_main, has_tail, final_norm):
    refs = list(refs)
    x_ref, g_ref, wg_ref, wu_ref, wd_ref = refs[:5]
    del refs[:5]
    if has_tail:
        tail_refs = refs[:3]
        del refs[:3]
    if final_norm:
        fg_ref = refs.pop(0)
    o_ref, h_scr = refs
    j = pl.program_id(1)

    @pl.when(j == 0)
    def _():
        h_scr[...] = _rms(x_ref[...], g_ref[...]).astype(BF16)

    def hidden_slab(wg, wu, wd, first=False):
        h = h_scr[...]
        a = jnp.dot(h, wg[...], preferred_element_type=F32)
        b = jnp.dot(h, wu[...], preferred_element_type=F32)
        z = (a * jax.nn.sigmoid(a) * b).astype(BF16)
        base = x_ref[...] if first else o_ref[...]
        o_ref[...] = base + 0.5 * jnp.dot(z, wd[...], preferred_element_type=F32)

    main = functools.partial(hidden_slab, wg_ref, wu_ref, wd_ref)
    pl.when(j == 0)(functools.partial(main, first=True))
    if has_tail:
        pl.when((j > 0) & (j < n_main))(main)
        pl.when(j == n_main)(functools.partial(hidden_slab, *tail_refs))
    else:
        pl.when(j > 0)(main)

    if final_norm:
        @pl.when(j == n_main + has_tail - 1)
        def _():
            o_ref[...] = _rms(o_ref[...], fg_ref[...])


def _ffn(x, g, w_gate, w_up, w_down, final_g=None):
    t, d = x.shape
    f = w_gate.shape[1]
    tm = min(512, t)
    tf = 512
    n_main, f_tail = divmod(f, tf)
    assert f_tail % V7X_LANES == 0 and n_main >= 1
    has_tail = int(f_tail > 0)
    f_main = n_main * tf
    wg, wu, wd = w_gate.astype(BF16), w_up.astype(BF16), w_down.astype(BF16)
    final_norm = final_g is not None
    row = pl.BlockSpec((tm, d), lambda i, j: (i, 0))
    vec = pl.BlockSpec((1, d), lambda i, j: (0, 0))
    slab = lambda i, j: (0, jnp.minimum(j, n_main - 1))
    in_specs = [row, vec,
                pl.BlockSpec((d, tf), slab),
                pl.BlockSpec((d, tf), slab),
                pl.BlockSpec((tf, d), lambda i, j: (jnp.minimum(j, n_main - 1), 0))]
    args = [x, g.reshape(1, d), wg, wu, wd]
    if has_tail:
        in_specs += [_resident((d, f_tail), lambda i, j: (0, 0)),
                     _resident((d, f_tail), lambda i, j: (0, 0)),
                     _resident((f_tail, d), lambda i, j: (0, 0))]
        args += [wg[:, f_main:], wu[:, f_main:], wd[f_main:]]
    if final_norm:
        in_specs.append(vec)
        args.append(final_g.reshape(1, d))
    return pl.pallas_call(
        functools.partial(_ffn_body, n_main=n_main, has_tail=has_tail, final_norm=final_norm),
        grid=(t // tm, n_main + has_tail),
        in_specs=in_specs,
        out_specs=row,
        out_shape=jax.ShapeDtypeStruct((t, d), F32),
        scratch_shapes=[pltpu.VMEM((tm, d), BF16)],
        compiler_params=_params(("parallel", "arbitrary"), 48),
        name="ffn_final" if final_norm else "ffn",
    )(*args)


def _rope(x, cos2, sin2):
    return x * cos2 + pltpu.roll(x, HEAD_DIM // 2, 1) * sin2


def _in_proj_body(x_ref, g_ref, w_ref, wgate_ref, cos_ref, sin_ref, blk_ref,
                  u_ref, qraw_ref, qrot_ref, kc_ref, vc_ref, k_ref, v_ref, gate_ref, kv_scr):
    h = _rms(x_ref[...], g_ref[...]).astype(BF16)
    cos2 = cos_ref[...]
    sin2 = sin_ref[...]
    scale = HEAD_DIM ** -0.5

    def proj(c0, width):
        return jnp.dot(h, w_ref[:, c0:c0 + width], preferred_element_type=F32)

    glu = proj(0, 2 * CONV_CH)
    u_ref[...] = glu[:, :CONV_CH] * jax.nn.sigmoid(glu[:, CONV_CH:])

    def head(cols, i):
        return cols[:, i * HEAD_DIM:(i + 1) * HEAD_DIM]

    c0 = 2 * CONV_CH
    for gk in range(N_KV):
        cols = proj(c0, GROUP * HEAD_DIM) * (scale * LOG2_E)
        for r in range(GROUP):
            qh = head(cols, r)
            qraw_ref[0, gk * GROUP + r] = qh.astype(BF16)
            qrot_ref[0, gk * GROUP + r] = _rope(qh, cos2, sin2).astype(BF16)
        c0 += GROUP * HEAD_DIM
    cols = proj(c0, 2 * KV_DIM)
    n_unit = kv_scr.shape[1] // CMP_STRIDE
    for ref, first in ((kc_ref, 0), (vc_ref, N_KV)):
        for gk in range(N_KV):
            kv_scr[first + gk] = head(cols, first + gk)
            for slot in range(CMP_STRIDE):
                ref[0, gk, :, slot * HEAD_DIM:(slot + 1) * HEAD_DIM] = kv_scr[
                    first + gk, pl.ds(slot, n_unit, stride=CMP_STRIDE), :]
    c0 += 2 * KV_DIM
    lo, hi = slice(0, HEAD_DIM), slice(HEAD_DIM, 2 * HEAD_DIM)
    for branch in (SLC, WIN):
        cols = proj(c0, 2 * KV_DIM)
        for gk in range(N_KV):
            k_ref[0, gk, branch, :, lo] = _rope(head(cols, gk), cos2, sin2).astype(BF16)
            v_ref[0, gk, branch, :, lo] = head(cols, N_KV + gk).astype(BF16)
        c0 += 2 * KV_DIM
    cols = jax.nn.sigmoid(jnp.dot(h, wgate_ref[...], preferred_element_type=F32))
    for gk in range(N_KV):
        k_ref[0, gk, SLC, :, hi] = blk_ref[...]
        k_ref[0, gk, WIN, :, hi] = jnp.zeros(blk_ref.shape, BF16)
        v_ref[0, gk, SLC, :, hi] = jnp.ones(blk_ref.shape, BF16)
        v_ref[0, gk, WIN, :, hi] = jnp.ones(blk_ref.shape, BF16)
        gate_ref[0, gk] = head(cols, gk)


def _in_proj(x, g, w_in, batch, seq):
    t, d = x.shape
    tm = min(256, seq)
    n_s = seq // tm
    main = 2 * CONV_CH + Q_DIM + 6 * KV_DIM
    gate_w = w_in[:, main:].reshape(d, N_KV, GROUP * 3)
    gate_w = jnp.pad(gate_w, ((0, 0), (0, 0), (0, V7X_LANES - GROUP * 3))).reshape(d, N_KV * V7X_LANES)
    gate_w = gate_w.astype(BF16)
    w = w_in[:, :main].astype(BF16)

    inv = jnp.power(ROPE_THETA, -jnp.arange(0, HEAD_DIM, 2, dtype=F32) / HEAD_DIM)
    ang = jnp.arange(seq, dtype=F32)[:, None] * inv[None, :]
    cos2 = jnp.concatenate([jnp.cos(ang), jnp.cos(ang)], axis=1)
    sin2 = jnp.concatenate([-jnp.sin(ang), jnp.sin(ang)], axis=1)

    key_blk = np.arange(seq)[:, None] // SLC_BLOCK == np.arange(seq // SLC_BLOCK)[None, :]
    blk_mask = jnp.asarray(np.where(key_blk, -(2.0 ** 100), 0.0), BF16)

    def heads(n, dtype):
        return (jax.ShapeDtypeStruct((batch, n, seq, HEAD_DIM), dtype),
                pl.BlockSpec((1, n, tm, HEAD_DIM), lambda i: (i // n_s, 0, i % n_s, 0)))

    stacked = (jax.ShapeDtypeStruct((batch, N_KV, 2, seq, 2 * HEAD_DIM), BF16),
               pl.BlockSpec((1, N_KV, 2, tm, 2 * HEAD_DIM), lambda i: (i // n_s, 0, 0, i % n_s, 0)))
    unit = CMP_STRIDE * HEAD_DIM
    units = (jax.ShapeDtypeStruct((batch, N_KV, seq // CMP_STRIDE, unit), F32),
             pl.BlockSpec((1, N_KV, tm // CMP_STRIDE, unit), lambda i: (i // n_s, 0, i % n_s, 0)))
    outs = [(jax.ShapeDtypeStruct((t, CONV_CH), F32), pl.BlockSpec((tm, CONV_CH), lambda i: (i, 0))),
            heads(N_HEADS, BF16), heads(N_HEADS, BF16),
            units, units,
            stacked, stacked,
            heads(N_KV, F32)]
    table = pl.BlockSpec((tm, HEAD_DIM), lambda i: (i % n_s, 0))
    return pl.pallas_call(
        _in_proj_body,
        grid=(t // tm,),
        in_specs=[pl.BlockSpec((tm, d), lambda i: (i, 0)),
                  pl.BlockSpec((1, d), lambda i: (0, 0)),
                  _resident((d, main), lambda i: (0, 0)),
                  _resident(gate_w.shape, lambda i: (0, 0)),
                  table, table, table],
        out_specs=[o[1] for o in outs],
        out_shape=[o[0] for o in outs],
        scratch_shapes=[pltpu.VMEM((2 * N_KV, tm, HEAD_DIM), F32)],
        compiler_params=_params(("parallel",), 48),
        name="in_proj",
    )(x, g.reshape(1, d), w, gate_w, cos2, sin2, blk_mask)


CONV_ROWS = 512


def _conv_body(u_ref, halo_ref, dw_ref, db_ref, lg_ref, lb_ref, pw_ref, og_ref, o_ref, ext_scr, y_scr, *, ts):
    i = pl.program_id(1)
    ext_scr[0, 0:CONV_HALO, :] = jnp.where(i == 0, 0.0, halo_ref[0])
    ext_scr[0, CONV_HALO:CONV_HALO + ts, :] = u_ref[0]
    moved = CONV_HALO + ts - V7X_SUBLANES
    for s in range(1, V7X_SUBLANES):
        ext_scr[s, 0:moved, :] = ext_scr[0, s:s + moved, :]
    first = CONV_HALO - (CONV_K - 1)
    for c in range(ts // CONV_ROWS):
        r0 = c * CONV_ROWS
        acc = jnp.broadcast_to(db_ref[...], (CONV_ROWS, CONV_CH))
        for k in range(CONV_K):
            s, base = (first + k) % V7X_SUBLANES, (first + k) // V7X_SUBLANES * V7X_SUBLANES
            acc = acc + dw_ref[k:k + 1, :] * ext_scr[s, r0 + base:r0 + base + CONV_ROWS, :]
        mu = jnp.mean(acc, axis=-1, keepdims=True)
        xc = acc - mu
        var = jnp.mean(xc * xc, axis=-1, keepdims=True)
        y = xc * lax.rsqrt(var + EPS) * lg_ref[...] + lb_ref[...]
        y_scr[r0:r0 + CONV_ROWS, :] = (y * jax.nn.sigmoid(y)).astype(BF16)
    z = jnp.dot(y_scr[...], pw_ref[...], preferred_element_type=F32)
    o_ref[0] = _rms(z, og_ref[...]).astype(BF16)


def _conv(u, dw_w, dw_b, ln_g, ln_b, pw_w, out_g):
    b, s, c = u.shape
    ts = min(512, s)
    per = ts // CONV_HALO
    vec = pl.BlockSpec((1, c), lambda bi, i: (0, 0))
    return pl.pallas_call(
        functools.partial(_conv_body, ts=ts),
        grid=(b, s // ts),
        in_specs=[pl.BlockSpec((1, ts, c), lambda bi, i: (bi, i, 0)),
                  pl.BlockSpec((1, CONV_HALO, c), lambda bi, i: (bi, jnp.maximum(i * per - 1, 0), 0)),
                  pl.BlockSpec((CONV_K, c), lambda bi, i: (0, 0)),
                  vec, vec, vec,
                  pl.BlockSpec((c, c), lambda bi, i: (0, 0)),
                  vec],
        out_specs=pl.BlockSpec((1, ts, c), lambda bi, i: (bi, i, 0)),
        out_shape=jax.ShapeDtypeStruct((b, s, c), BF16),
        scratch_shapes=[pltpu.VMEM((V7X_SUBLANES, CONV_HALO + ts, c), F32), pltpu.VMEM((ts, c), BF16)],
        compiler_params=_params(("parallel", "parallel"), 32),
        name="conv",
    )(u, u, dw_w, dw_b.reshape(1, c), ln_g.reshape(1, c), ln_b.reshape(1, c), pw_w.astype(BF16),
      out_g.reshape(1, c))


def _compress_one(u_ref, pos_ref, w1_ref, w2_ref, o_ref):
    u = u_ref[0, 0]
    half = u.shape[1]
    nu = u.shape[0]
    top = jnp.dot((u + pos_ref[0:1, :]).astype(BF16), w1_ref[0:half, :], preferred_element_type=F32)
    bot = jnp.dot((u + pos_ref[1:2, :]).astype(BF16), w1_ref[half:2 * half, :], preferred_element_type=F32)
    hid = top + pltpu.roll(bot, nu - 1, 0)
    hid = hid * jax.nn.sigmoid(hid)
    o_ref[0, 0] = jnp.dot(hid.astype(BF16), w2_ref[...], preferred_element_type=F32).astype(BF16)


def _compress_body(uk_ref, uv_ref, pk_ref, pv_ref, kw1_ref, kw2_ref, vw1_ref, vw2_ref, ok_ref, ov_ref):
    _compress_one(uk_ref, pk_ref, kw1_ref, kw2_ref, ok_ref)
    _compress_one(uv_ref, pv_ref, vw1_ref, vw2_ref, ov_ref)


def _compress(kc, vc, pos_k, pos_v, kw1, kw2, vw1, vw2):
    b, g, nu, unit = kc.shape
    dh = unit // CMP_STRIDE
    pos = lambda p: p.reshape(CMP_BLOCK // CMP_STRIDE, unit)
    u_spec = pl.BlockSpec((1, 1, nu, unit), lambda bi, gi: (bi, gi, 0, 0))
    full = lambda shape: pl.BlockSpec(shape, lambda bi, gi: (0,) * len(shape))
    o_spec = pl.BlockSpec((1, 1, nu, dh), lambda bi, gi: (bi, gi, 0, 0))
    o_shape = jax.ShapeDtypeStruct((b, g, nu, dh), BF16)
    return pl.pallas_call(
        _compress_body,
        grid=(b, g),
        in_specs=[u_spec, u_spec, full((2, unit)), full((2, unit)),
                  full((CMP_BLOCK * dh, CMP_HIDDEN)), full((CMP_HIDDEN, dh)),
                  full((CMP_BLOCK * dh, CMP_HIDDEN)), full((CMP_HIDDEN, dh))],
        out_specs=[o_spec, o_spec],
        out_shape=[o_shape, o_shape],
        compiler_params=_params(("parallel", "parallel"), 40),
        name="compress",
    )(kc, vc, pos(pos_k), pos(pos_v),
      kw1.astype(BF16), kw2.astype(BF16), vw1.astype(BF16), vw2.astype(BF16))


def _cmp_select_prefix(q_ref, kc_ref, vc_ref, ct_ref, ocmp_ref, sel_ref, *, tq, nc, nb):
    q0 = pl.program_id(2) * tq
    n_slc = ct_ref.shape[0]
    q = q_ref[0].reshape(GROUP * tq, HEAD_DIM)
    s = lax.dot_general(q, kc_ref[0, 0, 0:nc, :], _NT, preferred_element_type=F32).reshape(GROUP, tq, nc)
    t = q0 + lax.broadcasted_iota(jnp.int32, (tq, nc), 0)
    cmp_end = lax.broadcasted_iota(jnp.int32, (tq, nc), 1) * CMP_STRIDE + (CMP_BLOCK - 1)
    s = s + jnp.where(cmp_end <= t, 0.0, MASKED)[None]
    m = jnp.max(s, axis=-1, keepdims=True)
    m = jnp.where(m > 0.5 * MASKED, m, 0.0)
    e = jnp.exp2(s - m)
    p = e * (1.0 / jnp.maximum(jnp.sum(e, axis=-1, keepdims=True), 1e-30))
    o = jnp.dot(p.reshape(GROUP * tq, nc).astype(BF16), vc_ref[0, 0, 0:nc, :], preferred_element_type=F32)
    ocmp_ref[0] = o.reshape(GROUP, tq, HEAD_DIM)

    psum = p[0] + p[1] + p[2] + p[3]
    hi = psum.astype(BF16)
    r1 = psum - hi.astype(F32)
    mid = r1.astype(BF16)
    lo = (r1 - mid.astype(F32)).astype(BF16)
    ct = ct_ref[0:nb, 0:nc]
    imp = (lax.dot_general(ct, hi, _NT, preferred_element_type=F32)
           + lax.dot_general(ct, mid, _NT, preferred_element_type=F32)
           + lax.dot_general(ct, lo, _NT, preferred_element_type=F32))

    blk = lax.broadcasted_iota(jnp.int32, (nb, tq), 0)
    jt = (q0 + lax.broadcasted_iota(jnp.int32, (nb, tq), 1)) // SLC_BLOCK
    forced = (blk == 0) | (blk == jt) | (blk == jt - 1)
    candidate = (blk >= 1) & (blk < jt - 1)
    val = jnp.where(candidate, imp, -1.0)
    blk_f = blk.astype(F32)
    for _ in range(N_SELECT - N_FORCED):
        best = jnp.max(val, axis=0, keepdims=True)
        first = jnp.min(jnp.where(val == best, blk_f, float(n_slc)), axis=0, keepdims=True)
        val = jnp.where(blk_f == first, -1.0, val)
    unsel = jnp.where(forced | (candidate & (val < 0.0)), 0.0, 1.0)
    if nb < n_slc:
        unsel = jnp.concatenate([unsel, jnp.ones((n_slc - nb, tq), F32)], axis=0)
    sel_ref[0, 0] = unsel.T.astype(BF16)


def _cmp_select_body(q_ref, kc_ref, vc_ref, ct_ref, ocmp_ref, sel_ref, *, tq):
    nu = kc_ref.shape[2]
    n_slc = ct_ref.shape[0]
    per_slc = nu // n_slc
    visible = (pl.program_id(2) * tq + tq - CMP_BLOCK) // CMP_STRIDE + 1
    n_prefix = nu // V7X_LANES
    need = jnp.clip((visible + V7X_LANES - 1) // V7X_LANES, 1, n_prefix)
    for v in range(1, n_prefix + 1):
        nc = v * V7X_LANES
        pl.when(need == v)(functools.partial(
            _cmp_select_prefix, q_ref, kc_ref, vc_ref, ct_ref, ocmp_ref, sel_ref, tq=tq, nc=nc, nb=nc // per_slc))


def _cmp_to_slc_t(nu, n_slc):
    per_slc = SLC_BLOCK // CMP_STRIDE
    c = np.arange(nu)[None, :]
    j = np.arange(n_slc)[:, None]
    m = np.zeros((n_slc, nu), np.float32)
    for unit in range(CMP_BLOCK // CMP_STRIDE):
        m += ((c + unit) // per_slc == j)
    m[:, nu - 1] = 0.0
    return jnp.asarray(m, BF16)


def _cmp_select(q_raw, k_cmp, v_cmp):
    b, _, s, dh = q_raw.shape
    nu = k_cmp.shape[2]
    n_slc = s // SLC_BLOCK
    tq = 1024
    for q_end in range(tq, s + 1, tq):
        prefix = -(-((q_end - CMP_BLOCK) // CMP_STRIDE + 1) // V7X_LANES) * V7X_LANES
        assert prefix * n_slc // nu >= (q_end - 1) // SLC_BLOCK + 1
    kv_spec = pl.BlockSpec((1, 1, nu, dh), lambda bi, gi, qi: (bi, gi, 0, 0))
    return pl.pallas_call(
        functools.partial(_cmp_select_body, tq=tq),
        grid=(b, N_KV, s // tq),
        in_specs=[pl.BlockSpec((1, GROUP, tq, dh), lambda bi, gi, qi: (bi, gi, qi, 0)),
                  kv_spec, kv_spec,
                  pl.BlockSpec((n_slc, nu), lambda bi, gi, qi: (0, 0))],
        out_specs=[pl.BlockSpec((1, GROUP, tq, dh), lambda bi, gi, qi: (bi, gi, qi, 0)),
                   pl.BlockSpec((1, 1, tq, n_slc), lambda bi, gi, qi: (bi, gi, qi, 0))],
        out_shape=[jax.ShapeDtypeStruct((b, N_HEADS, s, dh), F32),
                   jax.ShapeDtypeStruct((b, N_KV, s, n_slc), BF16)],
        compiler_params=_params(("parallel", "parallel", "parallel"), 32),
        name="cmp_select",
    )(q_raw, k_cmp, v_cmp, _cmp_to_slc_t(nu, n_slc))


LOWER, UPPER, ALL_MASKED = 0, 1, 2


def _attend_body(q_ref, ks_ref, vs_ref, kwo_ref, vwo_ref, kwn_ref, vwn_ref, tri_ref, unsel_ref, ocmp_ref, g_ref,
                 o_ref, qa_scr, s_scr, m_scr, acc_scr, *, tq, tk):
    q0 = pl.program_id(2) * tq
    rows = GROUP * tq

    qa_scr[:, 0:HEAD_DIM] = q_ref[0].reshape(rows, HEAD_DIM)
    qa_scr[:, HEAD_DIM:2 * HEAD_DIM] = jnp.concatenate([unsel_ref[0, 0]] * GROUP, axis=0)
    m_scr[...] = jnp.full(m_scr.shape, MASKED, F32)
    acc_scr[...] = jnp.zeros(acc_scr.shape, F32)

    last = q0 // tk

    def scores(k_ref, k0):
        s_scr[...] = lax.dot_general(qa_scr[...], k_ref[0, 0, 0, pl.ds(k0, tk), :], _NT,
                                     preferred_element_type=F32)

    def softmax_pv(branch, v_ref, k0, kind=None):
        if kind is not None:
            s = (s_scr[...].reshape(GROUP, tq, tk) + tri_ref[kind][None]).reshape(rows, tk)
        else:
            s = s_scr[...]
        m_old = m_scr[branch]
        m_new = jnp.maximum(m_old, jnp.max(s, axis=-1, keepdims=True))
        p = jnp.exp2(s - jnp.concatenate([m_new] * (tk // V7X_LANES), axis=1)).astype(BF16)
        pv = jnp.dot(p, v_ref[0, 0, 0, pl.ds(k0, tk), :], preferred_element_type=F32)
        alpha = jnp.exp2(m_old - m_new)
        acc_scr[branch] = jnp.concatenate([alpha] * (2 * HEAD_DIM // V7X_LANES), axis=1) * acc_scr[branch] + pv
        m_scr[branch] = m_new

    scores(ks_ref, 0)

    def interior_step(i, carry):
        softmax_pv(SLC, vs_ref, pl.multiple_of(i * tk, tk))
        scores(ks_ref, pl.multiple_of((i + 1) * tk, tk))
        return carry

    lax.fori_loop(0, last, interior_step, 0)
    softmax_pv(SLC, vs_ref, pl.multiple_of(last * tk, tk), LOWER)
    scores(kwo_ref, 0)
    softmax_pv(WIN, vwo_ref, 0, jnp.where(last > 0, UPPER, ALL_MASKED))
    scores(kwn_ref, 0)
    softmax_pv(WIN, vwn_ref, 0, LOWER)

    def normalized(branch):
        acc = acc_scr[branch]
        return acc[:, 0:HEAD_DIM] * (1.0 / acc[:, HEAD_DIM:2 * HEAD_DIM])

    gate = g_ref[0, 0]
    stacked = lambda j: jnp.concatenate([gate[:, 3 * r + j:3 * r + j + 1] for r in range(GROUP)], axis=0)
    o = (stacked(0) * ocmp_ref[0].reshape(rows, HEAD_DIM) + stacked(1) * normalized(SLC)
         + stacked(2) * normalized(WIN))
    for r in range(GROUP):
        o_ref[0, :, r * HEAD_DIM:(r + 1) * HEAD_DIM] = o[r * tq:(r + 1) * tq]


def _attend(q_rot, k_all, v_all, unsel, o_cmp, gates):
    b, _, s, dh = q_rot.shape
    n_slc = unsel.shape[3]
    assert n_slc == dh, "the unselected one-hot fills the second half of the augmented contraction"
    tq = tk = WINDOW
    row, col = np.arange(tq)[:, None], np.arange(tk)[None, :]
    tri = jnp.asarray(np.stack([np.where(col <= row, 0.0, MASKED),
                                np.where(col > row, 0.0, MASKED),
                                np.full((tq, tk), MASKED)]), F32)
    q_spec = pl.BlockSpec((1, GROUP, tq, dh), lambda bi, gi, qi: (bi, gi, qi, 0))
    slc_spec = pl.BlockSpec((1, 1, 1, s, 2 * dh), lambda bi, gi, qi: (bi, gi, SLC, 0, 0))
    old_spec = pl.BlockSpec((1, 1, 1, tk, 2 * dh), lambda bi, gi, qi: (bi, gi, WIN, jnp.maximum(qi - 1, 0), 0))
    new_spec = pl.BlockSpec((1, 1, 1, tk, 2 * dh), lambda bi, gi, qi: (bi, gi, WIN, qi, 0))
    row_spec = lambda w: pl.BlockSpec((1, 1, tq, w), lambda bi, gi, qi: (bi, gi, qi, 0))
    rows = GROUP * tq
    return pl.pallas_call(
        functools.partial(_attend_body, tq=tq, tk=tk),
        grid=(b, N_KV, s // tq),
        in_specs=[q_spec, slc_spec, slc_spec, old_spec, old_spec, new_spec, new_spec,
                  _resident(tri.shape, lambda bi, gi, qi: (0, 0, 0)),
                  row_spec(n_slc), q_spec, row_spec(V7X_LANES)],
        out_specs=pl.BlockSpec((1, tq, GROUP * dh), lambda bi, gi, qi: (bi, qi, gi)),
        out_shape=jax.ShapeDtypeStruct((b, s, Q_DIM), F32),
        scratch_shapes=[pltpu.VMEM((rows, 2 * dh), BF16),
                        pltpu.VMEM((rows, tk), F32),
                        pltpu.VMEM((2, rows, V7X_LANES), F32),
                        pltpu.VMEM((2, rows, 2 * dh), F32)],
        compiler_params=_params(("parallel", "parallel", "arbitrary"), 48),
        name="attend",
    )(q_rot, k_all, v_all, k_all, v_all, k_all, v_all, tri, unsel, o_cmp, gates)


def _out_proj_body(cn_ref, a_ref, x_ref, gn_ref, wc_ref, wa_ref, o_ref):
    an = _rms(a_ref[...], gn_ref[...]).astype(BF16)
    y = (jnp.dot(cn_ref[...], wc_ref[...], preferred_element_type=F32)
         + jnp.dot(an, wa_ref[...], preferred_element_type=F32))
    o_ref[...] = x_ref[...] + y


def _out_proj(conv_n, attn, x, nsa_g, w_out):
    t, d = x.shape
    tm = min(512, t)
    wc = w_out[:CONV_CH].astype(BF16)
    wa = w_out[CONV_CH:].astype(BF16)
    return pl.pallas_call(
        _out_proj_body,
        grid=(t // tm,),
        in_specs=[pl.BlockSpec((tm, CONV_CH), lambda i: (i, 0)),
                  pl.BlockSpec((tm, Q_DIM), lambda i: (i, 0)),
                  pl.BlockSpec((tm, d), lambda i: (i, 0)),
                  pl.BlockSpec((1, Q_DIM), lambda i: (0, 0)),
                  _resident((CONV_CH, d), lambda i: (0, 0)),
                  _resident((Q_DIM, d), lambda i: (0, 0))],
        out_specs=pl.BlockSpec((tm, d), lambda i: (i, 0)),
        out_shape=jax.ShapeDtypeStruct((t, d), F32),
        compiler_params=_params(("parallel",), 40),
        name="out_proj",
    )(conv_n, attn, x, nsa_g.reshape(1, Q_DIM), wc, wa)


def kernel(x, ffn1_norm, ffn1_w_gate, ffn1_w_up, ffn1_w_down, mix_norm, w_in, cmp_pos_k, cmp_pos_v, cmp_k_w1, cmp_k_w2, cmp_v_w1, cmp_v_w2, conv_dw_w, conv_dw_b, conv_ln_g, conv_ln_b, conv_pw_w, out_norm_conv, out_norm_nsa, w_out, ffn2_norm, ffn2_w_gate, ffn2_w_up, ffn2_w_down, final_norm):
    b, s, d = x.shape
    assert s % (SLC_BLOCK * V7X_LANES) == 0, "selection blocks must fill whole 128-lane rows"
    depth = ffn1_norm.shape[0]
    y = x.reshape(b * s, d)
    for l in range(depth):
        y = _ffn(y, ffn1_norm[l], ffn1_w_gate[l], ffn1_w_up[l], ffn1_w_down[l])
        u, q_raw, q_rot, kc, vc, k_all, v_all, gates = _in_proj(y, mix_norm[l], w_in[l], b, s)
        conv_n = _conv(u.reshape(b, s, CONV_CH), conv_dw_w[l], conv_dw_b[l], conv_ln_g[l], conv_ln_b[l],
                       conv_pw_w[l], out_norm_conv[l])
        k_cmp, v_cmp = _compress(kc, vc, cmp_pos_k[l], cmp_pos_v[l],
                                 cmp_k_w1[l], cmp_k_w2[l], cmp_v_w1[l], cmp_v_w2[l])
        o_cmp, unsel = _cmp_select(q_raw, k_cmp, v_cmp)
        attn = _attend(q_rot, k_all, v_all, unsel, o_cmp, gates)
        y = _out_proj(conv_n.reshape(b * s, CONV_CH), attn.reshape(b * s, Q_DIM), y, out_norm_nsa[l], w_out[l])
        y = _ffn(y, ffn2_norm[l], ffn2_w_gate[l], ffn2_w_up[l], ffn2_w_down[l],
                 final_g=final_norm if l == depth - 1 else None)
    return y.reshape(b, s, d)
```
